```python
import math
import jax
import jax.numpy as jnp
from jax import lax
import numpy as np


D_MODEL = 1024
BATCH = 8
SEQ = 2048
DEPTH = 4

N_MIXERS = 3
N_MEM = 256
MIX_WIDTH = (3 * D_MODEL) // 4
XATTN_HEADS = 4
XATTN_WIDTH = D_MODEL - MIX_WIDTH
XATTN_HEAD_DIM = XATTN_WIDTH // XATTN_HEADS
CAT_WIDTH = MIX_WIDTH + XATTN_WIDTH

S5_GROUP = 16
S5_GROUPS = MIX_WIDTH // S5_GROUP
S5_STATE = 64
S5_DT_MIN = 1e-3
S5_DT_MAX = 1e-1
S5_IN = MIX_WIDTH

ML_HEADS = 4
ML_QK = MIX_WIDTH // 2
ML_DQK = ML_QK // ML_HEADS
ML_DV = MIX_WIDTH // ML_HEADS
ML_CONV = 4
ML_CHUNK = 64
ML_IN = 2 * ML_QK + 2 * MIX_WIDTH + 2 * ML_HEADS

RET_HEADS = 4
RET_QK = MIX_WIDTH // 2
RET_DQK = RET_QK // RET_HEADS
RET_DV = MIX_WIDTH // RET_HEADS
RET_CHUNK = 128
RET_IN = 2 * RET_QK + 2 * MIX_WIDTH
ROPE_BASE = 10000.0

N_EXPERTS = 32
TOP_K = 4
D_EXPERT = D_MODEL
SWIGLU_LIMIT = 7.0
SWIGLU_ALPHA = 1.702
MOE_BLOCK = 256

DEEPNORM_ALPHA = (2.0 * DEPTH) ** 0.25
DEEPNORM_BETA = (8.0 * DEPTH) ** -0.25
LN_EPS = 1e-5

kernel_name = 'hybrid_s5_mlstm_retention_moe_deepnorm'


def layer_norm(h, g, b):
    hf = h.astype(jnp.float32)
    mu = jnp.mean(hf, axis=-1, keepdims=True)
    var = jnp.mean(jnp.square(hf - mu), axis=-1, keepdims=True)
    return ((hf - mu) * lax.rsqrt(var + LN_EPS) * g + b).astype(h.dtype)


def head_norm(h, g):
    hf = h.astype(jnp.float32)
    mu = jnp.mean(hf, axis=-1, keepdims=True)
    var = jnp.mean(jnp.square(hf - mu), axis=-1, keepdims=True)
    return ((hf - mu) * lax.rsqrt(var + LN_EPS)).astype(h.dtype) * g.reshape(h.shape[-2], h.shape[-1])


def to_chunks(t, chunk):
    b, l, nh, d = t.shape
    return t.reshape(b, l // chunk, chunk, nh, d).transpose(0, 3, 1, 2, 4)


def from_chunks(t):
    b, nh, nc, cl, d = t.shape
    return t.transpose(0, 2, 3, 1, 4).reshape(b, nc * cl, nh, d)


def rope_tables(positions, dim):
    inv = ROPE_BASE ** (-jnp.arange(0, dim, 2, dtype=jnp.float32) / dim)
    ang = positions.astype(jnp.float32)[..., None] * inv
    return jnp.cos(ang)[:, :, None, :], jnp.sin(ang)[:, :, None, :]


def apply_rope(t, cos, sin):
    t1, t2 = jnp.split(t, 2, axis=-1)
    return jnp.concatenate([t1 * cos - t2 * sin, t2 * cos + t1 * sin], axis=-1)


def causal_dwconv(t, w):
    return lax.conv_general_dilated(t, w.astype(t.dtype)[:, None, :], window_strides=(1,),
                                    padding=[(w.shape[0] - 1, 0)],
                                    dimension_numbers=('NWC', 'WIO', 'NWC'),
                                    feature_group_count=t.shape[-1])


def _complex_linear_combine(e1, e2):
    a1r, a1i, b1r, b1i = e1
    a2r, a2i, b2r, b2i = e2
    return (a2r * a1r - a2i * a1i, a2r * a1i + a2i * a1r,
            a2r * b1r - a2i * b1i + b2r, a2r * b1i + a2i * b1r + b2i)


def s5_mixer(u, a_re, a_im, log_dt, b_re, b_im, c_re, c_im, d_skip, w_glu, b_glu):
    bsz, seqlen, _ = u.shape
    ug = u.reshape(bsz, seqlen, S5_GROUPS, S5_GROUP).astype(jnp.float32)
    lam_re = jnp.minimum(a_re.astype(jnp.float32), -1e-4)
    lam_im = a_im.astype(jnp.float32)
    dt = jnp.exp(log_dt.astype(jnp.float32))[:, None]
    mag = jnp.exp(dt * lam_re)
    ab_re = mag * jnp.cos(dt * lam_im)
    ab_im = mag * jnp.sin(dt * lam_im)
    den = lam_re * lam_re + lam_im * lam_im
    num_re = ab_re - 1.0
    coef_re = (num_re * lam_re + ab_im * lam_im) / den
    coef_im = (ab_im * lam_re - num_re * lam_im) / den
    bre = b_re.astype(jnp.float32)
    bim = b_im.astype(jnp.float32)
    bb_re = coef_re[..., None] * bre - coef_im[..., None] * bim
    bb_im = coef_re[..., None] * bim + coef_im[..., None] * bre
    bu_re = jnp.einsum('gph,blgh->lbgp', bb_re, ug)
    bu_im = jnp.einsum('gph,blgh->lbgp', bb_im, ug)
    a_seq_re = jnp.broadcast_to(ab_re[None, None], (seqlen, 1, S5_GROUPS, S5_STATE))
    a_seq_im = jnp.broadcast_to(ab_im[None, None], (seqlen, 1, S5_GROUPS, S5_STATE))
    _, _, x_re, x_im = lax.associative_scan(_complex_linear_combine,
                                            (a_seq_re, a_seq_im, bu_re, bu_im), axis=0)
    y = (jnp.einsum('ghp,lbgp->blgh', c_re.astype(jnp.float32), x_re)
         - jnp.einsum('ghp,lbgp->blgh', c_im.astype(jnp.float32), x_im))
    y = y.reshape(bsz, seqlen, MIX_WIDTH) + d_skip * u
    y = jax.nn.gelu(y)
    return y * jax.nn.sigmoid(y @ w_glu + b_glu)


def mlstm_mixer(p, conv_q, conv_k, b_i, b_f, norm_g):
    bsz, seqlen, _ = p.shape
    q_pre, k_pre, v, o, ig_pre, fg_pre = jnp.split(
        p, [ML_QK, 2 * ML_QK, 2 * ML_QK + MIX_WIDTH, 2 * ML_QK + 2 * MIX_WIDTH,
            2 * ML_QK + 2 * MIX_WIDTH + ML_HEADS], axis=-1)
    q = jax.nn.silu(causal_dwconv(q_pre, conv_q))
    k = jax.nn.silu(causal_dwconv(k_pre, conv_k))
    qc = to_chunks(q.reshape(bsz, seqlen, ML_HEADS, ML_DQK), ML_CHUNK) * (ML_DQK ** -0.5)
    kc = to_chunks(k.reshape(bsz, seqlen, ML_HEADS, ML_DQK), ML_CHUNK)
    vc = to_chunks(v.reshape(bsz, seqlen, ML_HEADS, ML_DV), ML_CHUNK)
    ig = (ig_pre + b_i).astype(jnp.float32)
    lf = jax.nn.log_sigmoid((fg_pre + b_f).astype(jnp.float32))
    igc = to_chunks(ig[..., None], ML_CHUNK)[..., 0]
    lfc = to_chunks(lf[..., None], ML_CHUNK)[..., 0]
    bcum = jnp.cumsum(lfc, axis=-1)
    btot = bcum[..., -1]
    w = btot[..., None] - bcum + igc
    m_loc = jnp.max(w, axis=-1)
    e = jnp.exp(w - m_loc[..., None])
    kv = jnp.einsum('bhcs,bhcsk,bhcsv->bhckv', e, kc, vc).astype(jnp.float32)
    nk = jnp.einsum('bhcs,bhcsk->bhck', e, kc).astype(jnp.float32)

    def step(carry, inp):
        c_st, n_st, m_st = carry
        kv_c, nk_c, mloc_c, btot_c = inp
        m_new = jnp.maximum(btot_c + m_st, mloc_c)
        sa = jnp.exp(btot_c + m_st - m_new)
        sb = jnp.exp(mloc_c - m_new)
        c_new = sa[..., None, None] * c_st + sb[..., None, None] * kv_c
        n_new = sa[..., None] * n_st + sb[..., None] * nk_c
        return (c_new, n_new, m_new), (c_st, n_st, m_st)

    init = (jnp.zeros((bsz, ML_HEADS, ML_DQK, ML_DV), jnp.float32),
            jnp.zeros((bsz, ML_HEADS, ML_DQK), jnp.float32),
            jnp.zeros((bsz, ML_HEADS), jnp.float32))
    _, (c_prev, n_prev, m_prev) = lax.scan(
        step, init, (jnp.moveaxis(kv, 2, 0), jnp.moveaxis(nk, 2, 0),
                     jnp.moveaxis(m_loc, 2, 0), jnp.moveaxis(btot, 2, 0)))
    c_prev = jnp.moveaxis(c_prev, 0, 2)
    n_prev = jnp.moveaxis(n_prev, 0, 2)
    m_prev = jnp.moveaxis(m_prev, 0, 2)
    idx = jnp.arange(ML_CHUNK)
    causal = idx[:, None] >= idx[None, :]
    dmat = jnp.where(causal, bcum[..., :, None] - bcum[..., None, :] + igc[..., None, :], -jnp.inf)
    g = bcum + m_prev[..., None]
    m_row = jnp.maximum(g, jnp.max(dmat, axis=-1))
    inter = jnp.exp(g - m_row)
    s_qk = jnp.einsum('bhcjd,bhcsd->bhcjs', qc, kc).astype(jnp.float32) * jnp.exp(dmat - m_row[..., None])
    num = (inter[..., None] * jnp.einsum('bhcjd,bhcdv->bhcjv', qc, c_prev)
           + jnp.einsum('bhcjs,bhcsv->bhcjv', s_qk, vc))
    den = inter * jnp.einsum('bhcjd,bhcd->bhcj', qc, n_prev) + jnp.sum(s_qk, axis=-1)
    h = num / jnp.maximum(jnp.abs(den), jnp.exp(-m_row))[..., None]
    h = head_norm(from_chunks(h), norm_g).reshape(bsz, seqlen, MIX_WIDTH)
    return jax.nn.sigmoid(o) * h


def retention_mixer(p, cos, sin, norm_g):
    bsz, seqlen, _ = p.shape
    q, k, v, gate = jnp.split(p, [RET_QK, 2 * RET_QK, 2 * RET_QK + MIX_WIDTH], axis=-1)
    q = apply_rope(q.reshape(bsz, seqlen, RET_HEADS, RET_DQK), cos, sin)
    k = apply_rope(k.reshape(bsz, seqlen, RET_HEADS, RET_DQK), cos, sin) * (RET_DQK ** -0.5)
    qc = to_chunks(q, RET_CHUNK)
    kc = to_chunks(k, RET_CHUNK)
    vc = to_chunks(v.reshape(bsz, seqlen, RET_HEADS, RET_DV), RET_CHUNK)
    log_gamma = jnp.log(1.0 - jnp.power(2.0, -5.0 - jnp.arange(RET_HEADS, dtype=jnp.float32)))
    idx = jnp.arange(RET_CHUNK, dtype=jnp.float32)
    rel = idx[:, None] - idx[None, :]
    decay_intra = jnp.where(rel >= 0, jnp.exp(jnp.maximum(rel, 0.0) * log_gamma[:, None, None]), 0.0)
    s = jnp.einsum('bhcjd,bhcsd->bhcjs', qc, kc) * decay_intra[:, None]
    intra = jnp.einsum('bhcjs,bhcsv->bhcjv', s, vc)
    zeta = jnp.exp((RET_CHUNK - 1 - idx) * log_gamma[:, None])
    r = jnp.einsum('bhcsk,hs,bhcsv->bhckv', kc, zeta, vc).astype(jnp.float32)
    chunk_decay = jnp.exp(RET_CHUNK * log_gamma)[None, :, None, None]

    def step(state, r_c):
        return chunk_decay * state + r_c, state

    _, s_prev = lax.scan(step, jnp.zeros((bsz, RET_HEADS, RET_DQK, RET_DV), jnp.float32),
                         jnp.moveaxis(r, 2, 0))
    s_prev = jnp.moveaxis(s_prev, 0, 2)
    xi = jnp.exp((idx + 1.0) * log_gamma[:, None])
    cross = jnp.einsum('bhcjk,bhckv->bhcjv', qc, s_prev) * xi[:, None, :, None]
    y = head_norm(from_chunks(intra + cross), norm_g).reshape(bsz, seqlen, MIX_WIDTH)
    return jax.nn.silu(gate) * y


def memory_cross_attention(xq, mem_k, mem_v):
    bsz, seqlen, _ = xq.shape
    q = xq.reshape(bsz, seqlen, XATTN_HEADS, XATTN_HEAD_DIM) * (XATTN_HEAD_DIM ** -0.5)
    scores = jnp.einsum('blhd,bmhd->bhlm', q, mem_k).astype(jnp.float32)
    probs = jax.nn.softmax(scores, axis=-1).astype(mem_v.dtype)
    return jnp.einsum('bhlm,bmhd->blhd', probs, mem_v).reshape(bsz, seqlen, XATTN_WIDTH)


def moe_ffn(h, router_w, router_b, w_gu, b_gu, w_down, b_down):
    bsz, seqlen, d = h.shape
    t = bsz * seqlen
    xt = h.reshape(t, d)
    logits = (xt @ router_w + router_b).astype(jnp.float32)
    top_val, top_idx = lax.top_k(logits, TOP_K)
    gates = jax.nn.softmax(top_val, axis=-1)
    n_assign = t * TOP_K
    e_flat = top_idx.reshape(n_assign).astype(jnp.int32)
    g_flat = gates.reshape(n_assign)
    tok_flat = jnp.arange(n_assign, dtype=jnp.int32) // TOP_K
    counts = jax.ops.segment_sum(jnp.ones((n_assign,), jnp.int32), e_flat, num_segments=N_EXPERTS)
    padded = ((counts + MOE_BLOCK - 1) // MOE_BLOCK) * MOE_BLOCK
    pad_end = jnp.cumsum(padded)
    pad_start = pad_end - padded
    raw_start = jnp.cumsum(counts) - counts
    order = jnp.argsort(e_flat)
    se = e_flat[order]
    dest = pad_start[se] + (jnp.arange(n_assign, dtype=jnp.int32) - raw_start[se])
    n_rows = (-(-n_assign // MOE_BLOCK) + N_EXPERTS) * MOE_BLOCK
    n_blocks = n_rows // MOE_BLOCK
    row_tok = jnp.zeros((n_rows,), jnp.int32).at[dest].set(tok_flat[order])
    row_gate = jnp.zeros((n_rows,), jnp.float32).at[dest].set(g_flat[order])
    block_start = jnp.arange(n_blocks, dtype=jnp.int32) * MOE_BLOCK
    block_exp = jnp.minimum(jnp.sum((block_start[:, None] >= pad_end[None, :]).astype(jnp.int32), axis=1),
                            N_EXPERTS - 1)
    xr = xt[row_tok].reshape(n_blocks, MOE_BLOCK, d)

    def expert_block(args):
        xb, e = args
        gu = xb @ w_gu[e] + b_gu[e]
        x_glu = jnp.minimum(gu[..., :D_EXPERT], SWIGLU_LIMIT)
        x_lin = jnp.clip(gu[..., D_EXPERT:], -SWIGLU_LIMIT, SWIGLU_LIMIT)
        act = x_glu * jax.nn.sigmoid(SWIGLU_ALPHA * x_glu) * (x_lin + 1.0)
        return act @ w_down[e] + b_down[e]

    yr = lax.map(expert_block, (xr, block_exp)).reshape(n_rows, d)
    out = jnp.zeros((t, d), yr.dtype).at[row_tok].add(row_gate[:, None].astype(yr.dtype) * yr)
    return out.reshape(bsz, seqlen, d)


def _normal(k, shape, scale):
    return jax.random.normal(k, shape, jnp.float32) * scale


def _s5_params(key, prefix):
    k = jax.random.split(key, 11)
    n = jnp.arange(S5_STATE, dtype=jnp.float32)[None, :]
    gp = (S5_GROUPS, S5_STATE)
    return {
        prefix + 'w_in': _normal(k[0], (D_MODEL, S5_IN + XATTN_WIDTH), D_MODEL ** -0.5),
        prefix + 's5_a_re': -0.5 + _normal(k[1], gp, 0.01),
        prefix + 's5_a_im': math.pi * n + _normal(k[2], gp, 0.01),
        prefix + 's5_log_dt': jax.random.uniform(k[3], (S5_GROUPS,), jnp.float32,
                                                 minval=math.log(S5_DT_MIN), maxval=math.log(S5_DT_MAX)),
        prefix + 's5_b_re': _normal(k[4], (S5_GROUPS, S5_STATE, S5_GROUP), (2 * S5_GROUP) ** -0.5),
        prefix + 's5_b_im': _normal(k[5], (S5_GROUPS, S5_STATE, S5_GROUP), (2 * S5_GROUP) ** -0.5),
        prefix + 's5_c_re': _normal(k[6], (S5_GROUPS, S5_GROUP, S5_STATE), S5_STATE ** -0.5),
        prefix + 's5_c_im': _normal(k[7], (S5_GROUPS, S5_GROUP, S5_STATE), S5_STATE ** -0.5),
        prefix + 's5_d': 1.0 + _normal(k[8], (MIX_WIDTH,), 0.1),
        prefix + 's5_w_glu': _normal(k[9], (MIX_WIDTH, MIX_WIDTH), MIX_WIDTH ** -0.5),
        prefix + 's5_b_glu': _normal(k[10], (MIX_WIDTH,), 0.01),
    }


def setup_inputs(seed: int = 0) -> dict:
    key = jax.random.key(seed)
    k = jax.random.split(key, 32)
    inputs = {
        'x': _normal(k[0], (BATCH, SEQ, D_MODEL), 1.0),
        'mem': _normal(k[1], (BATCH, N_MEM, D_MODEL), 1.0),
        'positions': (jnp.arange(SEQ, dtype=jnp.int32)[None, :]
                      + jax.random.randint(k[2], (BATCH, 1), 0, 4096, dtype=jnp.int32)),
        'mem_w_k': _normal(k[3], (D_MODEL, XATTN_WIDTH), D_MODEL ** -0.5),
        'mem_w_v': _normal(k[4], (D_MODEL, XATTN_WIDTH), D_MODEL ** -0.5),
    }
    inputs.update(_s5_params(k[5], 'l0_'))
    inputs.update({
        'l1_w_in': _normal(k[6], (D_MODEL, ML_IN + XATTN_WIDTH), D_MODEL ** -0.5),
        'l1_ml_conv_q': _normal(k[7], (ML_CONV, ML_QK), ML_CONV ** -0.5),
        'l1_ml_conv_k': _normal(k[8], (ML_CONV, ML_QK), ML_CONV ** -0.5),
        'l1_ml_b_i': _normal(k[9], (ML_HEADS,), 0.1),
        'l1_ml_b_f': jnp.linspace(3.0, 6.0, ML_HEADS, dtype=jnp.float32) + _normal(k[10], (ML_HEADS,), 0.1),
        'l1_ml_norm_g': 1.0 + _normal(k[11], (MIX_WIDTH,), 0.1),
        'l2_w_in': _normal(k[12], (D_MODEL, RET_IN + XATTN_WIDTH), D_MODEL ** -0.5),
        'l2_ret_norm_g': 1.0 + _normal(k[13], (MIX_WIDTH,), 0.1),
    })
    inputs.update(_s5_params(k[14], 'l3_'))
    inputs.update({
        'w_out': _normal(k[15], (DEPTH, CAT_WIDTH, D_MODEL), CAT_WIDTH ** -0.5 * DEEPNORM_BETA),
        'ln1_g': 1.0 + _normal(k[16], (DEPTH, D_MODEL), 0.1),
        'ln1_b': _normal(k[17], (DEPTH, D_MODEL), 0.01),
        'ln2_g': 1.0 + _normal(k[18], (DEPTH, D_MODEL), 0.1),
        'ln2_b': _normal(k[19], (DEPTH, D_MODEL), 0.01),
        'router_w': _normal(k[20], (DEPTH, D_MODEL, N_EXPERTS), D_MODEL ** -0.5),
        'router_b': _normal(k[21], (DEPTH, N_EXPERTS), 0.01),
        'exp_w_gu': _normal(k[22], (DEPTH, N_EXPERTS, D_MODEL, 2 * D_EXPERT), D_MODEL ** -0.5),
        'exp_b_gu': _normal(k[23], (DEPTH, N_EXPERTS, 2 * D_EXPERT), 0.01),
        'exp_w_down': _normal(k[24], (DEPTH, N_EXPERTS, D_EXPERT, D_MODEL), D_EXPERT ** -0.5 * DEEPNORM_BETA),
        'exp_b_down': _normal(k[25], (DEPTH, N_EXPERTS, D_MODEL), 0.01),
    })
    return inputs


def reference(x, mem, positions, mem_w_k, mem_w_v,
              l0_w_in, l0_s5_a_re, l0_s5_a_im, l0_s5_log_dt, l0_s5_b_re, l0_s5_b_im,
              l0_s5_c_re, l0_s5_c_im, l0_s5_d, l0_s5_w_glu, l0_s5_b_glu,
              l1_w_in, l1_ml_conv_q, l1_ml_conv_k, l1_ml_b_i, l1_ml_b_f, l1_ml_norm_g,
              l2_w_in, l2_ret_norm_g,
              l3_w_in, l3_s5_a_re, l3_s5_a_im, l3_s5_log_dt, l3_s5_b_re, l3_s5_b_im,
              l3_s5_c_re, l3_s5_c_im, l3_s5_d, l3_s5_w_glu, l3_s5_b_glu,
              w_out, ln1_g, ln1_b, ln2_g, ln2_b, router_w, router_b,
              exp_w_gu, exp_b_gu, exp_w_down, exp_b_down):
    bsz, seqlen, _ = x.shape
    cos, sin = rope_tables(positions, RET_DQK)
    mem_k = (mem @ mem_w_k).reshape(bsz, N_MEM, XATTN_HEADS, XATTN_HEAD_DIM)
    mem_v = (mem @ mem_w_v).reshape(bsz, N_MEM, XATTN_HEADS, XATTN_HEAD_DIM)
    w_ins = (l0_w_in, l1_w_in, l2_w_in, l3_w_in)
    mixer_params = (
        (l0_s5_a_re, l0_s5_a_im, l0_s5_log_dt, l0_s5_b_re, l0_s5_b_im,
         l0_s5_c_re, l0_s5_c_im, l0_s5_d, l0_s5_w_glu, l0_s5_b_glu),
        (l1_ml_conv_q, l1_ml_conv_k, l1_ml_b_i, l1_ml_b_f, l1_ml_norm_g),
        (l2_ret_norm_g,),
        (l3_s5_a_re, l3_s5_a_im, l3_s5_log_dt, l3_s5_b_re, l3_s5_b_im,
         l3_s5_c_re, l3_s5_c_im, l3_s5_d, l3_s5_w_glu, l3_s5_b_glu),
    )
    h = x
    for i in range(DEPTH):
        kind = i % N_MIXERS
        proj = h @ w_ins[i]
        p_mix = proj[..., :-XATTN_WIDTH]
        xq = proj[..., -XATTN_WIDTH:]
        if kind == 0:
            y_mix = s5_mixer(p_mix, *mixer_params[i])
        elif kind == 1:
            y_mix = mlstm_mixer(p_mix, *mixer_params[i])
        else:
            y_mix = retention_mixer(p_mix, cos, sin, *mixer_params[i])
        y_mem = memory_cross_attention(xq, mem_k, mem_v)
        y = jnp.concatenate([y_mix.astype(y_mem.dtype), y_mem], axis=-1) @ w_out[i]
        h = layer_norm(DEEPNORM_ALPHA * h + y, ln1_g[i], ln1_b[i])
        y = moe_ffn(h, router_w[i], router_b[i], exp_w_gu[i], exp_b_gu[i], exp_w_down[i], exp_b_down[i])
        h = layer_norm(DEEPNORM_ALPHA * h + y, ln2_g[i], ln2_b[i])
    return h
```

```python
import functools
import math

import numpy as np
import jax
import jax.numpy as jnp
from jax import lax
from jax.experimental import pallas as pl
from jax.experimental.pallas import tpu as pltpu

F32 = jnp.float32
BF16 = jnp.bfloat16

D_MODEL = 1024
DEPTH = 4
N_MEM = 256
MIX_WIDTH = 768
XATTN_HEADS = 4
XATTN_WIDTH = 256
XATTN_HEAD_DIM = 64
S5_GROUP = 16
S5_GROUPS = 48
S5_STATE = 64
N_HEADS = 4
DQK = 96
DV = 192
ML_QK = 384
ML_CONV = 4
ML_CHUNK = 64
RET_CHUNK = 128
ROPE_BASE = 10000.0
N_EXPERTS = 32
TOP_K = 4
SWIGLU_LIMIT = 7.0
SWIGLU_ALPHA = 1.702
DEEPNORM_ALPHA = (2.0 * DEPTH) ** 0.25
LN_EPS = 1e-5

LANES = 128
SUBLANES = 8
DQK_PAD = 128
DV_PAD = 256
HEADS_QK = N_HEADS * DQK_PAD
HEADS_V = N_HEADS * DV_PAD
S5_LC = 16
S5_K = S5_LC * S5_GROUP
MOE_BM = 256
ROW_TILE = 256
VMEM_LIMIT = 56 * 1024 * 1024

_NT = (((1,), (1,)), ((), ()))
_TN = (((0,), (0,)), ((), ()))


def _cparams(sem):
    return pltpu.CompilerParams(dimension_semantics=sem, vmem_limit_bytes=VMEM_LIMIT)


def _dot(a, b):
    return jnp.dot(a, b, preferred_element_type=F32)


def _layer_norm_rows(z, g, b):
    mu = jnp.mean(z, axis=-1, keepdims=True)
    d = z - mu
    var = jnp.mean(d * d, axis=-1, keepdims=True)
    return d * lax.rsqrt(var + LN_EPS) * g + b


def _inproj_kernel(x_ref, w_ref, o_ref):
    xb = x_ref[...].astype(BF16)
    n = o_ref.shape[1]
    step = 512
    for c0 in range(0, n, step):
        c1 = min(c0 + step, n)
        o_ref[:, c0:c1] = _dot(xb, w_ref[:, c0:c1])


def _inproj(x, w_bf16, tm=ROW_TILE):
    t, d = x.shape
    n = w_bf16.shape[1]
    return pl.pallas_call(
        _inproj_kernel,
        out_shape=jax.ShapeDtypeStruct((t, n), F32),
        grid=(t // tm,),
        in_specs=[pl.BlockSpec((tm, d), lambda i: (i, 0)),
                  pl.BlockSpec((d, n), lambda i: (0, 0))],
        out_specs=pl.BlockSpec((tm, n), lambda i: (i, 0)),
        compiler_params=_cparams(("parallel",)),
        name="inproj",
    )(x, w_bf16)


def _s5_prep(a_re, a_im, log_dt, b_re, b_im, c_re, c_im):
    hp = lax.Precision.HIGHEST
    lam_re = jnp.minimum(a_re.astype(F32), -1e-4)
    lam_im = a_im.astype(F32)
    dt = jnp.exp(log_dt.astype(F32))[:, None]
    mag = jnp.exp(dt * lam_re)
    ab_re = mag * jnp.cos(dt * lam_im)
    ab_im = mag * jnp.sin(dt * lam_im)
    den = lam_re * lam_re + lam_im * lam_im
    num_re = ab_re - 1.0
    coef_re = (num_re * lam_re + ab_im * lam_im) / den
    coef_im = (ab_im * lam_re - num_re * lam_im) / den
    bre = b_re.astype(F32)
    bim = b_im.astype(F32)
    bb_re = coef_re[..., None] * bre - coef_im[..., None] * bim
    bb_im = coef_re[..., None] * bim + coef_im[..., None] * bre
    pr = [jnp.ones_like(ab_re)]
    pi = [jnp.zeros_like(ab_im)]
    for _ in range(S5_LC):
        r, i = pr[-1], pi[-1]
        pr.append(r * ab_re - i * ab_im)
        pi.append(r * ab_im + i * ab_re)
    pw_re = jnp.stack(pr)
    pw_im = jnp.stack(pi)
    p_re = pw_re[:S5_LC, :, :, None] * bb_re[None] - pw_im[:S5_LC, :, :, None] * bb_im[None]
    p_im = pw_re[:S5_LC, :, :, None] * bb_im[None] + pw_im[:S5_LC, :, :, None] * bb_re[None]
    cre = c_re.astype(F32)
    cim = c_im.astype(F32)
    kmat = (jnp.einsum('ghp,tgpk->gthk', cre, p_re, precision=hp)
            - jnp.einsum('ghp,tgpk->gthk', cim, p_im, precision=hp))
    s_idx = jnp.arange(S5_LC)[:, None]
    j_idx = jnp.arange(S5_LC)[None, :]
    tau = j_idx - s_idx
    kg = kmat[:, jnp.clip(tau, 0, S5_LC - 1)]
    kg = jnp.where((tau >= 0)[None, :, :, None, None], kg, 0.0)
    t_mat = kg.transpose(0, 1, 4, 2, 3).reshape(S5_GROUPS, S5_K, S5_K)
    w1_re = p_re[::-1].transpose(1, 0, 3, 2).reshape(S5_GROUPS, S5_K, S5_STATE)
    w1_im = p_im[::-1].transpose(1, 0, 3, 2).reshape(S5_GROUPS, S5_K, S5_STATE)
    ar = pw_re[1:].transpose(1, 2, 0)[:, :, :, None]
    ai = pw_im[1:].transpose(1, 2, 0)[:, :, :, None]
    crt = cre.transpose(0, 2, 1)[:, :, None, :]
    cit = cim.transpose(0, 2, 1)[:, :, None, :]
    w3_re = (crt * ar - cit * ai).reshape(S5_GROUPS, S5_STATE, S5_K)
    w3_im = (-crt * ai - cit * ar).reshape(S5_GROUPS, S5_STATE, S5_K)
    al_re = pw_re[S5_LC][:, None, :]
    al_im = pw_im[S5_LC][:, None, :]
    return (t_mat.astype(BF16), w1_re.astype(BF16), w1_im.astype(BF16),
            w3_re.astype(BF16), w3_im.astype(BF16), al_re, al_im)


def _s5_kernel(u_ref, t_ref, w1r_ref, w1i_ref, w3r_ref, w3i_ref, alr_ref, ali_ref,
               y_ref, sre, sim, xpr, xpi, *, n_sub, bsz):
    ub = u_ref[0].astype(BF16)
    sre[...] = _dot(ub, w1r_ref[0])
    sim[...] = _dot(ub, w1i_ref[0])
    ar = alr_ref[0]
    ai = ali_ref[0]

    def step(c, carry):
        xr, xi = carry
        r0 = pl.multiple_of(c * bsz, bsz)
        xpr[pl.ds(r0, bsz), :] = xr
        xpi[pl.ds(r0, bsz), :] = xi
        sr = sre[pl.ds(r0, bsz), :]
        si = sim[pl.ds(r0, bsz), :]
        return ar * xr - ai * xi + sr, ar * xi + ai * xr + si

    z = jnp.zeros((bsz, S5_STATE), F32)
    lax.fori_loop(0, n_sub, step, (z, z))
    y_ref[0] = (_dot(ub, t_ref[0]) + _dot(xpr[...].astype(BF16), w3r_ref[0])
                + _dot(xpi[...].astype(BF16), w3i_ref[0]))


def _s5_scan(u_t, mats, n_sub, bsz):
    t_mat, w1r, w1i, w3r, w3i, alr, ali = mats
    g, r, k = u_t.shape
    blk3 = lambda a, b: pl.BlockSpec((1, a, b), lambda i: (i, 0, 0))
    return pl.pallas_call(
        functools.partial(_s5_kernel, n_sub=n_sub, bsz=bsz),
        out_shape=jax.ShapeDtypeStruct((g, r, k), F32),
        grid=(g,),
        in_specs=[blk3(r, k), blk3(k, k), blk3(k, S5_STATE), blk3(k, S5_STATE),
                  blk3(S5_STATE, k), blk3(S5_STATE, k), blk3(1, S5_STATE), blk3(1, S5_STATE)],
        out_specs=blk3(r, k),
        scratch_shapes=[pltpu.VMEM((r, S5_STATE), F32)] * 4,
        compiler_params=_cparams(("parallel",)),
        name="s5_scan",
    )(u_t, t_mat, w1r, w1i, w3r, w3i, alr, ali)


def _s5_post_kernel(y_ref, u_ref, d_ref, w_ref, b_ref, o_ref):
    y = y_ref[...] + d_ref[...] * u_ref[...]
    y = jax.nn.gelu(y)
    o_ref[...] = y * jax.nn.sigmoid(_dot(y.astype(BF16), w_ref[...]) + b_ref[...])


def _s5_post(y_ssm, proj, d_skip, w_glu_bf16, b_glu, tm=ROW_TILE):
    t = y_ssm.shape[0]
    w = MIX_WIDTH
    return pl.pallas_call(
        _s5_post_kernel,
        out_shape=jax.ShapeDtypeStruct((t, w), F32),
        grid=(t // tm,),
        in_specs=[pl.BlockSpec((tm, w), lambda i: (i, 0)),
                  pl.BlockSpec((tm, w), lambda i: (i, 0)),
                  pl.BlockSpec((1, w), lambda i: (0, 0)),
                  pl.BlockSpec((w, w), lambda i: (0, 0)),
                  pl.BlockSpec((1, w), lambda i: (0, 0))],
        out_specs=pl.BlockSpec((tm, w), lambda i: (i, 0)),
        compiler_params=_cparams(("parallel",)),
        name="s5_post",
    )(y_ssm, proj, d_skip.reshape(1, w), w_glu_bf16, b_glu.reshape(1, w))


def _s5_mixer(proj, bsz, seqlen, a_re, a_im, log_dt, b_re, b_im, c_re, c_im, d_skip, w_glu, b_glu):
    mats = _s5_prep(a_re, a_im, log_dt, b_re, b_im, c_re, c_im)
    n_sub = seqlen // S5_LC
    u = proj[:, :MIX_WIDTH]
    u_t = (u.reshape(bsz, n_sub, S5_LC, S5_GROUPS, S5_GROUP)
           .transpose(3, 1, 0, 2, 4).reshape(S5_GROUPS, n_sub * bsz, S5_K))
    y_t = _s5_scan(u_t, mats, n_sub, bsz)
    y = (y_t.reshape(S5_GROUPS, n_sub, bsz, S5_LC, S5_GROUP)
         .transpose(2, 1, 3, 0, 4).reshape(bsz * seqlen, MIX_WIDTH))
    return _s5_post(y, proj, d_skip, w_glu.astype(BF16), b_glu)


def _head_norm_padded(hv, g):
    lane = lax.broadcasted_iota(jnp.int32, hv.shape, 1)
    real = lane < DV
    mu = jnp.sum(hv, axis=-1, keepdims=True) * (1.0 / DV)
    d = jnp.where(real, hv - mu, 0.0)
    var = jnp.sum(d * d, axis=-1, keepdims=True) * (1.0 / DV)
    return d * lax.rsqrt(var + LN_EPS) * g


def _log_sigmoid(x):
    return jnp.minimum(x, 0.0) - jnp.log(1.0 + jnp.exp(-jnp.abs(x)))


def _mlstm_kernel(q_ref, k_ref, v_ref, o_ref, gt_ref, cw_ref, gb_ref, ng_ref,
                  y_ref, cbuf, c_st, n_st, m_st, *, tl):
    i = pl.program_id(1)

    @pl.when(i == 0)
    def _():
        cbuf[0:SUBLANES, :] = jnp.zeros((SUBLANES, 2 * HEADS_QK), F32)
        c_st[...] = jnp.zeros_like(c_st)
        n_st[...] = jnp.zeros_like(n_st)
        m_st[...] = jnp.zeros_like(m_st)

    cbuf[SUBLANES:SUBLANES + tl, 0:HEADS_QK] = q_ref[...]
    cbuf[SUBLANES:SUBLANES + tl, HEADS_QK:2 * HEADS_QK] = k_ref[...]
    acc = jnp.zeros((tl, 2 * HEADS_QK), F32)
    for w in range(ML_CONV):
        acc = acc + cbuf[pl.ds(SUBLANES - (ML_CONV - 1) + w, tl), :] * cw_ref[w:w + 1, :]
    qk = acc * jax.nn.sigmoid(acc)
    cbuf[0:SUBLANES, :] = cbuf[tl:tl + SUBLANES, :]

    gt = gt_ref[...] + gb_ref[...]
    lf = _log_sigmoid(gt)
    gt_t = gt.T
    lf_t = _log_sigmoid(gt_t)

    cl = ML_CHUNK
    row = lax.broadcasted_iota(jnp.int32, (cl, cl), 0)
    col = lax.broadcasted_iota(jnp.int32, (cl, cl), 1)
    tri = row >= col
    scale = DQK ** -0.5
    for cc in range(tl // cl):
        r0 = cc * cl
        for h in range(N_HEADS):
            ig_col = gt[r0:r0 + cl, h:h + 1]
            lf_col = lf[r0:r0 + cl, N_HEADS + h:N_HEADS + h + 1]
            ig_row = gt_t[h:h + 1, r0:r0 + cl]
            lf_row = lf_t[N_HEADS + h:N_HEADS + h + 1, r0:r0 + cl]
            bcum_col = jnp.sum(jnp.where(tri, lf_row, 0.0), axis=1, keepdims=True)
            bcum_row = jnp.sum(jnp.where(col >= row, lf_col, 0.0), axis=0, keepdims=True)
            btot = jnp.sum(lf_row, axis=1, keepdims=True)
            w_row = btot - bcum_row + ig_row
            m_loc = jnp.max(w_row, axis=1, keepdims=True)
            e_col = jnp.exp(btot - bcum_col + ig_col - m_loc)
            m_prev = m_st[h:h + 1, 0:1]
            c_prev = c_st[h]
            n_prev = n_st[h:h + 1, :]
            q = qk[r0:r0 + cl, h * DQK_PAD:(h + 1) * DQK_PAD] * scale
            k = qk[r0:r0 + cl, HEADS_QK + h * DQK_PAD:HEADS_QK + (h + 1) * DQK_PAD]
            v = v_ref[r0:r0 + cl, h * DV_PAD:(h + 1) * DV_PAD]
            qb = q.astype(BF16)
            kb = k.astype(BF16)
            vb = v.astype(BF16)
            dmat = jnp.where(tri, bcum_col - bcum_row + ig_row, -jnp.inf)
            g_col = bcum_col + m_prev
            m_row = jnp.maximum(g_col, jnp.max(dmat, axis=1, keepdims=True))
            inter = jnp.exp(g_col - m_row)
            s_qk = lax.dot_general(qb, kb, _NT, preferred_element_type=F32) * jnp.exp(dmat - m_row)
            num = inter * _dot(qb, c_prev.astype(BF16)) + _dot(s_qk.astype(BF16), vb)
            den = (inter * jnp.sum(q * n_prev, axis=1, keepdims=True)
                   + jnp.sum(s_qk, axis=1, keepdims=True))
            hv = num / jnp.maximum(jnp.abs(den), jnp.exp(-m_row))
            ke = k * e_col
            kv = lax.dot_general(ke.astype(BF16), vb, _TN, preferred_element_type=F32)
            nk = jnp.sum(ke, axis=0, keepdims=True)
            m_new = jnp.maximum(btot + m_prev, m_loc)
            sa = jnp.exp(btot + m_prev - m_new)
            sb = jnp.exp(m_loc - m_new)
            c_st[h] = sa * c_prev + sb * kv
            n_st[h:h + 1, :] = sa * n_prev + sb * nk
            m_st[h:h + 1, :] = jnp.broadcast_to(m_new, (1, LANES))
            hn = _head_norm_padded(hv, ng_ref[0:1, h * DV_PAD:(h + 1) * DV_PAD])
            og = o_ref[r0:r0 + cl, h * DV_PAD:(h + 1) * DV_PAD]
            y_ref[r0:r0 + cl, h * DV_PAD:(h + 1) * DV_PAD] = jax.nn.sigmoid(og) * hn


def _mlstm_mixer(proj, bsz, seqlen, cw, gbias, norm_g, tl=ROW_TILE):
    t = proj.shape[0]
    nl = seqlen // tl
    rows = lambda b, i: b * nl + i
    return pl.pallas_call(
        functools.partial(_mlstm_kernel, tl=tl),
        out_shape=jax.ShapeDtypeStruct((t, HEADS_V), F32),
        grid=(bsz, nl),
        in_specs=[pl.BlockSpec((tl, HEADS_QK), lambda b, i: (rows(b, i), 0)),
                  pl.BlockSpec((tl, HEADS_QK), lambda b, i: (rows(b, i), 1)),
                  pl.BlockSpec((tl, HEADS_V), lambda b, i: (rows(b, i), 1)),
                  pl.BlockSpec((tl, HEADS_V), lambda b, i: (rows(b, i), 2)),
                  pl.BlockSpec((tl, LANES), lambda b, i: (rows(b, i), 26)),
                  pl.BlockSpec((ML_CONV, 2 * HEADS_QK), lambda b, i: (0, 0)),
                  pl.BlockSpec((1, LANES), lambda b, i: (0, 0)),
                  pl.BlockSpec((1, HEADS_V), lambda b, i: (0, 0))],
        out_specs=pl.BlockSpec((tl, HEADS_V), lambda b, i: (rows(b, i), 0)),
        scratch_shapes=[pltpu.VMEM((tl + SUBLANES, 2 * HEADS_QK), F32),
                        pltpu.VMEM((N_HEADS, DQK_PAD, DV_PAD), F32),
                        pltpu.VMEM((SUBLANES, DQK_PAD), F32),
                        pltpu.VMEM((SUBLANES, LANES), F32)],
        compiler_params=_cparams(("parallel", "arbitrary")),
        name="mlstm",
    )(proj, proj, proj, proj, proj, cw, gbias, norm_g)


def _ret_log_gamma(h):
    return float(np.log(np.float32(1.0) - np.power(np.float32(2.0), np.float32(-5.0 - h))))


def _ret_kernel(q_ref, k_ref, v_ref, g_ref, pos_ref, inv_ref, sgn_ref, ng_ref,
                y_ref, s_st, *, tl):
    i = pl.program_id(1)

    @pl.when(i == 0)
    def _():
        s_st[...] = jnp.zeros_like(s_st)

    ang = pos_ref[...] * inv_ref[...]
    cos_t = jnp.cos(ang)
    sin_t = jnp.sin(ang) * sgn_ref[...]
    cl = RET_CHUNK
    row = lax.broadcasted_iota(jnp.int32, (cl, cl), 0)
    col = lax.broadcasted_iota(jnp.int32, (cl, cl), 1)
    rel = (row - col).astype(F32)
    jcol = lax.broadcasted_iota(jnp.int32, (cl, 1), 0).astype(F32)
    kscale = DQK ** -0.5
    for h in range(N_HEADS):
        lg = _ret_log_gamma(h)
        decay = jnp.where(rel >= 0, jnp.exp(jnp.maximum(rel, 0.0) * lg), 0.0)
        zeta = jnp.exp((cl - 1 - jcol) * lg)
        xi = jnp.exp((jcol + 1.0) * lg)
        chunk_decay = float(np.exp(np.float32(cl) * np.float32(lg)))
        qh = q_ref[:, h * DQK_PAD:(h + 1) * DQK_PAD]
        kh = k_ref[:, h * DQK_PAD:(h + 1) * DQK_PAD]
        qh = qh * cos_t + pltpu.roll(qh, DQK_PAD // 2, 1) * sin_t
        kh = (kh * cos_t + pltpu.roll(kh, DQK_PAD // 2, 1) * sin_t) * kscale
        for cc in range(tl // cl):
            r0 = cc * cl
            qb = qh[r0:r0 + cl].astype(BF16)
            k = kh[r0:r0 + cl]
            kb = k.astype(BF16)
            v = v_ref[r0:r0 + cl, h * DV_PAD:(h + 1) * DV_PAD]
            vb = v.astype(BF16)
            s_prev = s_st[h]
            s = lax.dot_general(qb, kb, _NT, preferred_element_type=F32) * decay
            intra = _dot(s.astype(BF16), vb)
            cross = _dot(qb, s_prev.astype(BF16)) * xi
            r = lax.dot_general((k * zeta).astype(BF16), vb, _TN, preferred_element_type=F32)
            s_st[h] = chunk_decay * s_prev + r
            hn = _head_norm_padded(intra + cross, ng_ref[0:1, h * DV_PAD:(h + 1) * DV_PAD])
            gate = g_ref[r0:r0 + cl, h * DV_PAD:(h + 1) * DV_PAD]
            y_ref[r0:r0 + cl, h * DV_PAD:(h + 1) * DV_PAD] = gate * jax.nn.sigmoid(gate) * hn


def _ret_mixer(proj, bsz, seqlen, pos_f, inv_pad, sgn_pad, norm_g, tl=ROW_TILE):
    t = proj.shape[0]
    nl = seqlen // tl
    rows = lambda b, i: b * nl + i
    return pl.pallas_call(
        functools.partial(_ret_kernel, tl=tl),
        out_shape=jax.ShapeDtypeStruct((t, HEADS_V), F32),
        grid=(bsz, nl),
        in_specs=[pl.BlockSpec((tl, HEADS_QK), lambda b, i: (rows(b, i), 0)),
                  pl.BlockSpec((tl, HEADS_QK), lambda b, i: (rows(b, i), 1)),
                  pl.BlockSpec((tl, HEADS_V), lambda b, i: (rows(b, i), 1)),
                  pl.BlockSpec((tl, HEADS_V), lambda b, i: (rows(b, i), 2)),
                  pl.BlockSpec((tl, 1), lambda b, i: (rows(b, i), 0)),
                  pl.BlockSpec((1, DQK_PAD), lambda b, i: (0, 0)),
                  pl.BlockSpec((1, DQK_PAD), lambda b, i: (0, 0)),
                  pl.BlockSpec((1, HEADS_V), lambda b, i: (0, 0))],
        out_specs=pl.BlockSpec((tl, HEADS_V), lambda b, i: (rows(b, i), 0)),
        scratch_shapes=[pltpu.VMEM((N_HEADS, DQK_PAD, DV_PAD), F32)],
        compiler_params=_cparams(("parallel", "arbitrary")),
        name="retention",
    )(proj, proj, proj, proj, pos_f, inv_pad, sgn_pad, norm_g)


def _post_kernel(ym_ref, xq_ref, h_ref, mk_ref, mv_ref, wom_ref, wox_ref, g_ref, b_ref,
                 rw_ref, rb_ref, h1_ref, idx_ref, gate_ref):
    tl = xq_ref.shape[0]
    xq = xq_ref[...] * (XATTN_HEAD_DIM ** -0.5)
    lane = lax.broadcasted_iota(jnp.int32, (tl, XATTN_WIDTH), 1)
    head = lane // XATTN_HEAD_DIM
    mk = mk_ref[0]
    mv = mv_ref[0]
    ymem = jnp.zeros((tl, XATTN_WIDTH), F32)
    for hh in range(XATTN_HEADS):
        sel = head == hh
        qh = jnp.where(sel, xq, 0.0).astype(BF16)
        s = lax.dot_general(qh, mk, _NT, preferred_element_type=F32)
        s = s - jnp.max(s, axis=-1, keepdims=True)
        p = jnp.exp(s)
        p = p / jnp.sum(p, axis=-1, keepdims=True)
        ymem = jnp.where(sel, _dot(p.astype(BF16), mv), ymem)
    y = _dot(ym_ref[...].astype(BF16), wom_ref[...]) + _dot(ymem.astype(BF16), wox_ref[...])
    h1 = _layer_norm_rows(DEEPNORM_ALPHA * h_ref[...] + y, g_ref[...], b_ref[...])
    h1_ref[...] = h1
    logits = jnp.dot(h1, rw_ref[...], preferred_element_type=F32,
                     precision=lax.Precision.HIGHEST) + rb_ref[...]
    ln = lax.broadcasted_iota(jnp.int32, logits.shape, 1)
    vals = logits
    tv, ti = [], []
    for _ in range(TOP_K):
        m = jnp.max(vals, axis=-1, keepdims=True)
        ix = jnp.min(jnp.where(vals == m, ln, LANES), axis=-1, keepdims=True)
        tv.append(m)
        ti.append(ix)
        vals = jnp.where(ln == ix, -jnp.inf, vals)
    ex = [jnp.exp(v - tv[0]) for v in tv]
    tot = ex[0] + ex[1] + ex[2] + ex[3]
    idx_out = jnp.zeros(logits.shape, jnp.int32)
    gate_out = jnp.zeros(logits.shape, F32)
    for k in range(TOP_K):
        idx_out = jnp.where(ln == k, ti[k], idx_out)
        gate_out = jnp.where(ln == k, ex[k] / tot, gate_out)
    idx_ref[...] = idx_out
    gate_ref[...] = gate_out


def _post_mixer(y_mix, proj, xq_blk, h, mem_k, mem_v, wo_mix, wo_mem, ln_g, ln_b, rw, rb,
                seqlen, tl=ROW_TILE):
    t, cm = y_mix.shape
    nl = seqlen // tl
    d = D_MODEL
    full = lambda a, b: pl.BlockSpec((a, b), lambda i: (0, 0))
    return pl.pallas_call(
        _post_kernel,
        out_shape=(jax.ShapeDtypeStruct((t, d), F32),
                   jax.ShapeDtypeStruct((t, LANES), jnp.int32),
                   jax.ShapeDtypeStruct((t, LANES), F32)),
        grid=(t // tl,),
        in_specs=[pl.BlockSpec((tl, cm), lambda i: (i, 0)),
                  pl.BlockSpec((tl, XATTN_WIDTH), lambda i: (i, xq_blk)),
                  pl.BlockSpec((tl, d), lambda i: (i, 0)),
                  pl.BlockSpec((1, N_MEM, XATTN_WIDTH), lambda i: (i // nl, 0, 0)),
                  pl.BlockSpec((1, N_MEM, XATTN_WIDTH), lambda i: (i // nl, 0, 0)),
                  full(cm, d), full(XATTN_WIDTH, d), full(1, d), full(1, d),
                  full(d, LANES), full(1, LANES)],
        out_specs=(pl.BlockSpec((tl, d), lambda i: (i, 0)),
                   pl.BlockSpec((tl, LANES), lambda i: (i, 0)),
                   pl.BlockSpec((tl, LANES), lambda i: (i, 0))),
        compiler_params=_cparams(("parallel",)),
        name="post_mixer",
    )(y_mix, proj, h, mem_k, mem_v, wo_mix, wo_mem, ln_g, ln_b, rw, rb)


def _row_copy(src_hbm, src_row, dst, dst_row, sem):
    return pltpu.make_async_copy(src_hbm.at[pl.ds(src_row, 1)], dst.at[pl.ds(dst_row, 1)], sem)


def _expert_kernel(bexp_ref, nused_ref, rtok_ref, x_hbm, wgu_ref, bgu_ref, wd_ref, bd_ref,
                   y_ref, xbuf, wgu_bf, wd_bf, sem):
    i = pl.program_id(0)
    bm = xbuf.shape[0]
    de = wd_ref.shape[1]

    @pl.when(i < nused_ref[0])
    def _():
        base = i * bm

        def issue(r, c):
            _row_copy(x_hbm, rtok_ref[base + r], xbuf, r, sem).start()
            return c

        lax.fori_loop(0, bm, issue, 0, unroll=8)

        prev = bexp_ref[jnp.maximum(i - 1, 0)]

        @pl.when((i == 0) | (prev != bexp_ref[i]))
        def _():
            wgu_bf[...] = wgu_ref[0].astype(BF16)
            wd_bf[...] = wd_ref[0].astype(BF16)

        def wait(r, c):
            _row_copy(x_hbm, 0, xbuf, r, sem).wait()
            return c

        lax.fori_loop(0, bm, wait, 0, unroll=8)

        xb = xbuf[...].astype(BF16)
        gu = _dot(xb, wgu_bf[...]) + bgu_ref[0]
        x_glu = jnp.minimum(gu[:, :de], SWIGLU_LIMIT)
        x_lin = jnp.clip(gu[:, de:], -SWIGLU_LIMIT, SWIGLU_LIMIT)
        act = x_glu * jax.nn.sigmoid(SWIGLU_ALPHA * x_glu) * (x_lin + 1.0)
        y_ref[...] = _dot(act.astype(BF16), wd_bf[...]) + bd_ref[0]

    @pl.when(i >= nused_ref[0])
    def _():
        y_ref[...] = jnp.zeros_like(y_ref)


def _experts(h1, block_exp, n_used, row_tok, w_gu, b_gu, w_down, b_down, bm=MOE_BM):
    t, d = h1.shape
    n_rows = row_tok.shape[0]
    n_blocks = n_rows // bm
    ne, _, de2 = w_gu.shape
    de = de2 // 2
    grid_spec = pltpu.PrefetchScalarGridSpec(
        num_scalar_prefetch=3,
        grid=(n_blocks,),
        in_specs=[pl.BlockSpec(memory_space=pl.ANY),
                  pl.BlockSpec((1, d, de2), lambda i, be, nu, rt: (be[i], 0, 0)),
                  pl.BlockSpec((1, 1, de2), lambda i, be, nu, rt: (be[i], 0, 0)),
                  pl.BlockSpec((1, de, d), lambda i, be, nu, rt: (be[i], 0, 0)),
                  pl.BlockSpec((1, 1, d), lambda i, be, nu, rt: (be[i], 0, 0))],
        out_specs=pl.BlockSpec((bm, d), lambda i, be, nu, rt: (i, 0)),
        scratch_shapes=[pltpu.VMEM((bm, d), F32),
                        pltpu.VMEM((d, de2), BF16),
                        pltpu.VMEM((de, d), BF16),
                        pltpu.SemaphoreType.DMA],
    )
    return pl.pallas_call(
        _expert_kernel,
        out_shape=jax.ShapeDtypeStruct((n_rows, d), F32),
        grid_spec=grid_spec,
        compiler_params=_cparams(("arbitrary",)),
        name="experts",
    )(block_exp, n_used, row_tok, h1, w_gu, b_gu.reshape(ne, 1, de2), w_down,
      b_down.reshape(ne, 1, d))


def _combine_kernel(pos_ref, yr_hbm, h1_ref, gate_ref, g_ref, b_ref, o_ref, buf, sem, *, tl):
    i = pl.program_id(0)
    base = i * tl

    def issue(t, c):
        for k in range(TOP_K):
            _row_copy(yr_hbm, pos_ref[(base + t) * TOP_K + k], buf, k * tl + t, sem).start()
        return c

    lax.fori_loop(0, tl, issue, 0, unroll=2)

    def wait(r, c):
        _row_copy(yr_hbm, 0, buf, r, sem).wait()
        return c

    lax.fori_loop(0, TOP_K * tl, wait, 0, unroll=8)

    gate = gate_ref[...]
    acc = DEEPNORM_ALPHA * h1_ref[...]
    for k in range(TOP_K):
        acc = acc + gate[:, k:k + 1] * buf[k * tl:(k + 1) * tl, :]
    o_ref[...] = _layer_norm_rows(acc, g_ref[...], b_ref[...])


def _combine(pos_flat, yr, h1, gates, ln_g, ln_b, tl=ROW_TILE):
    t, d = h1.shape
    grid_spec = pltpu.PrefetchScalarGridSpec(
        num_scalar_prefetch=1,
        grid=(t // tl,),
        in_specs=[pl.BlockSpec(memory_space=pl.ANY),
                  pl.BlockSpec((tl, d), lambda i, p: (i, 0)),
                  pl.BlockSpec((tl, LANES), lambda i, p: (i, 0)),
                  pl.BlockSpec((1, d), lambda i, p: (0, 0)),
                  pl.BlockSpec((1, d), lambda i, p: (0, 0))],
        out_specs=pl.BlockSpec((tl, d), lambda i, p: (i, 0)),
        scratch_shapes=[pltpu.VMEM((TOP_K * tl, d), F32), pltpu.SemaphoreType.DMA],
    )
    return pl.pallas_call(
        functools.partial(_combine_kernel, tl=tl),
        out_shape=jax.ShapeDtypeStruct((t, d), F32),
        grid_spec=grid_spec,
        compiler_params=_cparams(("arbitrary",)),
        name="combine",
    )(pos_flat, yr, h1, gates, ln_g, ln_b)


def _route(idx, bm=MOE_BM):
    t = idx.shape[0]
    n_assign = t * TOP_K
    n_rows = (-(-n_assign // bm) + N_EXPERTS) * bm
    n_blocks = n_rows // bm
    onehot = (idx[:, :, None] == jnp.arange(N_EXPERTS, dtype=jnp.int32)[None, None, :])
    sel = jnp.sum(onehot.astype(jnp.int32), axis=1)
    csum = jnp.cumsum(sel, axis=0)
    counts = csum[-1]
    rank = csum - sel
    padded = ((counts + bm - 1) // bm) * bm
    pad_end = jnp.cumsum(padded)
    pad_start = pad_end - padded
    pos = pad_start[idx] + jnp.take_along_axis(rank, idx, axis=1)
    tok = jnp.broadcast_to(jnp.arange(t, dtype=jnp.int32)[:, None], (t, TOP_K))
    row_tok = jnp.zeros((n_rows,), jnp.int32).at[pos.reshape(-1)].set(tok.reshape(-1))
    block_start = jnp.arange(n_blocks, dtype=jnp.int32) * bm
    block_exp = jnp.minimum(
        jnp.sum((block_start[:, None] >= pad_end[None, :]).astype(jnp.int32), axis=1),
        N_EXPERTS - 1).astype(jnp.int32)
    n_used = (pad_end[-1] // bm).astype(jnp.int32).reshape(1)
    return pos.reshape(-1).astype(jnp.int32), row_tok, block_exp, n_used


def _take_cols(w, cols):
    cols = np.asarray(cols, np.int32)
    out = jnp.take(w, jnp.asarray(np.maximum(cols, 0)), axis=-1)
    return jnp.where(jnp.asarray(cols >= 0), out, 0.0)


def _head_cols(offset, width, pad):
    cols = []
    for h in range(N_HEADS):
        cols += list(range(offset + h * width, offset + (h + 1) * width)) + [-1] * (pad - width)
    return cols


def _rope_head_cols(offset):
    half = DQK // 2
    slot = DQK_PAD // 2
    cols = []
    for h in range(N_HEADS):
        b = offset + h * DQK
        cols += list(range(b, b + half)) + [-1] * (slot - half)
        cols += list(range(b + half, b + DQK)) + [-1] * (slot - half)
    return cols


_ML_GATE_OFF = 2 * ML_QK + 2 * MIX_WIDTH
_ML_COLS = (_head_cols(0, DQK, DQK_PAD) + _head_cols(ML_QK, DQK, DQK_PAD)
            + _head_cols(2 * ML_QK, DV, DV_PAD) + _head_cols(2 * ML_QK + MIX_WIDTH, DV, DV_PAD)
            + list(range(_ML_GATE_OFF + 2 * N_HEADS, _ML_GATE_OFF + 2 * N_HEADS + XATTN_WIDTH))
            + list(range(_ML_GATE_OFF, _ML_GATE_OFF + 2 * N_HEADS)) + [-1] * (LANES - 2 * N_HEADS))
_RET_COLS = (_rope_head_cols(0) + _rope_head_cols(ML_QK)
             + _head_cols(2 * ML_QK, DV, DV_PAD) + _head_cols(2 * ML_QK + MIX_WIDTH, DV, DV_PAD)
             + list(range(2 * ML_QK + 2 * MIX_WIDTH, 2 * ML_QK + 2 * MIX_WIDTH + XATTN_WIDTH)))
_MIX_PAD_COLS = _head_cols(0, DV, DV_PAD)
_XQ_BLK_PADDED = (2 * HEADS_QK + 2 * HEADS_V) // XATTN_WIDTH
_XQ_BLK_S5 = MIX_WIDTH // XATTN_WIDTH


def kernel(x, mem, positions, mem_w_k, mem_w_v, l0_w_in, l0_s5_a_re, l0_s5_a_im, l0_s5_log_dt, l0_s5_b_re, l0_s5_b_im, l0_s5_c_re, l0_s5_c_im, l0_s5_d, l0_s5_w_glu, l0_s5_b_glu, l1_w_in, l1_ml_conv_q, l1_ml_conv_k, l1_ml_b_i, l1_ml_b_f, l1_ml_norm_g, l2_w_in, l2_ret_norm_g, l3_w_in, l3_s5_a_re, l3_s5_a_im, l3_s5_log_dt, l3_s5_b_re, l3_s5_b_im, l3_s5_c_re, l3_s5_c_im, l3_s5_d, l3_s5_w_glu, l3_s5_b_glu, w_out, ln1_g, ln1_b, ln2_g, ln2_b, router_w, router_b, exp_w_gu, exp_b_gu, exp_w_down, exp_b_down):
    bsz, seqlen, d = x.shape
    t = bsz * seqlen
    h = x.reshape(t, d)

    w_kv = jnp.concatenate([mem_w_k, mem_w_v], axis=1).astype(BF16)
    kv = _inproj(mem.reshape(bsz * N_MEM, d), w_kv).astype(BF16)
    mem_k = kv[:, :XATTN_WIDTH].reshape(bsz, N_MEM, XATTN_WIDTH)
    mem_v = kv[:, XATTN_WIDTH:].reshape(bsz, N_MEM, XATTN_WIDTH)

    half = DQK // 2
    inv = ROPE_BASE ** (-jnp.arange(0, DQK, 2, dtype=F32) / DQK)
    zpad = jnp.zeros((DQK_PAD // 2 - half,), F32)
    inv_pad = jnp.concatenate([inv, zpad, inv, zpad]).reshape(1, DQK_PAD)
    sgn_pad = jnp.concatenate([-jnp.ones((half,), F32), zpad, jnp.ones((half,), F32), zpad]
                              ).reshape(1, DQK_PAD)
    pos_f = positions.astype(F32).reshape(t, 1)

    s5_params = {
        0: (l0_s5_a_re, l0_s5_a_im, l0_s5_log_dt, l0_s5_b_re, l0_s5_b_im, l0_s5_c_re, l0_s5_c_im,
            l0_s5_d, l0_s5_w_glu, l0_s5_b_glu),
        3: (l3_s5_a_re, l3_s5_a_im, l3_s5_log_dt, l3_s5_b_re, l3_s5_b_im, l3_s5_c_re, l3_s5_c_im,
            l3_s5_d, l3_s5_w_glu, l3_s5_b_glu),
    }
    w_ins = (l0_w_in, l1_w_in, l2_w_in, l3_w_in)

    for i in range(DEPTH):
        kind = i % 3
        wo = w_out[i]
        if kind == 0:
            proj = _inproj(h, w_ins[i].astype(BF16))
            y_mix = _s5_mixer(proj, bsz, seqlen, *s5_params[i])
            wo_mix = wo[:MIX_WIDTH].astype(BF16)
            xq_blk = _XQ_BLK_S5
        elif kind == 1:
            proj = _inproj(h, _take_cols(w_ins[i], _ML_COLS).astype(BF16))
            cw = jnp.concatenate([_take_cols(l1_ml_conv_q, _head_cols(0, DQK, DQK_PAD)),
                                  _take_cols(l1_ml_conv_k, _head_cols(0, DQK, DQK_PAD))], axis=1)
            gbias = jnp.concatenate([l1_ml_b_i, l1_ml_b_f,
                                     jnp.zeros((LANES - 2 * N_HEADS,), F32)]).reshape(1, LANES)
            norm_g = _take_cols(l1_ml_norm_g, _MIX_PAD_COLS).reshape(1, HEADS_V)
            y_mix = _mlstm_mixer(proj, bsz, seqlen, cw, gbias, norm_g)
            wo_mix = _take_cols(wo[:MIX_WIDTH].T, _MIX_PAD_COLS).T.astype(BF16)
            xq_blk = _XQ_BLK_PADDED
        else:
            proj = _inproj(h, _take_cols(w_ins[i], _RET_COLS).astype(BF16))
            norm_g = _take_cols(l2_ret_norm_g, _MIX_PAD_COLS).reshape(1, HEADS_V)
            y_mix = _ret_mixer(proj, bsz, seqlen, pos_f, inv_pad, sgn_pad, norm_g)
            wo_mix = _take_cols(wo[:MIX_WIDTH].T, _MIX_PAD_COLS).T.astype(BF16)
            xq_blk = _XQ_BLK_PADDED
        wo_mem = wo[MIX_WIDTH:].astype(BF16)
        rw = jnp.pad(router_w[i], ((0, 0), (0, LANES - N_EXPERTS)))
        rb = jnp.concatenate([router_b[i], jnp.full((LANES - N_EXPERTS,), -1e30, F32)]
                             ).reshape(1, LANES)
        h1, idx, gates = _post_mixer(y_mix, proj, xq_blk, h, mem_k, mem_v, wo_mix, wo_mem,
                                     ln1_g[i].reshape(1, d), ln1_b[i].reshape(1, d), rw, rb, seqlen)
        pos_flat, row_tok, block_exp, n_used = _route(idx[:, :TOP_K])
        yr = _experts(h1, block_exp, n_used, row_tok, exp_w_gu[i], exp_b_gu[i],
                      exp_w_down[i], exp_b_down[i])
        h = _combine(pos_flat, yr, h1, gates, ln2_g[i].reshape(1, d), ln2_b[i].reshape(1, d))
    return h.reshape(bsz, seqlen, d)
```

```python
import functools
import math

import numpy as np
import jax
import jax.numpy as jnp
from jax import lax
from jax.experimental import pallas as pl
from jax.experimental.pallas import tpu as pltpu

F32 = jnp.float32
BF16 = jnp.bfloat16

D_MODEL = 1024
DEPTH = 4
N_MEM = 256
MIX_WIDTH = 768
XATTN_HEADS = 4
XATTN_WIDTH = 256
XATTN_HEAD_DIM = 64
S5_GROUP = 16
S5_GROUPS = 48
S5_STATE = 64
N_HEADS = 4
DQK = 96
DV = 192
ML_QK = 384
ML_CONV = 4
ML_CHUNK = 64
RET_CHUNK = 128
ROPE_BASE = 10000.0
N_EXPERTS = 32
TOP_K = 4
SWIGLU_LIMIT = 7.0
SWIGLU_ALPHA = 1.702
DEEPNORM_ALPHA = (2.0 * DEPTH) ** 0.25
LN_EPS = 1e-5

LANES = 128
SUBLANES = 8
DQK_PAD = 128
DV_PAD = 256
HEADS_QK = N_HEADS * DQK_PAD
HEADS_V = N_HEADS * DV_PAD
S5_LC = 16
S5_K = S5_LC * S5_GROUP
MOE_BM = 256
ROW_TILE = 256
VMEM_LIMIT = 56 * 1024 * 1024

_NT = (((1,), (1,)), ((), ()))
_TN = (((0,), (0,)), ((), ()))


def _cparams(sem):
    return pltpu.CompilerParams(dimension_semantics=sem, vmem_limit_bytes=VMEM_LIMIT)


def _dot(a, b):
    return jnp.dot(a, b, preferred_element_type=F32)


def _layer_norm_rows(z, g, b):
    mu = jnp.mean(z, axis=-1, keepdims=True)
    d = z - mu
    var = jnp.mean(d * d, axis=-1, keepdims=True)
    return d * lax.rsqrt(var + LN_EPS) * g + b


def _inproj_kernel(x_ref, w_ref, o_ref):
    xb = x_ref[...].astype(BF16)
    n = o_ref.shape[1]
    step = 512
    for c0 in range(0, n, step):
        c1 = min(c0 + step, n)
        o_ref[:, c0:c1] = _dot(xb, w_ref[:, c0:c1])


def _inproj(x, w_bf16, tm=ROW_TILE):
    t, d = x.shape
    n = w_bf16.shape[1]
    return pl.pallas_call(
        _inproj_kernel,
        out_shape=jax.ShapeDtypeStruct((t, n), F32),
        grid=(t // tm,),
        in_specs=[pl.BlockSpec((tm, d), lambda i: (i, 0)),
                  pl.BlockSpec((d, n), lambda i: (0, 0))],
        out_specs=pl.BlockSpec((tm, n), lambda i: (i, 0)),
        compiler_params=_cparams(("parallel",)),
        name="inproj",
    )(x, w_bf16)


def _s5_prep(a_re, a_im, log_dt, b_re, b_im, c_re, c_im):
    hp = lax.Precision.HIGHEST
    lam_re = jnp.minimum(a_re.astype(F32), -1e-4)
    lam_im = a_im.astype(F32)
    dt = jnp.exp(log_dt.astype(F32))[:, None]
    mag = jnp.exp(dt * lam_re)
    ab_re = mag * jnp.cos(dt * lam_im)
    ab_im = mag * jnp.sin(dt * lam_im)
    den = lam_re * lam_re + lam_im * lam_im
    num_re = ab_re - 1.0
    coef_re = (num_re * lam_re + ab_im * lam_im) / den
    coef_im = (ab_im * lam_re - num_re * lam_im) / den
    bre = b_re.astype(F32)
    bim = b_im.astype(F32)
    bb_re = coef_re[..., None] * bre - coef_im[..., None] * bim
    bb_im = coef_re[..., None] * bim + coef_im[..., None] * bre
    pr = [jnp.ones_like(ab_re)]
    pi = [jnp.zeros_like(ab_im)]
    for _ in range(S5_LC):
        r, i = pr[-1], pi[-1]
        pr.append(r * ab_re - i * ab_im)
        pi.append(r * ab_im + i * ab_re)
    pw_re = jnp.stack(pr)
    pw_im = jnp.stack(pi)
    p_re = pw_re[:S5_LC, :, :, None] * bb_re[None] - pw_im[:S5_LC, :, :, None] * bb_im[None]
    p_im = pw_re[:S5_LC, :, :, None] * bb_im[None] + pw_im[:S5_LC, :, :, None] * bb_re[None]
    cre = c_re.astype(F32)
    cim = c_im.astype(F32)
    kmat = (jnp.einsum('ghp,tgpk->gthk', cre, p_re, precision=hp)
            - jnp.einsum('ghp,tgpk->gthk', cim, p_im, precision=hp))
    s_idx = jnp.arange(S5_LC)[:, None]
    j_idx = jnp.arange(S5_LC)[None, :]
    tau = j_idx - s_idx
    kg = kmat[:, jnp.clip(tau, 0, S5_LC - 1)]
    kg = jnp.where((tau >= 0)[None, :, :, None, None], kg, 0.0)
    t_mat = kg.transpose(0, 1, 4, 2, 3).reshape(S5_GROUPS, S5_K, S5_K)
    w1_re = p_re[::-1].transpose(1, 0, 3, 2).reshape(S5_GROUPS, S5_K, S5_STATE)
    w1_im = p_im[::-1].transpose(1, 0, 3, 2).reshape(S5_GROUPS, S5_K, S5_STATE)
    ar = pw_re[1:].transpose(1, 2, 0)[:, :, :, None]
    ai = pw_im[1:].transpose(1, 2, 0)[:, :, :, None]
    crt = cre.transpose(0, 2, 1)[:, :, None, :]
    cit = cim.transpose(0, 2, 1)[:, :, None, :]
    w3_re = (crt * ar - cit * ai).reshape(S5_GROUPS, S5_STATE, S5_K)
    w3_im = (-crt * ai - cit * ar).reshape(S5_GROUPS, S5_STATE, S5_K)
    al_re = pw_re[S5_LC][:, None, :]
    al_im = pw_im[S5_LC][:, None, :]
    return (t_mat.astype(BF16), w1_re.astype(BF16), w1_im.astype(BF16),
            w3_re.astype(BF16), w3_im.astype(BF16), al_re, al_im)


def _s5_kernel(u_ref, t_ref, w1r_ref, w1i_ref, w3r_ref, w3i_ref, alr_ref, ali_ref,
               y_ref, sre, sim, xpr, xpi, *, n_sub, bsz):
    ub = u_ref[0].astype(BF16)
    sre[...] = _dot(ub, w1r_ref[0])
    sim[...] = _dot(ub, w1i_ref[0])
    ar = alr_ref[0]
    ai = ali_ref[0]

    def step(c, carry):
        xr, xi = carry
        r0 = pl.multiple_of(c * bsz, bsz)
        xpr[pl.ds(r0, bsz), :] = xr
        xpi[pl.ds(r0, bsz), :] = xi
        sr = sre[pl.ds(r0, bsz), :]
        si = sim[pl.ds(r0, bsz), :]
        return ar * xr - ai * xi + sr, ar * xi + ai * xr + si

    z = jnp.zeros((bsz, S5_STATE), F32)
    lax.fori_loop(0, n_sub, step, (z, z))
    y_ref[0] = (_dot(ub, t_ref[0]) + _dot(xpr[...].astype(BF16), w3r_ref[0])
                + _dot(xpi[...].astype(BF16), w3i_ref[0]))


def _s5_scan(u_t, mats, n_sub, bsz):
    t_mat, w1r, w1i, w3r, w3i, alr, ali = mats
    g, r, k = u_t.shape
    blk3 = lambda a, b: pl.BlockSpec((1, a, b), lambda i: (i, 0, 0))
    return pl.pallas_call(
        functools.partial(_s5_kernel, n_sub=n_sub, bsz=bsz),
        out_shape=jax.ShapeDtypeStruct((g, r, k), F32),
        grid=(g,),
        in_specs=[blk3(r, k), blk3(k, k), blk3(k, S5_STATE), blk3(k, S5_STATE),
                  blk3(S5_STATE, k), blk3(S5_STATE, k), blk3(1, S5_STATE), blk3(1, S5_STATE)],
        out_specs=blk3(r, k),
        scratch_shapes=[pltpu.VMEM((r, S5_STATE), F32)] * 4,
        compiler_params=_cparams(("parallel",)),
        name="s5_scan",
    )(u_t, t_mat, w1r, w1i, w3r, w3i, alr, ali)


def _s5_post_kernel(y_ref, u_ref, d_ref, w_ref, b_ref, o_ref):
    y = y_ref[...] + d_ref[...] * u_ref[...]
    y = jax.nn.gelu(y)
    o_ref[...] = y * jax.nn.sigmoid(_dot(y.astype(BF16), w_ref[...]) + b_ref[...])


def _s5_post(y_ssm, proj, d_skip, w_glu_bf16, b_glu, tm=ROW_TILE):
    t = y_ssm.shape[0]
    w = MIX_WIDTH
    return pl.pallas_call(
        _s5_post_kernel,
        out_shape=jax.ShapeDtypeStruct((t, w), F32),
        grid=(t // tm,),
        in_specs=[pl.BlockSpec((tm, w), lambda i: (i, 0)),
                  pl.BlockSpec((tm, w), lambda i: (i, 0)),
                  pl.BlockSpec((1, w), lambda i: (0, 0)),
                  pl.BlockSpec((w, w), lambda i: (0, 0)),
                  pl.BlockSpec((1, w), lambda i: (0, 0))],
        out_specs=pl.BlockSpec((tm, w), lambda i: (i, 0)),
        compiler_params=_cparams(("parallel",)),
        name="s5_post",
    )(y_ssm, proj, d_skip.reshape(1, w), w_glu_bf16, b_glu.reshape(1, w))


def _s5_mixer(proj, bsz, seqlen, a_re, a_im, log_dt, b_re, b_im, c_re, c_im, d_skip, w_glu, b_glu):
    mats = _s5_prep(a_re, a_im, log_dt, b_re, b_im, c_re, c_im)
    n_sub = seqlen // S5_LC
    u = proj[:, :MIX_WIDTH]
    u_t = (u.reshape(bsz, n_sub, S5_LC, S5_GROUPS, S5_GROUP)
           .transpose(3, 1, 0, 2, 4).reshape(S5_GROUPS, n_sub * bsz, S5_K))
    y_t = _s5_scan(u_t, mats, n_sub, bsz)
    y = (y_t.reshape(S5_GROUPS, n_sub, bsz, S5_LC, S5_GROUP)
         .transpose(2, 1, 3, 0, 4).reshape(bsz * seqlen, MIX_WIDTH))
    return _s5_post(y, proj, d_skip, w_glu.astype(BF16), b_glu)


def _head_norm_padded(hv, g):
    lane = lax.broadcasted_iota(jnp.int32, hv.shape, 1)
    real = lane < DV
    mu = jnp.sum(hv, axis=-1, keepdims=True) * (1.0 / DV)
    d = jnp.where(real, hv - mu, 0.0)
    var = jnp.sum(d * d, axis=-1, keepdims=True) * (1.0 / DV)
    return d * lax.rsqrt(var + LN_EPS) * g


def _log_sigmoid(x):
    return jnp.minimum(x, 0.0) - jnp.log(1.0 + jnp.exp(-jnp.abs(x)))


def _mlstm_kernel(q_ref, k_ref, v_ref, o_ref, gt_ref, cw_ref, gb_ref, ng_ref,
                  y_ref, cbuf, c_st, n_st, m_st, *, tl):
    i = pl.program_id(1)

    @pl.when(i == 0)
    def _():
        cbuf[0:SUBLANES, :] = jnp.zeros((SUBLANES, 2 * HEADS_QK), F32)
        c_st[...] = jnp.zeros_like(c_st)
        n_st[...] = jnp.zeros_like(n_st)
        m_st[...] = jnp.zeros_like(m_st)

    cbuf[SUBLANES:SUBLANES + tl, 0:HEADS_QK] = q_ref[...]
    cbuf[SUBLANES:SUBLANES + tl, HEADS_QK:2 * HEADS_QK] = k_ref[...]
    acc = jnp.zeros((tl, 2 * HEADS_QK), F32)
    for w in range(ML_CONV):
        acc = acc + cbuf[pl.ds(SUBLANES - (ML_CONV - 1) + w, tl), :] * cw_ref[w:w + 1, :]
    qk = acc * jax.nn.sigmoid(acc)
    cbuf[0:SUBLANES, :] = cbuf[tl:tl + SUBLANES, :]

    gt = gt_ref[...] + gb_ref[...]
    lf = _log_sigmoid(gt)
    gt_t = gt.T
    lf_t = _log_sigmoid(gt_t)

    cl = ML_CHUNK
    row = lax.broadcasted_iota(jnp.int32, (cl, cl), 0)
    col = lax.broadcasted_iota(jnp.int32, (cl, cl), 1)
    tri = row >= col
    scale = DQK ** -0.5
    for cc in range(tl // cl):
        r0 = cc * cl
        for h in range(N_HEADS):
            ig_col = gt[r0:r0 + cl, h:h + 1]
            lf_col = lf[r0:r0 + cl, N_HEADS + h:N_HEADS + h + 1]
            ig_row = gt_t[h:h + 1, r0:r0 + cl]
            lf_row = lf_t[N_HEADS + h:N_HEADS + h + 1, r0:r0 + cl]
            bcum_col = jnp.sum(jnp.where(tri, lf_row, 0.0), axis=1, keepdims=True)
            bcum_row = jnp.sum(jnp.where(col >= row, lf_col, 0.0), axis=0, keepdims=True)
            btot = jnp.sum(lf_row, axis=1, keepdims=True)
            w_row = btot - bcum_row + ig_row
            m_loc = jnp.max(w_row, axis=1, keepdims=True)
            e_col = jnp.exp(btot - bcum_col + ig_col - m_loc)
            m_prev = m_st[h:h + 1, 0:1]
            c_prev = c_st[h]
            n_prev = n_st[h:h + 1, :]
            q = qk[r0:r0 + cl, h * DQK_PAD:(h + 1) * DQK_PAD] * scale
            k = qk[r0:r0 + cl, HEADS_QK + h * DQK_PAD:HEADS_QK + (h + 1) * DQK_PAD]
            v = v_ref[r0:r0 + cl, h * DV_PAD:(h + 1) * DV_PAD]
            qb = q.astype(BF16)
            kb = k.astype(BF16)
            vb = v.astype(BF16)
            dmat = jnp.where(tri, bcum_col - bcum_row + ig_row, -jnp.inf)
            g_col = bcum_col + m_prev
            m_row = jnp.maximum(g_col, jnp.max(dmat, axis=1, keepdims=True))
            inter = jnp.exp(g_col - m_row)
            s_qk = lax.dot_general(qb, kb, _NT, preferred_element_type=F32) * jnp.exp(dmat - m_row)
            num = inter * _dot(qb, c_prev.astype(BF16)) + _dot(s_qk.astype(BF16), vb)
            den = (inter * jnp.sum(q * n_prev, axis=1, keepdims=True)
                   + jnp.sum(s_qk, axis=1, keepdims=True))
            hv = num / jnp.maximum(jnp.abs(den), jnp.exp(-m_row))
            ke = k * e_col
            kv = lax.dot_general(ke.astype(BF16), vb, _TN, preferred_element_type=F32)
            nk = jnp.sum(ke, axis=0, keepdims=True)
            m_new = jnp.maximum(btot + m_prev, m_loc)
            sa = jnp.exp(btot + m_prev - m_new)
            sb = jnp.exp(m_loc - m_new)
            c_st[h] = sa * c_prev + sb * kv
            n_st[h:h + 1, :] = sa * n_prev + sb * nk
            m_st[h:h + 1, :] = jnp.broadcast_to(m_new, (1, LANES))
            hn = _head_norm_padded(hv, ng_ref[0:1, h * DV_PAD:(h + 1) * DV_PAD])
            og = o_ref[r0:r0 + cl, h * DV_PAD:(h + 1) * DV_PAD]
            y_ref[r0:r0 + cl, h * DV_PAD:(h + 1) * DV_PAD] = jax.nn.sigmoid(og) * hn


def _mlstm_mixer(proj, bsz, seqlen, cw, gbias, norm_g, tl=ROW_TILE):
    t = proj.shape[0]
    nl = seqlen // tl
    rows = lambda b, i: b * nl + i
    return pl.pallas_call(
        functools.partial(_mlstm_kernel, tl=tl),
        out_shape=jax.ShapeDtypeStruct((t, HEADS_V), F32),
        grid=(bsz, nl),
        in_specs=[pl.BlockSpec((tl, HEADS_QK), lambda b, i: (rows(b, i), 0)),
                  pl.BlockSpec((tl, HEADS_QK), lambda b, i: (rows(b, i), 1)),
                  pl.BlockSpec((tl, HEADS_V), lambda b, i: (rows(b, i), 1)),
                  pl.BlockSpec((tl, HEADS_V), lambda b, i: (rows(b, i), 2)),
                  pl.BlockSpec((tl, LANES), lambda b, i: (rows(b, i), 26)),
                  pl.BlockSpec((ML_CONV, 2 * HEADS_QK), lambda b, i: (0, 0)),
                  pl.BlockSpec((1, LANES), lambda b, i: (0, 0)),
                  pl.BlockSpec((1, HEADS_V), lambda b, i: (0, 0))],
        out_specs=pl.BlockSpec((tl, HEADS_V), lambda b, i: (rows(b, i), 0)),
        scratch_shapes=[pltpu.VMEM((tl + SUBLANES, 2 * HEADS_QK), F32),
                        pltpu.VMEM((N_HEADS, DQK_PAD, DV_PAD), F32),
                        pltpu.VMEM((SUBLANES, DQK_PAD), F32),
                        pltpu.VMEM((SUBLANES, LANES), F32)],
        compiler_params=_cparams(("parallel", "arbitrary")),
        name="mlstm",
    )(proj, proj, proj, proj, proj, cw, gbias, norm_g)


def _ret_log_gamma(h):
    return float(np.log(np.float32(1.0) - np.power(np.float32(2.0), np.float32(-5.0 - h))))


def _ret_kernel(q_ref, k_ref, v_ref, g_ref, pos_ref, inv_ref, sgn_ref, ng_ref,
                y_ref, s_st, *, tl):
    i = pl.program_id(1)

    @pl.when(i == 0)
    def _():
        s_st[...] = jnp.zeros_like(s_st)

    ang = pos_ref[...] * inv_ref[...]
    cos_t = jnp.cos(ang)
    sin_t = jnp.sin(ang) * sgn_ref[...]
    cl = RET_CHUNK
    row = lax.broadcasted_iota(jnp.int32, (cl, cl), 0)
    col = lax.broadcasted_iota(jnp.int32, (cl, cl), 1)
    rel = (row - col).astype(F32)
    jcol = lax.broadcasted_iota(jnp.int32, (cl, 1), 0).astype(F32)
    kscale = DQK ** -0.5
    for h in range(N_HEADS):
        lg = _ret_log_gamma(h)
        decay = jnp.where(rel >= 0, jnp.exp(jnp.maximum(rel, 0.0) * lg), 0.0)
        zeta = jnp.exp((cl - 1 - jcol) * lg)
        xi = jnp.exp((jcol + 1.0) * lg)
        chunk_decay = float(np.exp(np.float32(cl) * np.float32(lg)))
        qh = q_ref[:, h * DQK_PAD:(h + 1) * DQK_PAD]
        kh = k_ref[:, h * DQK_PAD:(h + 1) * DQK_PAD]
        qh = qh * cos_t + pltpu.roll(qh, DQK_PAD // 2, 1) * sin_t
        kh = (kh * cos_t + pltpu.roll(kh, DQK_PAD // 2, 1) * sin_t) * kscale
        for cc in range(tl // cl):
            r0 = cc * cl
            qb = qh[r0:r0 + cl].astype(BF16)
            k = kh[r0:r0 + cl]
            kb = k.astype(BF16)
            v = v_ref[r0:r0 + cl, h * DV_PAD:(h + 1) * DV_PAD]
            vb = v.astype(BF16)
            s_prev = s_st[h]
            s = lax.dot_general(qb, kb, _NT, preferred_element_type=F32) * decay
            intra = _dot(s.astype(BF16), vb)
            cross = _dot(qb, s_prev.astype(BF16)) * xi
            r = lax.dot_general((k * zeta).astype(BF16), vb, _TN, preferred_element_type=F32)
            s_st[h] = chunk_decay * s_prev + r
            hn = _head_norm_padded(intra + cross, ng_ref[0:1, h * DV_PAD:(h + 1) * DV_PAD])
            gate = g_ref[r0:r0 + cl, h * DV_PAD:(h + 1) * DV_PAD]
            y_ref[r0:r0 + cl, h * DV_PAD:(h + 1) * DV_PAD] = gate * jax.nn.sigmoid(gate) * hn


def _ret_mixer(proj, bsz, seqlen, pos_f, inv_pad, sgn_pad, norm_g, tl=ROW_TILE):
    t = proj.shape[0]
    nl = seqlen // tl
    rows = lambda b, i: b * nl + i
    return pl.pallas_call(
        functools.partial(_ret_kernel, tl=tl),
        out_shape=jax.ShapeDtypeStruct((t, HEADS_V), F32),
        grid=(bsz, nl),
        in_specs=[pl.BlockSpec((tl, HEADS_QK), lambda b, i: (rows(b, i), 0)),
                  pl.BlockSpec((tl, HEADS_QK), lambda b, i: (rows(b, i), 1)),
                  pl.BlockSpec((tl, HEADS_V), lambda b, i: (rows(b, i), 1)),
                  pl.BlockSpec((tl, HEADS_V), lambda b, i: (rows(b, i), 2)),
                  pl.BlockSpec((tl, 1), lambda b, i: (rows(b, i), 0)),
                  pl.BlockSpec((1, DQK_PAD), lambda b, i: (0, 0)),
                  pl.BlockSpec((1, DQK_PAD), lambda b, i: (0, 0)),
                  pl.BlockSpec((1, HEADS_V), lambda b, i: (0, 0))],
        out_specs=pl.BlockSpec((tl, HEADS_V), lambda b, i: (rows(b, i), 0)),
        scratch_shapes=[pltpu.VMEM((N_HEADS, DQK_PAD, DV_PAD), F32)],
        compiler_params=_cparams(("parallel", "arbitrary")),
        name="retention",
    )(proj, proj, proj, proj, pos_f, inv_pad, sgn_pad, norm_g)


def _post_kernel(ym_ref, xq_ref, h_ref, mk_ref, mv_ref, wom_ref, wox_ref, g_ref, b_ref,
                 rw_ref, rb_ref, h1_ref, h1r_ref, idx_ref, gate_ref):
    tl = xq_ref.shape[0]
    xq = xq_ref[...] * (XATTN_HEAD_DIM ** -0.5)
    lane = lax.broadcasted_iota(jnp.int32, (tl, XATTN_WIDTH), 1)
    head = lane // XATTN_HEAD_DIM
    mk = mk_ref[0]
    mv = mv_ref[0]
    ymem = jnp.zeros((tl, XATTN_WIDTH), F32)
    for hh in range(XATTN_HEADS):
        sel = head == hh
        qh = jnp.where(sel, xq, 0.0).astype(BF16)
        s = lax.dot_general(qh, mk, _NT, preferred_element_type=F32)
        s = s - jnp.max(s, axis=-1, keepdims=True)
        p = jnp.exp(s)
        p = p / jnp.sum(p, axis=-1, keepdims=True)
        ymem = jnp.where(sel, _dot(p.astype(BF16), mv), ymem)
    y = _dot(ym_ref[...].astype(BF16), wom_ref[...]) + _dot(ymem.astype(BF16), wox_ref[...])
    h1 = _layer_norm_rows(DEEPNORM_ALPHA * h_ref[...] + y, g_ref[...], b_ref[...])
    h1_ref[...] = h1
    _store_row_tiles(h1r_ref, h1)
    logits = jnp.dot(h1, rw_ref[...], preferred_element_type=F32,
                     precision=lax.Precision.HIGHEST) + rb_ref[...]
    ln = lax.broadcasted_iota(jnp.int32, logits.shape, 1)
    vals = logits
    tv, ti = [], []
    for _ in range(TOP_K):
        m = jnp.max(vals, axis=-1, keepdims=True)
        ix = jnp.min(jnp.where(vals == m, ln, LANES), axis=-1, keepdims=True)
        tv.append(m)
        ti.append(ix)
        vals = jnp.where(ln == ix, -jnp.inf, vals)
    ex = [jnp.exp(v - tv[0]) for v in tv]
    tot = ex[0] + ex[1] + ex[2] + ex[3]
    idx_out = jnp.zeros(logits.shape, jnp.int32)
    gate_out = jnp.zeros(logits.shape, F32)
    for k in range(TOP_K):
        idx_out = jnp.where(ln == k, ti[k], idx_out)
        gate_out = jnp.where(ln == k, ex[k] / tot, gate_out)
    idx_ref[...] = idx_out
    gate_ref[...] = gate_out


def _post_mixer(y_mix, proj, xq_blk, h, mem_k, mem_v, wo_mix, wo_mem, ln_g, ln_b, rw, rb,
                seqlen, tl=ROW_TILE):
    t, cm = y_mix.shape
    nl = seqlen // tl
    d = D_MODEL
    full = lambda a, b: pl.BlockSpec((a, b), lambda i: (0, 0))
    return pl.pallas_call(
        _post_kernel,
        out_shape=(jax.ShapeDtypeStruct((t, d), F32),
                   jax.ShapeDtypeStruct((t, d // LANES, LANES), F32),
                   jax.ShapeDtypeStruct((t, LANES), jnp.int32),
                   jax.ShapeDtypeStruct((t, LANES), F32)),
        grid=(t // tl,),
        in_specs=[pl.BlockSpec((tl, cm), lambda i: (i, 0)),
                  pl.BlockSpec((tl, XATTN_WIDTH), lambda i: (i, xq_blk)),
                  pl.BlockSpec((tl, d), lambda i: (i, 0)),
                  pl.BlockSpec((1, N_MEM, XATTN_WIDTH), lambda i: (i // nl, 0, 0)),
                  pl.BlockSpec((1, N_MEM, XATTN_WIDTH), lambda i: (i // nl, 0, 0)),
                  full(cm, d), full(XATTN_WIDTH, d), full(1, d), full(1, d),
                  full(d, LANES), full(1, LANES)],
        out_specs=(pl.BlockSpec((tl, d), lambda i: (i, 0)),
                   pl.BlockSpec((tl, d // LANES, LANES), lambda i: (i, 0, 0)),
                   pl.BlockSpec((tl, LANES), lambda i: (i, 0)),
                   pl.BlockSpec((tl, LANES), lambda i: (i, 0))),
        compiler_params=_cparams(("parallel",)),
        name="post_mixer",
    )(y_mix, proj, h, mem_k, mem_v, wo_mix, wo_mem, ln_g, ln_b, rw, rb)


def _store_row_tiles(ref3, val):
    for j in range(val.shape[1] // LANES):
        ref3[:, j, :] = val[:, j * LANES:(j + 1) * LANES]


def _load_row_tiles(ref3):
    return jnp.concatenate([ref3[:, j, :] for j in range(ref3.shape[1])], axis=1)


def _gather_copy(src_hbm, src_row, dst, dst_row, sem):
    return pltpu.make_async_copy(src_hbm.at[src_row], dst.at[dst_row], sem)


def _expert_kernel(bexp_ref, nused_ref, rtok_ref, x_hbm, wgu_ref, bgu_ref, wd_ref, bd_ref,
                   y_ref, xbuf, act, wgu_bf, wd_bf, sems, *, n_blocks):
    i = pl.program_id(0)
    bm = xbuf.shape[1]
    de = wd_ref.shape[1]
    nused = nused_ref[0]
    slot = lax.rem(i, 2)
    n_groups = 8
    per_group = bm // n_groups

    def issue_rows(blk, dst_slot, r0, r1):
        base = blk * bm
        for r in range(r0, r1):
            _gather_copy(x_hbm, rtok_ref[base + r], xbuf.at[dst_slot], r, sems.at[dst_slot]).start()

    def wait_slot(s):
        def body(r, c):
            _gather_copy(x_hbm, 0, xbuf.at[s], r, sems.at[s]).wait()
            return c
        lax.fori_loop(0, bm, body, 0, unroll=8)

    @pl.when(i == 0)
    def _():
        issue_rows(0, 0, 0, bm)

    @pl.when(i <= nused)
    def _():
        wait_slot(slot)

    @pl.when(i < nused)
    def _():
        prev = bexp_ref[jnp.maximum(i - 1, 0)]

        @pl.when((i == 0) | (prev != bexp_ref[i]))
        def _():
            wgu_bf[...] = wgu_ref[0].astype(BF16)
            wd_bf[...] = wd_ref[0].astype(BF16)

        nxt = jnp.minimum(i + 1, n_blocks - 1)
        nslot = 1 - slot
        xb = _load_row_tiles(xbuf.at[slot]).astype(BF16)
        cw = de // (n_groups // 2)
        for c in range(n_groups // 2):
            g = _dot(xb, wgu_bf[:, c * cw:(c + 1) * cw]) + bgu_ref[0, :, c * cw:(c + 1) * cw]
            l = (_dot(xb, wgu_bf[:, de + c * cw:de + (c + 1) * cw])
                 + bgu_ref[0, :, de + c * cw:de + (c + 1) * cw])
            x_glu = jnp.minimum(g, SWIGLU_LIMIT)
            x_lin = jnp.clip(l, -SWIGLU_LIMIT, SWIGLU_LIMIT)
            act[:, c * cw:(c + 1) * cw] = (x_glu * jax.nn.sigmoid(SWIGLU_ALPHA * x_glu)
                                           * (x_lin + 1.0)).astype(BF16)
            issue_rows(nxt, nslot, c * per_group, (c + 1) * per_group)
        d = wd_bf.shape[1]
        ow = d // (n_groups // 2)
        for c in range(n_groups // 2):
            yc = _dot(act[...], wd_bf[:, c * ow:(c + 1) * ow]) + bd_ref[0, :, c * ow:(c + 1) * ow]
            for j in range(ow // LANES):
                y_ref[:, c * (ow // LANES) + j, :] = yc[:, j * LANES:(j + 1) * LANES]
            g0 = (n_groups // 2 + c) * per_group
            issue_rows(nxt, nslot, g0, g0 + per_group)

    @pl.when(i >= nused)
    def _():
        y_ref[...] = jnp.zeros_like(y_ref)

    @pl.when((i == n_blocks - 1) & (i < nused))
    def _():
        wait_slot(1 - slot)


def _experts(h1r, block_exp, n_used, row_tok, w_gu, b_gu, w_down, b_down, bm=MOE_BM):
    t, dj, _ = h1r.shape
    d = dj * LANES
    n_rows = row_tok.shape[0]
    n_blocks = n_rows // bm
    ne, _, de2 = w_gu.shape
    de = de2 // 2
    grid_spec = pltpu.PrefetchScalarGridSpec(
        num_scalar_prefetch=3,
        grid=(n_blocks,),
        in_specs=[pl.BlockSpec(memory_space=pl.ANY),
                  pl.BlockSpec((1, d, de2), lambda i, be, nu, rt: (be[i], 0, 0)),
                  pl.BlockSpec((1, 1, de2), lambda i, be, nu, rt: (be[i], 0, 0)),
                  pl.BlockSpec((1, de, d), lambda i, be, nu, rt: (be[i], 0, 0)),
                  pl.BlockSpec((1, 1, d), lambda i, be, nu, rt: (be[i], 0, 0))],
        out_specs=pl.BlockSpec((bm, dj, LANES), lambda i, be, nu, rt: (i, 0, 0)),
        scratch_shapes=[pltpu.VMEM((2, bm, dj, LANES), F32),
                        pltpu.VMEM((bm, de), BF16),
                        pltpu.VMEM((d, de2), BF16),
                        pltpu.VMEM((de, d), BF16),
                        pltpu.SemaphoreType.DMA((2,))],
    )
    return pl.pallas_call(
        functools.partial(_expert_kernel, n_blocks=n_blocks),
        out_shape=jax.ShapeDtypeStruct((n_rows, dj, LANES), F32),
        grid_spec=grid_spec,
        compiler_params=_cparams(("arbitrary",)),
        name="experts",
    )(block_exp, n_used, row_tok, h1r, w_gu, b_gu.reshape(ne, 1, de2), w_down,
      b_down.reshape(ne, 1, d))


def _combine_kernel(pos_ref, yr_hbm, h1_ref, gate_ref, g_ref, b_ref, o_ref, buf, sems, *, tl, n_tiles):
    i = pl.program_id(0)
    slot = lax.rem(i, 2)

    def issue_tile(tile, dst_slot):
        base = tile * tl

        def body(t, c):
            for k in range(TOP_K):
                _gather_copy(yr_hbm, pos_ref[(base + t) * TOP_K + k], buf.at[dst_slot],
                             k * tl + t, sems.at[dst_slot]).start()
            return c

        lax.fori_loop(0, tl, body, 0, unroll=4)

    def wait_slot(s):
        def body(r, c):
            _gather_copy(yr_hbm, 0, buf.at[s], r, sems.at[s]).wait()
            return c
        lax.fori_loop(0, TOP_K * tl, body, 0, unroll=8)

    @pl.when(i == 0)
    def _():
        issue_tile(0, 0)

    @pl.when(i + 1 < n_tiles)
    def _():
        issue_tile(i + 1, 1 - slot)

    wait_slot(slot)
    gate = gate_ref[...]
    h1 = h1_ref[...]
    cols = []
    for j in range(h1.shape[1] // LANES):
        acc = DEEPNORM_ALPHA * h1[:, j * LANES:(j + 1) * LANES]
        for k in range(TOP_K):
            acc = acc + gate[:, k:k + 1] * buf[slot, k * tl:(k + 1) * tl, j, :]
        cols.append(acc)
    o_ref[...] = _layer_norm_rows(jnp.concatenate(cols, axis=1), g_ref[...], b_ref[...])


def _combine(pos_flat, yr, h1, gates, ln_g, ln_b, tl=ROW_TILE):
    t, d = h1.shape
    dj = d // LANES
    n_tiles = t // tl
    grid_spec = pltpu.PrefetchScalarGridSpec(
        num_scalar_prefetch=1,
        grid=(n_tiles,),
        in_specs=[pl.BlockSpec(memory_space=pl.ANY),
                  pl.BlockSpec((tl, d), lambda i, p: (i, 0)),
                  pl.BlockSpec((tl, LANES), lambda i, p: (i, 0)),
                  pl.BlockSpec((1, d), lambda i, p: (0, 0)),
                  pl.BlockSpec((1, d), lambda i, p: (0, 0))],
        out_specs=pl.BlockSpec((tl, d), lambda i, p: (i, 0)),
        scratch_shapes=[pltpu.VMEM((2, TOP_K * tl, dj, LANES), F32),
                        pltpu.SemaphoreType.DMA((2,))],
    )
    return pl.pallas_call(
        functools.partial(_combine_kernel, tl=tl, n_tiles=n_tiles),
        out_shape=jax.ShapeDtypeStruct((t, d), F32),
        grid_spec=grid_spec,
        compiler_params=_cparams(("arbitrary",)),
        name="combine",
    )(pos_flat, yr, h1, gates, ln_g, ln_b)


def _route(idx, bm=MOE_BM):
    t = idx.shape[0]
    n_assign = t * TOP_K
    n_rows = (-(-n_assign // bm) + N_EXPERTS) * bm
    n_blocks = n_rows // bm
    onehot = (idx[:, :, None] == jnp.arange(N_EXPERTS, dtype=jnp.int32)[None, None, :])
    sel = jnp.sum(onehot.astype(jnp.int32), axis=1)
    csum = jnp.cumsum(sel, axis=0)
    counts = csum[-1]
    rank = csum - sel
    padded = ((counts + bm - 1) // bm) * bm
    pad_end = jnp.cumsum(padded)
    pad_start = pad_end - padded
    pos = pad_start[idx] + jnp.take_along_axis(rank, idx, axis=1)
    tok = jnp.broadcast_to(jnp.arange(t, dtype=jnp.int32)[:, None], (t, TOP_K))
    row_tok = jnp.zeros((n_rows,), jnp.int32).at[pos.reshape(-1)].set(tok.reshape(-1))
    block_start = jnp.arange(n_blocks, dtype=jnp.int32) * bm
    block_exp = jnp.minimum(
        jnp.sum((block_start[:, None] >= pad_end[None, :]).astype(jnp.int32), axis=1),
        N_EXPERTS - 1).astype(jnp.int32)
    n_used = (pad_end[-1] // bm).astype(jnp.int32).reshape(1)
    return pos.reshape(-1).astype(jnp.int32), row_tok, block_exp, n_used


def _take_cols(w, cols):
    cols = np.asarray(cols, np.int32)
    out = jnp.take(w, jnp.asarray(np.maximum(cols, 0)), axis=-1)
    return jnp.where(jnp.asarray(cols >= 0), out, 0.0)


def _head_cols(offset, width, pad):
    cols = []
    for h in range(N_HEADS):
        cols += list(range(offset + h * width, offset + (h + 1) * width)) + [-1] * (pad - width)
    return cols


def _rope_head_cols(offset):
    half = DQK // 2
    slot = DQK_PAD // 2
    cols = []
    for h in range(N_HEADS):
        b = offset + h * DQK
        cols += list(range(b, b + half)) + [-1] * (slot - half)
        cols += list(range(b + half, b + DQK)) + [-1] * (slot - half)
    return cols


_ML_GATE_OFF = 2 * ML_QK + 2 * MIX_WIDTH
_ML_COLS = (_head_cols(0, DQK, DQK_PAD) + _head_cols(ML_QK, DQK, DQK_PAD)
            + _head_cols(2 * ML_QK, DV, DV_PAD) + _head_cols(2 * ML_QK + MIX_WIDTH, DV, DV_PAD)
            + list(range(_ML_GATE_OFF + 2 * N_HEADS, _ML_GATE_OFF + 2 * N_HEADS + XATTN_WIDTH))
            + list(range(_ML_GATE_OFF, _ML_GATE_OFF + 2 * N_HEADS)) + [-1] * (LANES - 2 * N_HEADS))
_RET_COLS = (_rope_head_cols(0) + _rope_head_cols(ML_QK)
             + _head_cols(2 * ML_QK, DV, DV_PAD) + _head_cols(2 * ML_QK + MIX_WIDTH, DV, DV_PAD)
             + list(range(2 * ML_QK + 2 * MIX_WIDTH, 2 * ML_QK + 2 * MIX_WIDTH + XATTN_WIDTH)))
_MIX_PAD_COLS = _head_cols(0, DV, DV_PAD)
_XQ_BLK_PADDED = (2 * HEADS_QK + 2 * HEADS_V) // XATTN_WIDTH
_XQ_BLK_S5 = MIX_WIDTH // XATTN_WIDTH


def kernel(x, mem, positions, mem_w_k, mem_w_v, l0_w_in, l0_s5_a_re, l0_s5_a_im, l0_s5_log_dt, l0_s5_b_re, l0_s5_b_im, l0_s5_c_re, l0_s5_c_im, l0_s5_d, l0_s5_w_glu, l0_s5_b_glu, l1_w_in, l1_ml_conv_q, l1_ml_conv_k, l1_ml_b_i, l1_ml_b_f, l1_ml_norm_g, l2_w_in, l2_ret_norm_g, l3_w_in, l3_s5_a_re, l3_s5_a_im, l3_s5_log_dt, l3_s5_b_re, l3_s5_b_im, l3_s5_c_re, l3_s5_c_im, l3_s5_d, l3_s5_w_glu, l3_s5_b_glu, w_out, ln1_g, ln1_b, ln2_g, ln2_b, router_w, router_b, exp_w_gu, exp_b_gu, exp_w_down, exp_b_down):
    bsz, seqlen, d = x.shape
    t = bsz * seqlen
    h = x.reshape(t, d)

    w_kv = jnp.concatenate([mem_w_k, mem_w_v], axis=1).astype(BF16)
    kv = _inproj(mem.reshape(bsz * N_MEM, d), w_kv).astype(BF16)
    mem_k = kv[:, :XATTN_WIDTH].reshape(bsz, N_MEM, XATTN_WIDTH)
    mem_v = kv[:, XATTN_WIDTH:].reshape(bsz, N_MEM, XATTN_WIDTH)

    half = DQK // 2
    inv = ROPE_BASE ** (-jnp.arange(0, DQK, 2, dtype=F32) / DQK)
    zpad = jnp.zeros((DQK_PAD // 2 - half,), F32)
    inv_pad = jnp.concatenate([inv, zpad, inv, zpad]).reshape(1, DQK_PAD)
    sgn_pad = jnp.concatenate([-jnp.ones((half,), F32), zpad, jnp.ones((half,), F32), zpad]
                              ).reshape(1, DQK_PAD)
    pos_f = positions.astype(F32).reshape(t, 1)

    s5_params = {
        0: (l0_s5_a_re, l0_s5_a_im, l0_s5_log_dt, l0_s5_b_re, l0_s5_b_im, l0_s5_c_re, l0_s5_c_im,
            l0_s5_d, l0_s5_w_glu, l0_s5_b_glu),
        3: (l3_s5_a_re, l3_s5_a_im, l3_s5_log_dt, l3_s5_b_re, l3_s5_b_im, l3_s5_c_re, l3_s5_c_im,
            l3_s5_d, l3_s5_w_glu, l3_s5_b_glu),
    }
    w_ins = (l0_w_in, l1_w_in, l2_w_in, l3_w_in)

    for i in range(DEPTH):
        kind = i % 3
        wo = w_out[i]
        if kind == 0:
            proj = _inproj(h, w_ins[i].astype(BF16))
            y_mix = _s5_mixer(proj, bsz, seqlen, *s5_params[i])
            wo_mix = wo[:MIX_WIDTH].astype(BF16)
            xq_blk = _XQ_BLK_S5
        elif kind == 1:
            proj = _inproj(h, _take_cols(w_ins[i], _ML_COLS).astype(BF16))
            cw = jnp.concatenate([_take_cols(l1_ml_conv_q, _head_cols(0, DQK, DQK_PAD)),
                                  _take_cols(l1_ml_conv_k, _head_cols(0, DQK, DQK_PAD))], axis=1)
            gbias = jnp.concatenate([l1_ml_b_i, l1_ml_b_f,
                                     jnp.zeros((LANES - 2 * N_HEADS,), F32)]).reshape(1, LANES)
            norm_g = _take_cols(l1_ml_norm_g, _MIX_PAD_COLS).reshape(1, HEADS_V)
            y_mix = _mlstm_mixer(proj, bsz, seqlen, cw, gbias, norm_g)
            wo_mix = _take_cols(wo[:MIX_WIDTH].T, _MIX_PAD_COLS).T.astype(BF16)
            xq_blk = _XQ_BLK_PADDED
        else:
            proj = _inproj(h, _take_cols(w_ins[i], _RET_COLS).astype(BF16))
            norm_g = _take_cols(l2_ret_norm_g, _MIX_PAD_COLS).reshape(1, HEADS_V)
            y_mix = _ret_mixer(proj, bsz, seqlen, pos_f, inv_pad, sgn_pad, norm_g)
            wo_mix = _take_cols(wo[:MIX_WIDTH].T, _MIX_PAD_COLS).T.astype(BF16)
            xq_blk = _XQ_BLK_PADDED
        wo_mem = wo[MIX_WIDTH:].astype(BF16)
        rw = jnp.pad(router_w[i], ((0, 0), (0, LANES - N_EXPERTS)))
        rb = jnp.concatenate([router_b[i], jnp.full((LANES - N_EXPERTS,), -1e30, F32)]
                             ).reshape(1, LANES)
        h1, h1r, idx, gates = _post_mixer(y_mix, proj, xq_blk, h, mem_k, mem_v, wo_mix, wo_mem,
                                     ln1_g[i].reshape(1, d), ln1_b[i].reshape(1, d), rw, rb, seqlen)
        pos_flat, row_tok, block_exp, n_used = _route(idx[:, :TOP_K])
        yr = _experts(h1r, block_exp, n_used, row_tok, exp_w_gu[i], exp_b_gu[i],
                      exp_w_down[i], exp_b_down[i])
        h = _combine(pos_flat, yr, h1, gates, ln2_g[i].reshape(1, d), ln2_b[i].reshape(1, d))
    return h.reshape(bsz, seqlen, d)
```

```python
import functools
import math

import numpy as np
import jax
import jax.numpy as jnp
from jax import lax
from jax.experimental import pallas as pl
from jax.experimental.pallas import tpu as pltpu

F32 = jnp.float32
BF16 = jnp.bfloat16

D_MODEL = 1024
DEPTH = 4
N_MEM = 256
MIX_WIDTH = 768
XATTN_HEADS = 4
XATTN_WIDTH = 256
XATTN_HEAD_DIM = 64
S5_GROUP = 16
S5_GROUPS = 48
S5_STATE = 64
N_HEADS = 4
DQK = 96
DV = 192
ML_QK = 384
ML_CONV = 4
ML_CHUNK = 64
RET_CHUNK = 128
ROPE_BASE = 10000.0
N_EXPERTS = 32
TOP_K = 4
SWIGLU_LIMIT = 7.0
SWIGLU_ALPHA = 1.702
DEEPNORM_ALPHA = (2.0 * DEPTH) ** 0.25
LN_EPS = 1e-5

LANES = 128
SUBLANES = 8
DQK_PAD = 128
DV_PAD = 256
HEADS_QK = N_HEADS * DQK_PAD
HEADS_V = N_HEADS * DV_PAD
S5_LC = 16
S5_K = S5_LC * S5_GROUP
MOE_BM = 256
ROW_TILE = 256
VMEM_LIMIT = 56 * 1024 * 1024

_NT = (((1,), (1,)), ((), ()))
_TN = (((0,), (0,)), ((), ()))


def _cparams(sem):
    return pltpu.CompilerParams(dimension_semantics=sem, vmem_limit_bytes=VMEM_LIMIT)


def _dot(a, b):
    return jnp.dot(a, b, preferred_element_type=F32)


def _layer_norm_rows(z, g, b):
    mu = jnp.mean(z, axis=-1, keepdims=True)
    d = z - mu
    var = jnp.mean(d * d, axis=-1, keepdims=True)
    return d * lax.rsqrt(var + LN_EPS) * g + b


def _inproj_kernel(x_ref, w_ref, o_ref):
    xb = x_ref[...].astype(BF16)
    n = o_ref.shape[1]
    step = 512
    for c0 in range(0, n, step):
        c1 = min(c0 + step, n)
        o_ref[:, c0:c1] = _dot(xb, w_ref[:, c0:c1])


def _inproj(x, w_bf16, tm=ROW_TILE):
    t, d = x.shape
    n = w_bf16.shape[1]
    return pl.pallas_call(
        _inproj_kernel,
        out_shape=jax.ShapeDtypeStruct((t, n), F32),
        grid=(t // tm,),
        in_specs=[pl.BlockSpec((tm, d), lambda i: (i, 0)),
                  pl.BlockSpec((d, n), lambda i: (0, 0))],
        out_specs=pl.BlockSpec((tm, n), lambda i: (i, 0)),
        compiler_params=_cparams(("parallel",)),
        name="inproj",
    )(x, w_bf16)


S5_LANE_GROUPS = S5_K // S5_GROUP
S5_BLOCKS = MIX_WIDTH // S5_K
S5_SW = S5_LANE_GROUPS * S5_STATE
S5_LB = 512


def _block_diag(m):
    nb, g, a, b = m.shape
    eye = jnp.eye(g, dtype=m.dtype)
    return (m[:, :, :, None, :] * eye[None, :, None, :, None]).reshape(nb, g * a, g * b)


def _s5_prep(a_re, a_im, log_dt, b_re, b_im, c_re, c_im):
    hp = lax.Precision.HIGHEST
    lam_re = jnp.minimum(a_re.astype(F32), -1e-4)
    lam_im = a_im.astype(F32)
    dt = jnp.exp(log_dt.astype(F32))[:, None]
    mag = jnp.exp(dt * lam_re)
    ab_re = mag * jnp.cos(dt * lam_im)
    ab_im = mag * jnp.sin(dt * lam_im)
    den = lam_re * lam_re + lam_im * lam_im
    num_re = ab_re - 1.0
    coef_re = (num_re * lam_re + ab_im * lam_im) / den
    coef_im = (ab_im * lam_re - num_re * lam_im) / den
    bre = b_re.astype(F32)
    bim = b_im.astype(F32)
    bb_re = coef_re[..., None] * bre - coef_im[..., None] * bim
    bb_im = coef_re[..., None] * bim + coef_im[..., None] * bre
    pr = [jnp.ones_like(ab_re)]
    pi = [jnp.zeros_like(ab_im)]
    for _ in range(S5_LC):
        r, i = pr[-1], pi[-1]
        pr.append(r * ab_re - i * ab_im)
        pi.append(r * ab_im + i * ab_re)
    pw_re = jnp.stack(pr)
    pw_im = jnp.stack(pi)
    p_re = pw_re[:S5_LC, :, :, None] * bb_re[None] - pw_im[:S5_LC, :, :, None] * bb_im[None]
    p_im = pw_re[:S5_LC, :, :, None] * bb_im[None] + pw_im[:S5_LC, :, :, None] * bb_re[None]
    cre = c_re.astype(F32)
    cim = c_im.astype(F32)
    kmat = (jnp.einsum('ghp,tgpk->tgkh', cre, p_re, precision=hp)
            - jnp.einsum('ghp,tgpk->tgkh', cim, p_im, precision=hp))
    nb, lg = S5_BLOCKS, S5_LANE_GROUPS
    toep = _block_diag(kmat.reshape(S5_LC * nb, lg, S5_GROUP, S5_GROUP)
                       ).reshape(S5_LC, nb, S5_K, S5_K).transpose(1, 0, 2, 3)
    bmat = jnp.concatenate(
        [_block_diag(bb_re.transpose(0, 2, 1).reshape(nb, lg, S5_GROUP, S5_STATE)),
         _block_diag(bb_im.transpose(0, 2, 1).reshape(nb, lg, S5_GROUP, S5_STATE))], axis=2)
    cmat = jnp.concatenate(
        [_block_diag(cre.transpose(0, 2, 1).reshape(nb, lg, S5_STATE, S5_GROUP)),
         _block_diag(-cim.transpose(0, 2, 1).reshape(nb, lg, S5_STATE, S5_GROUP))], axis=1)
    lane = lambda v: v.reshape(nb, 1, S5_SW)
    avec = jnp.concatenate([lane(ab_re), lane(ab_im), lane(pw_re[S5_LC]), lane(pw_im[S5_LC])],
                           axis=1)
    return toep.astype(BF16), bmat.astype(BF16), cmat.astype(BF16), avec


def _s5_kernel(u_ref, t_ref, b_ref, c_ref, a_ref, y_ref, s_acc, x_prev, x_carry, u_half, y_half,
               *, lb):
    li = pl.program_id(1)
    bsz = u_ref.shape[0]
    ncb = lb // S5_LC
    m = bsz * ncb

    @pl.when(li == 0)
    def _():
        x_carry[...] = jnp.zeros_like(x_carry)

    ar = a_ref[0, 0:1, :]
    ai = a_ref[0, 1:2, :]
    alr = a_ref[0, 2:3, :]
    ali = a_ref[0, 3:4, :]

    def cmul(zr, zi, wr, wi):
        return wr * zr - wi * zi, wr * zi + wi * zr

    n_half = S5_K // LANES
    for hf in range(n_half):
        u_half[hf] = u_ref[:, :, hf * LANES:(hf + 1) * LANES].reshape(bsz * lb, LANES)
    xs = []
    for s in range(S5_LC):
        halves = [jnp.concatenate([u_half[hf, pl.ds(c * S5_LC + s, bsz, stride=lb), :]
                                   for c in range(ncb)], axis=0) for hf in range(n_half)]
        xs.append(jnp.concatenate(halves, axis=1).astype(BF16))
    bmat = b_ref[0]
    sr = jnp.zeros((m, S5_SW), F32)
    si = jnp.zeros((m, S5_SW), F32)
    for s in range(S5_LC):
        bu = _dot(xs[s], bmat)
        sr, si = cmul(sr, si, ar, ai)
        sr = sr + bu[:, :S5_SW]
        si = si + bu[:, S5_SW:]
    s_acc[:, :S5_SW] = sr
    s_acc[:, S5_SW:] = si

    xr = x_carry[:, :S5_SW]
    xi = x_carry[:, S5_SW:]
    for c in range(ncb):
        x_prev[c * bsz:(c + 1) * bsz, :S5_SW] = xr
        x_prev[c * bsz:(c + 1) * bsz, S5_SW:] = xi
        loc = s_acc[c * bsz:(c + 1) * bsz, :]
        xr, xi = cmul(xr, xi, alr, ali)
        xr = xr + loc[:, :S5_SW]
        xi = xi + loc[:, S5_SW:]
    x_carry[:, :S5_SW] = xr
    x_carry[:, S5_SW:] = xi

    cmat = c_ref[0]
    zr = x_prev[:, :S5_SW]
    zi = x_prev[:, S5_SW:]
    for j in range(S5_LC):
        zr, zi = cmul(zr, zi, ar, ai)
        yj = _dot(jnp.concatenate([zr, zi], axis=1).astype(BF16), cmat)
        for s in range(j + 1):
            yj = yj + _dot(xs[s], t_ref[0, j - s])
        for hf in range(n_half):
            for c in range(ncb):
                y_half[hf, pl.ds(c * S5_LC + j, bsz, stride=lb), :] = (
                    yj[c * bsz:(c + 1) * bsz, hf * LANES:(hf + 1) * LANES])
    for hf in range(n_half):
        y_ref[:, :, hf * LANES:(hf + 1) * LANES] = y_half[hf].reshape(bsz, lb, LANES)


def _s5_scan(proj3, mats, lb=S5_LB):
    toep, bmat, cmat, avec = mats
    bsz, seqlen, _ = proj3.shape
    return pl.pallas_call(
        functools.partial(_s5_kernel, lb=lb),
        out_shape=jax.ShapeDtypeStruct((bsz, seqlen, MIX_WIDTH), F32),
        grid=(S5_BLOCKS, seqlen // lb),
        in_specs=[pl.BlockSpec((bsz, lb, S5_K), lambda v, l: (0, l, v)),
                  pl.BlockSpec((1, S5_LC, S5_K, S5_K), lambda v, l: (v, 0, 0, 0)),
                  pl.BlockSpec((1, S5_K, 2 * S5_SW), lambda v, l: (v, 0, 0)),
                  pl.BlockSpec((1, 2 * S5_SW, S5_K), lambda v, l: (v, 0, 0)),
                  pl.BlockSpec((1, 4, S5_SW), lambda v, l: (v, 0, 0))],
        out_specs=pl.BlockSpec((bsz, lb, S5_K), lambda v, l: (0, l, v)),
        scratch_shapes=[pltpu.VMEM((bsz * (lb // S5_LC), 2 * S5_SW), F32),
                        pltpu.VMEM((bsz * (lb // S5_LC), 2 * S5_SW), F32),
                        pltpu.VMEM((bsz, 2 * S5_SW), F32),
                        pltpu.VMEM((S5_K // LANES, bsz * lb, LANES), F32),
                        pltpu.VMEM((S5_K // LANES, bsz * lb, LANES), F32)],
        compiler_params=_cparams(("parallel", "arbitrary")),
        name="s5_scan",
    )(proj3, toep, bmat, cmat, avec)


def _s5_post_kernel(y_ref, u_ref, d_ref, w_ref, b_ref, o_ref):
    y = y_ref[...] + d_ref[...] * u_ref[...]
    y = jax.nn.gelu(y)
    o_ref[...] = y * jax.nn.sigmoid(_dot(y.astype(BF16), w_ref[...]) + b_ref[...])


def _s5_post(y_ssm, proj, d_skip, w_glu_bf16, b_glu, tm=ROW_TILE):
    t = y_ssm.shape[0]
    w = MIX_WIDTH
    return pl.pallas_call(
        _s5_post_kernel,
        out_shape=jax.ShapeDtypeStruct((t, w), F32),
        grid=(t // tm,),
        in_specs=[pl.BlockSpec((tm, w), lambda i: (i, 0)),
                  pl.BlockSpec((tm, w), lambda i: (i, 0)),
                  pl.BlockSpec((1, w), lambda i: (0, 0)),
                  pl.BlockSpec((w, w), lambda i: (0, 0)),
                  pl.BlockSpec((1, w), lambda i: (0, 0))],
        out_specs=pl.BlockSpec((tm, w), lambda i: (i, 0)),
        compiler_params=_cparams(("parallel",)),
        name="s5_post",
    )(y_ssm, proj, d_skip.reshape(1, w), w_glu_bf16, b_glu.reshape(1, w))


def _s5_mixer(proj, bsz, seqlen, a_re, a_im, log_dt, b_re, b_im, c_re, c_im, d_skip, w_glu, b_glu):
    mats = _s5_prep(a_re, a_im, log_dt, b_re, b_im, c_re, c_im)
    y = _s5_scan(proj.reshape(bsz, seqlen, proj.shape[1]), mats)
    return _s5_post(y.reshape(bsz * seqlen, MIX_WIDTH), proj, d_skip, w_glu.astype(BF16), b_glu)


def _head_norm_padded(hv, g):
    lane = lax.broadcasted_iota(jnp.int32, hv.shape, 1)
    real = lane < DV
    mu = jnp.sum(hv, axis=-1, keepdims=True) * (1.0 / DV)
    d = jnp.where(real, hv - mu, 0.0)
    var = jnp.sum(d * d, axis=-1, keepdims=True) * (1.0 / DV)
    return d * lax.rsqrt(var + LN_EPS) * g


def _log_sigmoid(x):
    return jnp.minimum(x, 0.0) - jnp.log(1.0 + jnp.exp(-jnp.abs(x)))


def _mlstm_kernel(q_ref, k_ref, v_ref, o_ref, gt_ref, cw_ref, gb_ref, ng_ref,
                  y_ref, cbuf, c_st, n_st, m_st, *, tl):
    i = pl.program_id(1)

    @pl.when(i == 0)
    def _():
        cbuf[0:SUBLANES, :] = jnp.zeros((SUBLANES, 2 * HEADS_QK), F32)
        c_st[...] = jnp.zeros_like(c_st)
        n_st[...] = jnp.zeros_like(n_st)
        m_st[...] = jnp.zeros_like(m_st)

    cbuf[SUBLANES:SUBLANES + tl, 0:HEADS_QK] = q_ref[...]
    cbuf[SUBLANES:SUBLANES + tl, HEADS_QK:2 * HEADS_QK] = k_ref[...]
    acc = jnp.zeros((tl, 2 * HEADS_QK), F32)
    for w in range(ML_CONV):
        acc = acc + cbuf[pl.ds(SUBLANES - (ML_CONV - 1) + w, tl), :] * cw_ref[w:w + 1, :]
    qk = acc * jax.nn.sigmoid(acc)
    cbuf[0:SUBLANES, :] = cbuf[tl:tl + SUBLANES, :]

    gt = gt_ref[...] + gb_ref[...]
    lf = _log_sigmoid(gt)
    gt_t = gt.T
    lf_t = _log_sigmoid(gt_t)

    cl = ML_CHUNK
    row = lax.broadcasted_iota(jnp.int32, (cl, cl), 0)
    col = lax.broadcasted_iota(jnp.int32, (cl, cl), 1)
    tri = row >= col
    scale = DQK ** -0.5
    for cc in range(tl // cl):
        r0 = cc * cl
        for h in range(N_HEADS):
            ig_col = gt[r0:r0 + cl, h:h + 1]
            lf_col = lf[r0:r0 + cl, N_HEADS + h:N_HEADS + h + 1]
            ig_row = gt_t[h:h + 1, r0:r0 + cl]
            lf_row = lf_t[N_HEADS + h:N_HEADS + h + 1, r0:r0 + cl]
            bcum_col = jnp.sum(jnp.where(tri, lf_row, 0.0), axis=1, keepdims=True)
            bcum_row = jnp.sum(jnp.where(col >= row, lf_col, 0.0), axis=0, keepdims=True)
            btot = jnp.sum(lf_row, axis=1, keepdims=True)
            w_row = btot - bcum_row + ig_row
            m_loc = jnp.max(w_row, axis=1, keepdims=True)
            e_col = jnp.exp(btot - bcum_col + ig_col - m_loc)
            m_prev = m_st[h:h + 1, 0:1]
            c_prev = c_st[h]
            n_prev = n_st[h:h + 1, :]
            q = qk[r0:r0 + cl, h * DQK_PAD:(h + 1) * DQK_PAD] * scale
            k = qk[r0:r0 + cl, HEADS_QK + h * DQK_PAD:HEADS_QK + (h + 1) * DQK_PAD]
            v = v_ref[r0:r0 + cl, h * DV_PAD:(h + 1) * DV_PAD]
            qb = q.astype(BF16)
            kb = k.astype(BF16)
            vb = v.astype(BF16)
            dmat = jnp.where(tri, bcum_col - bcum_row + ig_row, -jnp.inf)
            g_col = bcum_col + m_prev
            m_row = jnp.maximum(g_col, jnp.max(dmat, axis=1, keepdims=True))
            inter = jnp.exp(g_col - m_row)
            s_qk = lax.dot_general(qb, kb, _NT, preferred_element_type=F32) * jnp.exp(dmat - m_row)
            num = inter * _dot(qb, c_prev.astype(BF16)) + _dot(s_qk.astype(BF16), vb)
            den = (inter * jnp.sum(q * n_prev, axis=1, keepdims=True)
                   + jnp.sum(s_qk, axis=1, keepdims=True))
            hv = num / jnp.maximum(jnp.abs(den), jnp.exp(-m_row))
            ke = k * e_col
            kv = lax.dot_general(ke.astype(BF16), vb, _TN, preferred_element_type=F32)
            nk = jnp.sum(ke, axis=0, keepdims=True)
            m_new = jnp.maximum(btot + m_prev, m_loc)
            sa = jnp.exp(btot + m_prev - m_new)
            sb = jnp.exp(m_loc - m_new)
            c_st[h] = sa * c_prev + sb * kv
            n_st[h:h + 1, :] = sa * n_prev + sb * nk
            m_st[h:h + 1, :] = jnp.broadcast_to(m_new, (1, LANES))
            hn = _head_norm_padded(hv, ng_ref[0:1, h * DV_PAD:(h + 1) * DV_PAD])
            og = o_ref[r0:r0 + cl, h * DV_PAD:(h + 1) * DV_PAD]
            y_ref[r0:r0 + cl, h * DV_PAD:(h + 1) * DV_PAD] = jax.nn.sigmoid(og) * hn


def _mlstm_mixer(proj, bsz, seqlen, cw, gbias, norm_g, tl=ROW_TILE):
    t = proj.shape[0]
    nl = seqlen // tl
    rows = lambda b, i: b * nl + i
    return pl.pallas_call(
        functools.partial(_mlstm_kernel, tl=tl),
        out_shape=jax.ShapeDtypeStruct((t, HEADS_V), F32),
        grid=(bsz, nl),
        in_specs=[pl.BlockSpec((tl, HEADS_QK), lambda b, i: (rows(b, i), 0)),
                  pl.BlockSpec((tl, HEADS_QK), lambda b, i: (rows(b, i), 1)),
                  pl.BlockSpec((tl, HEADS_V), lambda b, i: (rows(b, i), 1)),
                  pl.BlockSpec((tl, HEADS_V), lambda b, i: (rows(b, i), 2)),
                  pl.BlockSpec((tl, LANES), lambda b, i: (rows(b, i), 26)),
                  pl.BlockSpec((ML_CONV, 2 * HEADS_QK), lambda b, i: (0, 0)),
                  pl.BlockSpec((1, LANES), lambda b, i: (0, 0)),
                  pl.BlockSpec((1, HEADS_V), lambda b, i: (0, 0))],
        out_specs=pl.BlockSpec((tl, HEADS_V), lambda b, i: (rows(b, i), 0)),
        scratch_shapes=[pltpu.VMEM((tl + SUBLANES, 2 * HEADS_QK), F32),
                        pltpu.VMEM((N_HEADS, DQK_PAD, DV_PAD), F32),
                        pltpu.VMEM((SUBLANES, DQK_PAD), F32),
                        pltpu.VMEM((SUBLANES, LANES), F32)],
        compiler_params=_cparams(("parallel", "arbitrary")),
        name="mlstm",
    )(proj, proj, proj, proj, proj, cw, gbias, norm_g)


def _ret_log_gamma(h):
    return float(np.log(np.float32(1.0) - np.power(np.float32(2.0), np.float32(-5.0 - h))))


def _ret_kernel(q_ref, k_ref, v_ref, g_ref, pos_ref, inv_ref, sgn_ref, ng_ref,
                y_ref, s_st, *, tl):
    i = pl.program_id(1)

    @pl.when(i == 0)
    def _():
        s_st[...] = jnp.zeros_like(s_st)

    ang = pos_ref[...] * inv_ref[...]
    cos_t = jnp.cos(ang)
    sin_t = jnp.sin(ang) * sgn_ref[...]
    cl = RET_CHUNK
    row = lax.broadcasted_iota(jnp.int32, (cl, cl), 0)
    col = lax.broadcasted_iota(jnp.int32, (cl, cl), 1)
    rel = (row - col).astype(F32)
    jcol = lax.broadcasted_iota(jnp.int32, (cl, 1), 0).astype(F32)
    kscale = DQK ** -0.5
    for h in range(N_HEADS):
        lg = _ret_log_gamma(h)
        decay = jnp.where(rel >= 0, jnp.exp(jnp.maximum(rel, 0.0) * lg), 0.0)
        zeta = jnp.exp((cl - 1 - jcol) * lg)
        xi = jnp.exp((jcol + 1.0) * lg)
        chunk_decay = float(np.exp(np.float32(cl) * np.float32(lg)))
        qh = q_ref[:, h * DQK_PAD:(h + 1) * DQK_PAD]
        kh = k_ref[:, h * DQK_PAD:(h + 1) * DQK_PAD]
        qh = qh * cos_t + pltpu.roll(qh, DQK_PAD // 2, 1) * sin_t
        kh = (kh * cos_t + pltpu.roll(kh, DQK_PAD // 2, 1) * sin_t) * kscale
        for cc in range(tl // cl):
            r0 = cc * cl
            qb = qh[r0:r0 + cl].astype(BF16)
            k = kh[r0:r0 + cl]
            kb = k.astype(BF16)
            v = v_ref[r0:r0 + cl, h * DV_PAD:(h + 1) * DV_PAD]
            vb = v.astype(BF16)
            s_prev = s_st[h]
            s = lax.dot_general(qb, kb, _NT, preferred_element_type=F32) * decay
            intra = _dot(s.astype(BF16), vb)
            cross = _dot(qb, s_prev.astype(BF16)) * xi
            r = lax.dot_general((k * zeta).astype(BF16), vb, _TN, preferred_element_type=F32)
            s_st[h] = chunk_decay * s_prev + r
            hn = _head_norm_padded(intra + cross, ng_ref[0:1, h * DV_PAD:(h + 1) * DV_PAD])
            gate = g_ref[r0:r0 + cl, h * DV_PAD:(h + 1) * DV_PAD]
            y_ref[r0:r0 + cl, h * DV_PAD:(h + 1) * DV_PAD] = gate * jax.nn.sigmoid(gate) * hn


def _ret_mixer(proj, bsz, seqlen, pos_f, inv_pad, sgn_pad, norm_g, tl=ROW_TILE):
    t = proj.shape[0]
    nl = seqlen // tl
    rows = lambda b, i: b * nl + i
    return pl.pallas_call(
        functools.partial(_ret_kernel, tl=tl),
        out_shape=jax.ShapeDtypeStruct((t, HEADS_V), F32),
        grid=(bsz, nl),
        in_specs=[pl.BlockSpec((tl, HEADS_QK), lambda b, i: (rows(b, i), 0)),
                  pl.BlockSpec((tl, HEADS_QK), lambda b, i: (rows(b, i), 1)),
                  pl.BlockSpec((tl, HEADS_V), lambda b, i: (rows(b, i), 1)),
                  pl.BlockSpec((tl, HEADS_V), lambda b, i: (rows(b, i), 2)),
                  pl.BlockSpec((tl, 1), lambda b, i: (rows(b, i), 0)),
                  pl.BlockSpec((1, DQK_PAD), lambda b, i: (0, 0)),
                  pl.BlockSpec((1, DQK_PAD), lambda b, i: (0, 0)),
                  pl.BlockSpec((1, HEADS_V), lambda b, i: (0, 0))],
        out_specs=pl.BlockSpec((tl, HEADS_V), lambda b, i: (rows(b, i), 0)),
        scratch_shapes=[pltpu.VMEM((N_HEADS, DQK_PAD, DV_PAD), F32)],
        compiler_params=_cparams(("parallel", "arbitrary")),
        name="retention",
    )(proj, proj, proj, proj, pos_f, inv_pad, sgn_pad, norm_g)


def _post_kernel(ym_ref, xq_ref, h_ref, mk_ref, mv_ref, wom_ref, wox_ref, g_ref, b_ref,
                 rw_ref, rb_ref, h1_ref, h1r_ref, idx_ref, gate_ref):
    tl = xq_ref.shape[0]
    xq = xq_ref[...] * (XATTN_HEAD_DIM ** -0.5)
    lane = lax.broadcasted_iota(jnp.int32, (tl, XATTN_WIDTH), 1)
    head = lane // XATTN_HEAD_DIM
    mk = mk_ref[0]
    mv = mv_ref[0]
    ymem = jnp.zeros((tl, XATTN_WIDTH), F32)
    for hh in range(XATTN_HEADS):
        sel = head == hh
        qh = jnp.where(sel, xq, 0.0).astype(BF16)
        s = lax.dot_general(qh, mk, _NT, preferred_element_type=F32)
        s = s - jnp.max(s, axis=-1, keepdims=True)
        p = jnp.exp(s)
        p = p / jnp.sum(p, axis=-1, keepdims=True)
        ymem = jnp.where(sel, _dot(p.astype(BF16), mv), ymem)
    y = _dot(ym_ref[...].astype(BF16), wom_ref[...]) + _dot(ymem.astype(BF16), wox_ref[...])
    h1 = _layer_norm_rows(DEEPNORM_ALPHA * h_ref[...] + y, g_ref[...], b_ref[...])
    h1_ref[...] = h1
    h1r_ref[:, 0, :] = h1
    logits = jnp.dot(h1, rw_ref[...], preferred_element_type=F32,
                     precision=lax.Precision.HIGHEST) + rb_ref[...]
    ln = lax.broadcasted_iota(jnp.int32, logits.shape, 1)
    vals = logits
    tv, ti = [], []
    for _ in range(TOP_K):
        m = jnp.max(vals, axis=-1, keepdims=True)
        ix = jnp.min(jnp.where(vals == m, ln, LANES), axis=-1, keepdims=True)
        tv.append(m)
        ti.append(ix)
        vals = jnp.where(ln == ix, -jnp.inf, vals)
    ex = [jnp.exp(v - tv[0]) for v in tv]
    tot = ex[0] + ex[1] + ex[2] + ex[3]
    idx_out = jnp.zeros(logits.shape, jnp.int32)
    gate_out = jnp.zeros(logits.shape, F32)
    for k in range(TOP_K):
        idx_out = jnp.where(ln == k, ti[k], idx_out)
        gate_out = jnp.where(ln == k, ex[k] / tot, gate_out)
    idx_ref[...] = idx_out
    gate_ref[...] = gate_out


def _post_mixer(y_mix, proj, xq_blk, h, mem_k, mem_v, wo_mix, wo_mem, ln_g, ln_b, rw, rb,
                seqlen, tl=ROW_TILE):
    t, cm = y_mix.shape
    nl = seqlen // tl
    d = D_MODEL
    full = lambda a, b: pl.BlockSpec((a, b), lambda i: (0, 0))
    return pl.pallas_call(
        _post_kernel,
        out_shape=(jax.ShapeDtypeStruct((t, d), F32),
                   jax.ShapeDtypeStruct((t, 1, d), F32),
                   jax.ShapeDtypeStruct((t, LANES), jnp.int32),
                   jax.ShapeDtypeStruct((t, LANES), F32)),
        grid=(t // tl,),
        in_specs=[pl.BlockSpec((tl, cm), lambda i: (i, 0)),
                  pl.BlockSpec((tl, XATTN_WIDTH), lambda i: (i, xq_blk)),
                  pl.BlockSpec((tl, d), lambda i: (i, 0)),
                  pl.BlockSpec((1, N_MEM, XATTN_WIDTH), lambda i: (i // nl, 0, 0)),
                  pl.BlockSpec((1, N_MEM, XATTN_WIDTH), lambda i: (i // nl, 0, 0)),
                  full(cm, d), full(XATTN_WIDTH, d), full(1, d), full(1, d),
                  full(d, LANES), full(1, LANES)],
        out_specs=(pl.BlockSpec((tl, d), lambda i: (i, 0)),
                   pl.BlockSpec((tl, 1, d), lambda i: (i, 0, 0)),
                   pl.BlockSpec((tl, LANES), lambda i: (i, 0)),
                   pl.BlockSpec((tl, LANES), lambda i: (i, 0))),
        compiler_params=_cparams(("parallel",)),
        name="post_mixer",
    )(y_mix, proj, h, mem_k, mem_v, wo_mix, wo_mem, ln_g, ln_b, rw, rb)


def _gather_copy(src_hbm, src_row, dst, dst_row, sem):
    return pltpu.make_async_copy(src_hbm.at[src_row], dst.at[pl.ds(dst_row, 1), :], sem)


def _expert_kernel(bexp_ref, nused_ref, rtok_ref, x_hbm, wgu_ref, bgu_ref, wd_ref, bd_ref,
                   y_ref, xbuf, wgu_bf, wd_bf, sems, *, n_blocks):
    i = pl.program_id(0)
    bm = xbuf.shape[1]
    de = wd_ref.shape[2]
    nused = nused_ref[0]
    slot = lax.rem(i, 2)

    def issue_block(blk, dst_slot):
        base = blk * bm

        def body(r, c):
            _gather_copy(x_hbm, rtok_ref[base + r], xbuf.at[dst_slot], r, sems.at[dst_slot]).start()
            return c

        lax.fori_loop(0, bm, body, 0, unroll=8)

    def wait_slot(s):
        def body(r, c):
            _gather_copy(x_hbm, 0, xbuf.at[s], r, sems.at[s]).wait()
            return c
        lax.fori_loop(0, bm, body, 0, unroll=8)

    @pl.when(i == 0)
    def _():
        issue_block(0, 0)

    @pl.when(i <= nused)
    def _():
        wait_slot(slot)

    @pl.when(i < nused)
    def _():
        prev = bexp_ref[jnp.maximum(i - 1, 0)]

        @pl.when((i == 0) | (prev != bexp_ref[i]))
        def _():
            wgu_bf[...] = wgu_ref[0, 0].astype(BF16)
            wd_bf[...] = wd_ref[0, 0].astype(BF16)

        issue_block(jnp.minimum(i + 1, n_blocks - 1), 1 - slot)
        xb = xbuf[slot].astype(BF16)
        gu = _dot(xb, wgu_bf[...]) + bgu_ref[0, 0]
        x_glu = jnp.minimum(gu[:, :de], SWIGLU_LIMIT)
        x_lin = jnp.clip(gu[:, de:], -SWIGLU_LIMIT, SWIGLU_LIMIT)
        act = x_glu * jax.nn.sigmoid(SWIGLU_ALPHA * x_glu) * (x_lin + 1.0)
        y_ref[:, 0, :] = _dot(act.astype(BF16), wd_bf[...]) + bd_ref[0, 0]

    @pl.when(i >= nused)
    def _():
        y_ref[...] = jnp.zeros_like(y_ref)

    @pl.when((i == n_blocks - 1) & (i < nused))
    def _():
        wait_slot(1 - slot)


def _experts(h1r, block_exp, n_used, row_tok, layer, w_gu, b_gu, w_down, b_down, bm=MOE_BM):
    t, _, d = h1r.shape
    n_rows = row_tok.shape[0]
    n_blocks = n_rows // bm
    nl, ne, _, de2 = w_gu.shape
    de = de2 // 2
    grid_spec = pltpu.PrefetchScalarGridSpec(
        num_scalar_prefetch=3,
        grid=(n_blocks,),
        in_specs=[pl.BlockSpec(memory_space=pl.ANY),
                  pl.BlockSpec((1, 1, d, de2), lambda i, be, nu, rt: (layer, be[i], 0, 0)),
                  pl.BlockSpec((1, 1, 1, de2), lambda i, be, nu, rt: (layer, be[i], 0, 0)),
                  pl.BlockSpec((1, 1, de, d), lambda i, be, nu, rt: (layer, be[i], 0, 0)),
                  pl.BlockSpec((1, 1, 1, d), lambda i, be, nu, rt: (layer, be[i], 0, 0))],
        out_specs=pl.BlockSpec((bm, 1, d), lambda i, be, nu, rt: (i, 0, 0)),
        scratch_shapes=[pltpu.VMEM((2, bm, d), F32),
                        pltpu.VMEM((d, de2), BF16),
                        pltpu.VMEM((de, d), BF16),
                        pltpu.SemaphoreType.DMA((2,))],
    )
    return pl.pallas_call(
        functools.partial(_expert_kernel, n_blocks=n_blocks),
        out_shape=jax.ShapeDtypeStruct((n_rows, 1, d), F32),
        grid_spec=grid_spec,
        compiler_params=_cparams(("arbitrary",)),
        name="experts",
    )(block_exp, n_used, row_tok, h1r, w_gu, b_gu.reshape(nl, ne, 1, de2), w_down,
      b_down.reshape(nl, ne, 1, d))


def _combine_kernel(pos_ref, yr_hbm, h1_ref, gate_ref, g_ref, b_ref, o_ref, buf, sems, *, tl, n_tiles):
    i = pl.program_id(0)
    slot = lax.rem(i, 2)

    def issue_tile(tile, dst_slot):
        base = tile * tl

        def body(t, c):
            for k in range(TOP_K):
                _gather_copy(yr_hbm, pos_ref[(base + t) * TOP_K + k], buf.at[dst_slot],
                             k * tl + t, sems.at[dst_slot]).start()
            return c

        lax.fori_loop(0, tl, body, 0, unroll=4)

    def wait_slot(s):
        def body(r, c):
            _gather_copy(yr_hbm, 0, buf.at[s], r, sems.at[s]).wait()
            return c
        lax.fori_loop(0, TOP_K * tl, body, 0, unroll=8)

    @pl.when(i == 0)
    def _():
        issue_tile(0, 0)

    @pl.when(i + 1 < n_tiles)
    def _():
        issue_tile(i + 1, 1 - slot)

    wait_slot(slot)
    gate = gate_ref[...]
    acc = DEEPNORM_ALPHA * h1_ref[...]
    for k in range(TOP_K):
        acc = acc + gate[:, k:k + 1] * buf[slot, k * tl:(k + 1) * tl, :]
    o_ref[...] = _layer_norm_rows(acc, g_ref[...], b_ref[...])


def _combine(pos_flat, yr, h1, gates, ln_g, ln_b, tl=ROW_TILE):
    t, d = h1.shape
    n_tiles = t // tl
    grid_spec = pltpu.PrefetchScalarGridSpec(
        num_scalar_prefetch=1,
        grid=(n_tiles,),
        in_specs=[pl.BlockSpec(memory_space=pl.ANY),
                  pl.BlockSpec((tl, d), lambda i, p: (i, 0)),
                  pl.BlockSpec((tl, LANES), lambda i, p: (i, 0)),
                  pl.BlockSpec((1, d), lambda i, p: (0, 0)),
                  pl.BlockSpec((1, d), lambda i, p: (0, 0))],
        out_specs=pl.BlockSpec((tl, d), lambda i, p: (i, 0)),
        scratch_shapes=[pltpu.VMEM((2, TOP_K * tl, d), F32),
                        pltpu.SemaphoreType.DMA((2,))],
    )
    return pl.pallas_call(
        functools.partial(_combine_kernel, tl=tl, n_tiles=n_tiles),
        out_shape=jax.ShapeDtypeStruct((t, d), F32),
        grid_spec=grid_spec,
        compiler_params=_cparams(("arbitrary",)),
        name="combine",
    )(pos_flat, yr, h1, gates, ln_g, ln_b)


def _route(idx, bm=MOE_BM):
    t = idx.shape[0]
    n_assign = t * TOP_K
    n_rows = (-(-n_assign // bm) + N_EXPERTS) * bm
    n_blocks = n_rows // bm
    onehot = (idx[:, :, None] == jnp.arange(N_EXPERTS, dtype=jnp.int32)[None, None, :])
    sel = jnp.sum(onehot.astype(jnp.int32), axis=1)
    csum = jnp.cumsum(sel, axis=0)
    counts = csum[-1]
    rank = csum - sel
    padded = ((counts + bm - 1) // bm) * bm
    pad_end = jnp.cumsum(padded)
    pad_start = pad_end - padded
    pos = pad_start[idx] + jnp.take_along_axis(rank, idx, axis=1)
    tok = jnp.broadcast_to(jnp.arange(t, dtype=jnp.int32)[:, None], (t, TOP_K))
    row_tok = jnp.zeros((n_rows,), jnp.int32).at[pos.reshape(-1)].set(tok.reshape(-1))
    block_start = jnp.arange(n_blocks, dtype=jnp.int32) * bm
    block_exp = jnp.minimum(
        jnp.sum((block_start[:, None] >= pad_end[None, :]).astype(jnp.int32), axis=1),
        N_EXPERTS - 1).astype(jnp.int32)
    n_used = (pad_end[-1] // bm).astype(jnp.int32).reshape(1)
    return pos.reshape(-1).astype(jnp.int32), row_tok, block_exp, n_used


def _take_cols(w, cols):
    cols = np.asarray(cols, np.int32)
    out = jnp.take(w, jnp.asarray(np.maximum(cols, 0)), axis=-1)
    return jnp.where(jnp.asarray(cols >= 0), out, 0.0)


def _head_cols(offset, width, pad):
    cols = []
    for h in range(N_HEADS):
        cols += list(range(offset + h * width, offset + (h + 1) * width)) + [-1] * (pad - width)
    return cols


def _rope_head_cols(offset):
    half = DQK // 2
    slot = DQK_PAD // 2
    cols = []
    for h in range(N_HEADS):
        b = offset + h * DQK
        cols += list(range(b, b + half)) + [-1] * (slot - half)
        cols += list(range(b + half, b + DQK)) + [-1] * (slot - half)
    return cols


_ML_GATE_OFF = 2 * ML_QK + 2 * MIX_WIDTH
_ML_COLS = (_head_cols(0, DQK, DQK_PAD) + _head_cols(ML_QK, DQK, DQK_PAD)
            + _head_cols(2 * ML_QK, DV, DV_PAD) + _head_cols(2 * ML_QK + MIX_WIDTH, DV, DV_PAD)
            + list(range(_ML_GATE_OFF + 2 * N_HEADS, _ML_GATE_OFF + 2 * N_HEADS + XATTN_WIDTH))
            + list(range(_ML_GATE_OFF, _ML_GATE_OFF + 2 * N_HEADS)) + [-1] * (LANES - 2 * N_HEADS))
_RET_COLS = (_rope_head_cols(0) + _rope_head_cols(ML_QK)
             + _head_cols(2 * ML_QK, DV, DV_PAD) + _head_cols(2 * ML_QK + MIX_WIDTH, DV, DV_PAD)
             + list(range(2 * ML_QK + 2 * MIX_WIDTH, 2 * ML_QK + 2 * MIX_WIDTH + XATTN_WIDTH)))
_MIX_PAD_COLS = _head_cols(0, DV, DV_PAD)
_XQ_BLK_PADDED = (2 * HEADS_QK + 2 * HEADS_V) // XATTN_WIDTH
_XQ_BLK_S5 = MIX_WIDTH // XATTN_WIDTH


def kernel(x, mem, positions, mem_w_k, mem_w_v, l0_w_in, l0_s5_a_re, l0_s5_a_im, l0_s5_log_dt, l0_s5_b_re, l0_s5_b_im, l0_s5_c_re, l0_s5_c_im, l0_s5_d, l0_s5_w_glu, l0_s5_b_glu, l1_w_in, l1_ml_conv_q, l1_ml_conv_k, l1_ml_b_i, l1_ml_b_f, l1_ml_norm_g, l2_w_in, l2_ret_norm_g, l3_w_in, l3_s5_a_re, l3_s5_a_im, l3_s5_log_dt, l3_s5_b_re, l3_s5_b_im, l3_s5_c_re, l3_s5_c_im, l3_s5_d, l3_s5_w_glu, l3_s5_b_glu, w_out, ln1_g, ln1_b, ln2_g, ln2_b, router_w, router_b, exp_w_gu, exp_b_gu, exp_w_down, exp_b_down):
    bsz, seqlen, d = x.shape
    t = bsz * seqlen
    h = x.reshape(t, d)

    w_kv = jnp.concatenate([mem_w_k, mem_w_v], axis=1).astype(BF16)
    kv = _inproj(mem.reshape(bsz * N_MEM, d), w_kv).astype(BF16)
    mem_k = kv[:, :XATTN_WIDTH].reshape(bsz, N_MEM, XATTN_WIDTH)
    mem_v = kv[:, XATTN_WIDTH:].reshape(bsz, N_MEM, XATTN_WIDTH)

    half = DQK // 2
    inv = ROPE_BASE ** (-jnp.arange(0, DQK, 2, dtype=F32) / DQK)
    zpad = jnp.zeros((DQK_PAD // 2 - half,), F32)
    inv_pad = jnp.concatenate([inv, zpad, inv, zpad]).reshape(1, DQK_PAD)
    sgn_pad = jnp.concatenate([-jnp.ones((half,), F32), zpad, jnp.ones((half,), F32), zpad]
                              ).reshape(1, DQK_PAD)
    pos_f = positions.astype(F32).reshape(t, 1)

    s5_params = {
        0: (l0_s5_a_re, l0_s5_a_im, l0_s5_log_dt, l0_s5_b_re, l0_s5_b_im, l0_s5_c_re, l0_s5_c_im,
            l0_s5_d, l0_s5_w_glu, l0_s5_b_glu),
        3: (l3_s5_a_re, l3_s5_a_im, l3_s5_log_dt, l3_s5_b_re, l3_s5_b_im, l3_s5_c_re, l3_s5_c_im,
            l3_s5_d, l3_s5_w_glu, l3_s5_b_glu),
    }
    w_ins = (l0_w_in, l1_w_in, l2_w_in, l3_w_in)

    for i in range(DEPTH):
        kind = i % 3
        wo = w_out[i]
        if kind == 0:
            proj = _inproj(h, w_ins[i].astype(BF16))
            y_mix = _s5_mixer(proj, bsz, seqlen, *s5_params[i])
            wo_mix = wo[:MIX_WIDTH].astype(BF16)
            xq_blk = _XQ_BLK_S5
        elif kind == 1:
            proj = _inproj(h, _take_cols(w_ins[i], _ML_COLS).astype(BF16))
            cw = jnp.concatenate([_take_cols(l1_ml_conv_q, _head_cols(0, DQK, DQK_PAD)),
                                  _take_cols(l1_ml_conv_k, _head_cols(0, DQK, DQK_PAD))], axis=1)
            gbias = jnp.concatenate([l1_ml_b_i, l1_ml_b_f,
                                     jnp.zeros((LANES - 2 * N_HEADS,), F32)]).reshape(1, LANES)
            norm_g = _take_cols(l1_ml_norm_g, _MIX_PAD_COLS).reshape(1, HEADS_V)
            y_mix = _mlstm_mixer(proj, bsz, seqlen, cw, gbias, norm_g)
            wo_mix = _take_cols(wo[:MIX_WIDTH].T, _MIX_PAD_COLS).T.astype(BF16)
            xq_blk = _XQ_BLK_PADDED
        else:
            proj = _inproj(h, _take_cols(w_ins[i], _RET_COLS).astype(BF16))
            norm_g = _take_cols(l2_ret_norm_g, _MIX_PAD_COLS).reshape(1, HEADS_V)
            y_mix = _ret_mixer(proj, bsz, seqlen, pos_f, inv_pad, sgn_pad, norm_g)
            wo_mix = _take_cols(wo[:MIX_WIDTH].T, _MIX_PAD_COLS).T.astype(BF16)
            xq_blk = _XQ_BLK_PADDED
        wo_mem = wo[MIX_WIDTH:].astype(BF16)
        rw = jnp.pad(router_w[i], ((0, 0), (0, LANES - N_EXPERTS)))
        rb = jnp.concatenate([router_b[i], jnp.full((LANES - N_EXPERTS,), -1e30, F32)]
                             ).reshape(1, LANES)
        h1, h1r, idx, gates = _post_mixer(y_mix, proj, xq_blk, h, mem_k, mem_v, wo_mix, wo_mem,
                                          ln1_g[i].reshape(1, d), ln1_b[i].reshape(1, d), rw, rb,
                                          seqlen)
        pos_flat, row_tok, block_exp, n_used = _route(idx[:, :TOP_K])
        yr = _experts(h1r, block_exp, n_used, row_tok, i, exp_w_gu, exp_b_gu,
                      exp_w_down, exp_b_down)
        h = _combine(pos_flat, yr, h1, gates, ln2_g[i].reshape(1, d), ln2_b[i].reshape(1, d))
    return h.reshape(bsz, seqlen, d)
```

```python
import functools
import math

import numpy as np
import jax
import jax.numpy as jnp
from jax import lax
from jax.experimental import pallas as pl
from jax.experimental.pallas import tpu as pltpu
from jax.experimental.pallas import tpu_sc as plsc

F32 = jnp.float32
BF16 = jnp.bfloat16

D_MODEL = 1024
DEPTH = 4
N_MEM = 256
MIX_WIDTH = 768
XATTN_HEADS = 4
XATTN_WIDTH = 256
XATTN_HEAD_DIM = 64
S5_GROUP = 16
S5_GROUPS = 48
S5_STATE = 64
N_HEADS = 4
DQK = 96
DV = 192
ML_QK = 384
ML_CONV = 4
ML_CHUNK = 64
RET_CHUNK = 128
ROPE_BASE = 10000.0
N_EXPERTS = 32
TOP_K = 4
SWIGLU_LIMIT = 7.0
SWIGLU_ALPHA = 1.702
DEEPNORM_ALPHA = (2.0 * DEPTH) ** 0.25
LN_EPS = 1e-5

LANES = 128
SUBLANES = 8
DQK_PAD = 128
DV_PAD = 256
HEADS_QK = N_HEADS * DQK_PAD
HEADS_V = N_HEADS * DV_PAD
S5_LC = 16
S5_K = S5_LC * S5_GROUP
MOE_BM = 256
ROW_TILE = 256
VMEM_LIMIT = 56 * 1024 * 1024

_NT = (((1,), (1,)), ((), ()))
_TN = (((0,), (0,)), ((), ()))


def _cparams(sem):
    return pltpu.CompilerParams(dimension_semantics=sem, vmem_limit_bytes=VMEM_LIMIT)


def _dot(a, b):
    return jnp.dot(a, b, preferred_element_type=F32)


def _layer_norm_rows(z, g, b):
    mu = jnp.mean(z, axis=-1, keepdims=True)
    d = z - mu
    var = jnp.mean(d * d, axis=-1, keepdims=True)
    return d * lax.rsqrt(var + LN_EPS) * g + b


def _inproj_kernel(x_ref, w_ref, o_ref):
    xb = x_ref[...].astype(BF16)
    n = o_ref.shape[1]
    step = 512
    for c0 in range(0, n, step):
        c1 = min(c0 + step, n)
        o_ref[:, c0:c1] = _dot(xb, w_ref[:, c0:c1])


def _inproj(x, w_bf16, tm=ROW_TILE):
    t, d = x.shape
    n = w_bf16.shape[1]
    return pl.pallas_call(
        _inproj_kernel,
        out_shape=jax.ShapeDtypeStruct((t, n), F32),
        grid=(t // tm,),
        in_specs=[pl.BlockSpec((tm, d), lambda i: (i, 0)),
                  pl.BlockSpec((d, n), lambda i: (0, 0))],
        out_specs=pl.BlockSpec((tm, n), lambda i: (i, 0)),
        compiler_params=_cparams(("parallel",)),
        name="inproj",
    )(x, w_bf16)


S5_LANE_GROUPS = S5_K // S5_GROUP
S5_BLOCKS = MIX_WIDTH // S5_K
S5_SW = S5_LANE_GROUPS * S5_STATE
S5_LB = 512


def _block_diag(m):
    nb, g, a, b = m.shape
    eye = jnp.eye(g, dtype=m.dtype)
    return (m[:, :, :, None, :] * eye[None, :, None, :, None]).reshape(nb, g * a, g * b)


def _s5_prep(a_re, a_im, log_dt, b_re, b_im, c_re, c_im):
    hp = lax.Precision.HIGHEST
    lam_re = jnp.minimum(a_re.astype(F32), -1e-4)
    lam_im = a_im.astype(F32)
    dt = jnp.exp(log_dt.astype(F32))[:, None]
    mag = jnp.exp(dt * lam_re)
    ab_re = mag * jnp.cos(dt * lam_im)
    ab_im = mag * jnp.sin(dt * lam_im)
    den = lam_re * lam_re + lam_im * lam_im
    num_re = ab_re - 1.0
    coef_re = (num_re * lam_re + ab_im * lam_im) / den
    coef_im = (ab_im * lam_re - num_re * lam_im) / den
    bre = b_re.astype(F32)
    bim = b_im.astype(F32)
    bb_re = coef_re[..., None] * bre - coef_im[..., None] * bim
    bb_im = coef_re[..., None] * bim + coef_im[..., None] * bre
    pr = [jnp.ones_like(ab_re)]
    pi = [jnp.zeros_like(ab_im)]
    for _ in range(S5_LC):
        r, i = pr[-1], pi[-1]
        pr.append(r * ab_re - i * ab_im)
        pi.append(r * ab_im + i * ab_re)
    pw_re = jnp.stack(pr)
    pw_im = jnp.stack(pi)
    p_re = pw_re[:S5_LC, :, :, None] * bb_re[None] - pw_im[:S5_LC, :, :, None] * bb_im[None]
    p_im = pw_re[:S5_LC, :, :, None] * bb_im[None] + pw_im[:S5_LC, :, :, None] * bb_re[None]
    cre = c_re.astype(F32)
    cim = c_im.astype(F32)
    kmat = (jnp.einsum('ghp,tgpk->tgkh', cre, p_re, precision=hp)
            - jnp.einsum('ghp,tgpk->tgkh', cim, p_im, precision=hp))
    nb, lg = S5_BLOCKS, S5_LANE_GROUPS
    toep = _block_diag(kmat.reshape(S5_LC * nb, lg, S5_GROUP, S5_GROUP)
                       ).reshape(S5_LC, nb, S5_K, S5_K).transpose(1, 0, 2, 3)
    bmat = jnp.concatenate(
        [_block_diag(bb_re.transpose(0, 2, 1).reshape(nb, lg, S5_GROUP, S5_STATE)),
         _block_diag(bb_im.transpose(0, 2, 1).reshape(nb, lg, S5_GROUP, S5_STATE))], axis=2)
    cmat = jnp.concatenate(
        [_block_diag(cre.transpose(0, 2, 1).reshape(nb, lg, S5_STATE, S5_GROUP)),
         _block_diag(-cim.transpose(0, 2, 1).reshape(nb, lg, S5_STATE, S5_GROUP))], axis=1)
    lane = lambda v: v.reshape(nb, 1, S5_SW)
    avec = jnp.concatenate([lane(ab_re), lane(ab_im), lane(pw_re[S5_LC]), lane(pw_im[S5_LC])],
                           axis=1)
    return toep.astype(BF16), bmat.astype(BF16), cmat.astype(BF16), avec


def _s5_kernel(u_ref, t_ref, b_ref, c_ref, a_ref, y_ref, s_acc, x_prev, x_carry, u_half, y_half,
               *, lb):
    li = pl.program_id(1)
    bsz = u_ref.shape[0]
    ncb = lb // S5_LC
    m = bsz * ncb

    @pl.when(li == 0)
    def _():
        x_carry[...] = jnp.zeros_like(x_carry)

    ar = a_ref[0, 0:1, :]
    ai = a_ref[0, 1:2, :]
    alr = a_ref[0, 2:3, :]
    ali = a_ref[0, 3:4, :]

    def cmul(zr, zi, wr, wi):
        return wr * zr - wi * zi, wr * zi + wi * zr

    n_half = S5_K // LANES
    for hf in range(n_half):
        u_half[hf] = u_ref[:, :, hf * LANES:(hf + 1) * LANES].reshape(bsz * lb, LANES)
    xs = []
    for s in range(S5_LC):
        halves = [jnp.concatenate([u_half[hf, pl.ds(c * S5_LC + s, bsz, stride=lb), :]
                                   for c in range(ncb)], axis=0) for hf in range(n_half)]
        xs.append(jnp.concatenate(halves, axis=1).astype(BF16))
    bmat = b_ref[0]
    sr = jnp.zeros((m, S5_SW), F32)
    si = jnp.zeros((m, S5_SW), F32)
    for s in range(S5_LC):
        bu = _dot(xs[s], bmat)
        sr, si = cmul(sr, si, ar, ai)
        sr = sr + bu[:, :S5_SW]
        si = si + bu[:, S5_SW:]
    s_acc[:, :S5_SW] = sr
    s_acc[:, S5_SW:] = si

    xr = x_carry[:, :S5_SW]
    xi = x_carry[:, S5_SW:]
    for c in range(ncb):
        x_prev[c * bsz:(c + 1) * bsz, :S5_SW] = xr
        x_prev[c * bsz:(c + 1) * bsz, S5_SW:] = xi
        loc = s_acc[c * bsz:(c + 1) * bsz, :]
        xr, xi = cmul(xr, xi, alr, ali)
        xr = xr + loc[:, :S5_SW]
        xi = xi + loc[:, S5_SW:]
    x_carry[:, :S5_SW] = xr
    x_carry[:, S5_SW:] = xi

    cmat = c_ref[0]
    zr = x_prev[:, :S5_SW]
    zi = x_prev[:, S5_SW:]
    for j in range(S5_LC):
        zr, zi = cmul(zr, zi, ar, ai)
        yj = _dot(jnp.concatenate([zr, zi], axis=1).astype(BF16), cmat)
        for s in range(j + 1):
            yj = yj + _dot(xs[s], t_ref[0, j - s])
        for hf in range(n_half):
            for c in range(ncb):
                y_half[hf, pl.ds(c * S5_LC + j, bsz, stride=lb), :] = (
                    yj[c * bsz:(c + 1) * bsz, hf * LANES:(hf + 1) * LANES])
    for hf in range(n_half):
        y_ref[:, :, hf * LANES:(hf + 1) * LANES] = y_half[hf].reshape(bsz, lb, LANES)


def _s5_scan(proj3, mats, lb=S5_LB):
    toep, bmat, cmat, avec = mats
    bsz, seqlen, _ = proj3.shape
    return pl.pallas_call(
        functools.partial(_s5_kernel, lb=lb),
        out_shape=jax.ShapeDtypeStruct((bsz, seqlen, MIX_WIDTH), F32),
        grid=(S5_BLOCKS, seqlen // lb),
        in_specs=[pl.BlockSpec((bsz, lb, S5_K), lambda v, l: (0, l, v)),
                  pl.BlockSpec((1, S5_LC, S5_K, S5_K), lambda v, l: (v, 0, 0, 0)),
                  pl.BlockSpec((1, S5_K, 2 * S5_SW), lambda v, l: (v, 0, 0)),
                  pl.BlockSpec((1, 2 * S5_SW, S5_K), lambda v, l: (v, 0, 0)),
                  pl.BlockSpec((1, 4, S5_SW), lambda v, l: (v, 0, 0))],
        out_specs=pl.BlockSpec((bsz, lb, S5_K), lambda v, l: (0, l, v)),
        scratch_shapes=[pltpu.VMEM((bsz * (lb // S5_LC), 2 * S5_SW), F32),
                        pltpu.VMEM((bsz * (lb // S5_LC), 2 * S5_SW), F32),
                        pltpu.VMEM((bsz, 2 * S5_SW), F32),
                        pltpu.VMEM((S5_K // LANES, bsz * lb, LANES), F32),
                        pltpu.VMEM((S5_K // LANES, bsz * lb, LANES), F32)],
        compiler_params=_cparams(("parallel", "arbitrary")),
        name="s5_scan",
    )(proj3, toep, bmat, cmat, avec)


def _s5_post_kernel(y_ref, u_ref, d_ref, w_ref, b_ref, o_ref):
    y = y_ref[...] + d_ref[...] * u_ref[...]
    y = jax.nn.gelu(y)
    o_ref[...] = y * jax.nn.sigmoid(_dot(y.astype(BF16), w_ref[...]) + b_ref[...])


def _s5_post(y_ssm, proj, d_skip, w_glu_bf16, b_glu, tm=ROW_TILE):
    t = y_ssm.shape[0]
    w = MIX_WIDTH
    return pl.pallas_call(
        _s5_post_kernel,
        out_shape=jax.ShapeDtypeStruct((t, w), F32),
        grid=(t // tm,),
        in_specs=[pl.BlockSpec((tm, w), lambda i: (i, 0)),
                  pl.BlockSpec((tm, w), lambda i: (i, 0)),
                  pl.BlockSpec((1, w), lambda i: (0, 0)),
                  pl.BlockSpec((w, w), lambda i: (0, 0)),
                  pl.BlockSpec((1, w), lambda i: (0, 0))],
        out_specs=pl.BlockSpec((tm, w), lambda i: (i, 0)),
        compiler_params=_cparams(("parallel",)),
        name="s5_post",
    )(y_ssm, proj, d_skip.reshape(1, w), w_glu_bf16, b_glu.reshape(1, w))


def _s5_mixer(proj, bsz, seqlen, a_re, a_im, log_dt, b_re, b_im, c_re, c_im, d_skip, w_glu, b_glu):
    mats = _s5_prep(a_re, a_im, log_dt, b_re, b_im, c_re, c_im)
    y = _s5_scan(proj.reshape(bsz, seqlen, proj.shape[1]), mats)
    return _s5_post(y.reshape(bsz * seqlen, MIX_WIDTH), proj, d_skip, w_glu.astype(BF16), b_glu)


def _head_norm_padded(hv, g):
    lane = lax.broadcasted_iota(jnp.int32, hv.shape, 1)
    real = lane < DV
    mu = jnp.sum(hv, axis=-1, keepdims=True) * (1.0 / DV)
    d = jnp.where(real, hv - mu, 0.0)
    var = jnp.sum(d * d, axis=-1, keepdims=True) * (1.0 / DV)
    return d * lax.rsqrt(var + LN_EPS) * g


def _log_sigmoid(x):
    return jnp.minimum(x, 0.0) - jnp.log(1.0 + jnp.exp(-jnp.abs(x)))


def _mlstm_kernel(q_ref, k_ref, v_ref, o_ref, gt_ref, cw_ref, gb_ref, ng_ref,
                  y_ref, cbuf, c_st, n_st, m_st, *, tl):
    i = pl.program_id(1)

    @pl.when(i == 0)
    def _():
        cbuf[0:SUBLANES, :] = jnp.zeros((SUBLANES, 2 * HEADS_QK), F32)
        c_st[...] = jnp.zeros_like(c_st)
        n_st[...] = jnp.zeros_like(n_st)
        m_st[...] = jnp.zeros_like(m_st)

    cbuf[SUBLANES:SUBLANES + tl, 0:HEADS_QK] = q_ref[...]
    cbuf[SUBLANES:SUBLANES + tl, HEADS_QK:2 * HEADS_QK] = k_ref[...]
    acc = jnp.zeros((tl, 2 * HEADS_QK), F32)
    for w in range(ML_CONV):
        acc = acc + cbuf[pl.ds(SUBLANES - (ML_CONV - 1) + w, tl), :] * cw_ref[w:w + 1, :]
    qk = acc * jax.nn.sigmoid(acc)
    cbuf[0:SUBLANES, :] = cbuf[tl:tl + SUBLANES, :]

    gt = gt_ref[...] + gb_ref[...]
    lf = _log_sigmoid(gt)
    gt_t = gt.T
    lf_t = _log_sigmoid(gt_t)

    cl = ML_CHUNK
    row = lax.broadcasted_iota(jnp.int32, (cl, cl), 0)
    col = lax.broadcasted_iota(jnp.int32, (cl, cl), 1)
    tri = row >= col
    scale = DQK ** -0.5
    for cc in range(tl // cl):
        r0 = cc * cl
        for h in range(N_HEADS):
            ig_col = gt[r0:r0 + cl, h:h + 1]
            lf_col = lf[r0:r0 + cl, N_HEADS + h:N_HEADS + h + 1]
            ig_row = gt_t[h:h + 1, r0:r0 + cl]
            lf_row = lf_t[N_HEADS + h:N_HEADS + h + 1, r0:r0 + cl]
            bcum_col = jnp.sum(jnp.where(tri, lf_row, 0.0), axis=1, keepdims=True)
            bcum_row = jnp.sum(jnp.where(col >= row, lf_col, 0.0), axis=0, keepdims=True)
            btot = jnp.sum(lf_row, axis=1, keepdims=True)
            w_row = btot - bcum_row + ig_row
            m_loc = jnp.max(w_row, axis=1, keepdims=True)
            e_col = jnp.exp(btot - bcum_col + ig_col - m_loc)
            m_prev = m_st[h:h + 1, 0:1]
            c_prev = c_st[h]
            n_prev = n_st[h:h + 1, :]
            q = qk[r0:r0 + cl, h * DQK_PAD:(h + 1) * DQK_PAD] * scale
            k = qk[r0:r0 + cl, HEADS_QK + h * DQK_PAD:HEADS_QK + (h + 1) * DQK_PAD]
            v = v_ref[r0:r0 + cl, h * DV_PAD:(h + 1) * DV_PAD]
            qb = q.astype(BF16)
            kb = k.astype(BF16)
            vb = v.astype(BF16)
            dmat = jnp.where(tri, bcum_col - bcum_row + ig_row, -jnp.inf)
            g_col = bcum_col + m_prev
            m_row = jnp.maximum(g_col, jnp.max(dmat, axis=1, keepdims=True))
            inter = jnp.exp(g_col - m_row)
            s_qk = lax.dot_general(qb, kb, _NT, preferred_element_type=F32) * jnp.exp(dmat - m_row)
            num = inter * _dot(qb, c_prev.astype(BF16)) + _dot(s_qk.astype(BF16), vb)
            den = (inter * jnp.sum(q * n_prev, axis=1, keepdims=True)
                   + jnp.sum(s_qk, axis=1, keepdims=True))
            hv = num / jnp.maximum(jnp.abs(den), jnp.exp(-m_row))
            ke = k * e_col
            kv = lax.dot_general(ke.astype(BF16), vb, _TN, preferred_element_type=F32)
            nk = jnp.sum(ke, axis=0, keepdims=True)
            m_new = jnp.maximum(btot + m_prev, m_loc)
            sa = jnp.exp(btot + m_prev - m_new)
            sb = jnp.exp(m_loc - m_new)
            c_st[h] = sa * c_prev + sb * kv
            n_st[h:h + 1, :] = sa * n_prev + sb * nk
            m_st[h:h + 1, :] = jnp.broadcast_to(m_new, (1, LANES))
            hn = _head_norm_padded(hv, ng_ref[0:1, h * DV_PAD:(h + 1) * DV_PAD])
            og = o_ref[r0:r0 + cl, h * DV_PAD:(h + 1) * DV_PAD]
            y_ref[r0:r0 + cl, h * DV_PAD:(h + 1) * DV_PAD] = jax.nn.sigmoid(og) * hn


def _mlstm_mixer(proj, bsz, seqlen, cw, gbias, norm_g, tl=ROW_TILE):
    t = proj.shape[0]
    nl = seqlen // tl
    rows = lambda b, i: b * nl + i
    return pl.pallas_call(
        functools.partial(_mlstm_kernel, tl=tl),
        out_shape=jax.ShapeDtypeStruct((t, HEADS_V), F32),
        grid=(bsz, nl),
        in_specs=[pl.BlockSpec((tl, HEADS_QK), lambda b, i: (rows(b, i), 0)),
                  pl.BlockSpec((tl, HEADS_QK), lambda b, i: (rows(b, i), 1)),
                  pl.BlockSpec((tl, HEADS_V), lambda b, i: (rows(b, i), 1)),
                  pl.BlockSpec((tl, HEADS_V), lambda b, i: (rows(b, i), 2)),
                  pl.BlockSpec((tl, LANES), lambda b, i: (rows(b, i), 26)),
                  pl.BlockSpec((ML_CONV, 2 * HEADS_QK), lambda b, i: (0, 0)),
                  pl.BlockSpec((1, LANES), lambda b, i: (0, 0)),
                  pl.BlockSpec((1, HEADS_V), lambda b, i: (0, 0))],
        out_specs=pl.BlockSpec((tl, HEADS_V), lambda b, i: (rows(b, i), 0)),
        scratch_shapes=[pltpu.VMEM((tl + SUBLANES, 2 * HEADS_QK), F32),
                        pltpu.VMEM((N_HEADS, DQK_PAD, DV_PAD), F32),
                        pltpu.VMEM((SUBLANES, DQK_PAD), F32),
                        pltpu.VMEM((SUBLANES, LANES), F32)],
        compiler_params=_cparams(("parallel", "arbitrary")),
        name="mlstm",
    )(proj, proj, proj, proj, proj, cw, gbias, norm_g)


def _ret_log_gamma(h):
    return float(np.log(np.float32(1.0) - np.power(np.float32(2.0), np.float32(-5.0 - h))))


def _ret_kernel(q_ref, k_ref, v_ref, g_ref, pos_ref, inv_ref, sgn_ref, ng_ref,
                y_ref, s_st, *, tl):
    i = pl.program_id(1)

    @pl.when(i == 0)
    def _():
        s_st[...] = jnp.zeros_like(s_st)

    ang = pos_ref[...] * inv_ref[...]
    cos_t = jnp.cos(ang)
    sin_t = jnp.sin(ang) * sgn_ref[...]
    cl = RET_CHUNK
    row = lax.broadcasted_iota(jnp.int32, (cl, cl), 0)
    col = lax.broadcasted_iota(jnp.int32, (cl, cl), 1)
    rel = (row - col).astype(F32)
    jcol = lax.broadcasted_iota(jnp.int32, (cl, 1), 0).astype(F32)
    kscale = DQK ** -0.5
    for h in range(N_HEADS):
        lg = _ret_log_gamma(h)
        decay = jnp.where(rel >= 0, jnp.exp(jnp.maximum(rel, 0.0) * lg), 0.0)
        zeta = jnp.exp((cl - 1 - jcol) * lg)
        xi = jnp.exp((jcol + 1.0) * lg)
        chunk_decay = float(np.exp(np.float32(cl) * np.float32(lg)))
        qh = q_ref[:, h * DQK_PAD:(h + 1) * DQK_PAD]
        kh = k_ref[:, h * DQK_PAD:(h + 1) * DQK_PAD]
        qh = qh * cos_t + pltpu.roll(qh, DQK_PAD // 2, 1) * sin_t
        kh = (kh * cos_t + pltpu.roll(kh, DQK_PAD // 2, 1) * sin_t) * kscale
        for cc in range(tl // cl):
            r0 = cc * cl
            qb = qh[r0:r0 + cl].astype(BF16)
            k = kh[r0:r0 + cl]
            kb = k.astype(BF16)
            v = v_ref[r0:r0 + cl, h * DV_PAD:(h + 1) * DV_PAD]
            vb = v.astype(BF16)
            s_prev = s_st[h]
            s = lax.dot_general(qb, kb, _NT, preferred_element_type=F32) * decay
            intra = _dot(s.astype(BF16), vb)
            cross = _dot(qb, s_prev.astype(BF16)) * xi
            r = lax.dot_general((k * zeta).astype(BF16), vb, _TN, preferred_element_type=F32)
            s_st[h] = chunk_decay * s_prev + r
            hn = _head_norm_padded(intra + cross, ng_ref[0:1, h * DV_PAD:(h + 1) * DV_PAD])
            gate = g_ref[r0:r0 + cl, h * DV_PAD:(h + 1) * DV_PAD]
            y_ref[r0:r0 + cl, h * DV_PAD:(h + 1) * DV_PAD] = gate * jax.nn.sigmoid(gate) * hn


def _ret_mixer(proj, bsz, seqlen, pos_f, inv_pad, sgn_pad, norm_g, tl=ROW_TILE):
    t = proj.shape[0]
    nl = seqlen // tl
    rows = lambda b, i: b * nl + i
    return pl.pallas_call(
        functools.partial(_ret_kernel, tl=tl),
        out_shape=jax.ShapeDtypeStruct((t, HEADS_V), F32),
        grid=(bsz, nl),
        in_specs=[pl.BlockSpec((tl, HEADS_QK), lambda b, i: (rows(b, i), 0)),
                  pl.BlockSpec((tl, HEADS_QK), lambda b, i: (rows(b, i), 1)),
                  pl.BlockSpec((tl, HEADS_V), lambda b, i: (rows(b, i), 1)),
                  pl.BlockSpec((tl, HEADS_V), lambda b, i: (rows(b, i), 2)),
                  pl.BlockSpec((tl, 1), lambda b, i: (rows(b, i), 0)),
                  pl.BlockSpec((1, DQK_PAD), lambda b, i: (0, 0)),
                  pl.BlockSpec((1, DQK_PAD), lambda b, i: (0, 0)),
                  pl.BlockSpec((1, HEADS_V), lambda b, i: (0, 0))],
        out_specs=pl.BlockSpec((tl, HEADS_V), lambda b, i: (rows(b, i), 0)),
        scratch_shapes=[pltpu.VMEM((N_HEADS, DQK_PAD, DV_PAD), F32)],
        compiler_params=_cparams(("parallel", "arbitrary")),
        name="retention",
    )(proj, proj, proj, proj, pos_f, inv_pad, sgn_pad, norm_g)


def _post_kernel(ym_ref, xq_ref, h_ref, mk_ref, mv_ref, wom_ref, wox_ref, g_ref, b_ref,
                 rw_ref, rb_ref, h1_ref, idx_ref, gate_ref):
    tl = xq_ref.shape[0]
    xq = xq_ref[...] * (XATTN_HEAD_DIM ** -0.5)
    lane = lax.broadcasted_iota(jnp.int32, (tl, XATTN_WIDTH), 1)
    head = lane // XATTN_HEAD_DIM
    mk = mk_ref[0]
    mv = mv_ref[0]
    ymem = jnp.zeros((tl, XATTN_WIDTH), F32)
    for hh in range(XATTN_HEADS):
        sel = head == hh
        qh = jnp.where(sel, xq, 0.0).astype(BF16)
        s = lax.dot_general(qh, mk, _NT, preferred_element_type=F32)
        s = s - jnp.max(s, axis=-1, keepdims=True)
        p = jnp.exp(s)
        p = p / jnp.sum(p, axis=-1, keepdims=True)
        ymem = jnp.where(sel, _dot(p.astype(BF16), mv), ymem)
    y = _dot(ym_ref[...].astype(BF16), wom_ref[...]) + _dot(ymem.astype(BF16), wox_ref[...])
    h1 = _layer_norm_rows(DEEPNORM_ALPHA * h_ref[...] + y, g_ref[...], b_ref[...])
    h1_ref[...] = h1
    logits = jnp.dot(h1, rw_ref[...], preferred_element_type=F32,
                     precision=lax.Precision.HIGHEST) + rb_ref[...]
    ln = lax.broadcasted_iota(jnp.int32, logits.shape, 1)
    vals = logits
    tv, ti = [], []
    for _ in range(TOP_K):
        m = jnp.max(vals, axis=-1, keepdims=True)
        ix = jnp.min(jnp.where(vals == m, ln, LANES), axis=-1, keepdims=True)
        tv.append(m)
        ti.append(ix)
        vals = jnp.where(ln == ix, -jnp.inf, vals)
    ex = [jnp.exp(v - tv[0]) for v in tv]
    tot = ex[0] + ex[1] + ex[2] + ex[3]
    idx_out = jnp.zeros(logits.shape, jnp.int32)
    gate_out = jnp.zeros(logits.shape, F32)
    for k in range(TOP_K):
        idx_out = jnp.where(ln == k, ti[k], idx_out)
        gate_out = jnp.where(ln == k, ex[k] / tot, gate_out)
    idx_ref[...] = idx_out
    gate_ref[...] = gate_out


def _post_mixer(y_mix, proj, xq_blk, h, mem_k, mem_v, wo_mix, wo_mem, ln_g, ln_b, rw, rb,
                seqlen, tl=ROW_TILE):
    t, cm = y_mix.shape
    nl = seqlen // tl
    d = D_MODEL
    full = lambda a, b: pl.BlockSpec((a, b), lambda i: (0, 0))
    return pl.pallas_call(
        _post_kernel,
        out_shape=(jax.ShapeDtypeStruct((t, d), F32),
                   jax.ShapeDtypeStruct((t, LANES), jnp.int32),
                   jax.ShapeDtypeStruct((t, LANES), F32)),
        grid=(t // tl,),
        in_specs=[pl.BlockSpec((tl, cm), lambda i: (i, 0)),
                  pl.BlockSpec((tl, XATTN_WIDTH), lambda i: (i, xq_blk)),
                  pl.BlockSpec((tl, d), lambda i: (i, 0)),
                  pl.BlockSpec((1, N_MEM, XATTN_WIDTH), lambda i: (i // nl, 0, 0)),
                  pl.BlockSpec((1, N_MEM, XATTN_WIDTH), lambda i: (i // nl, 0, 0)),
                  full(cm, d), full(XATTN_WIDTH, d), full(1, d), full(1, d),
                  full(d, LANES), full(1, LANES)],
        out_specs=(pl.BlockSpec((tl, d), lambda i: (i, 0)),
                   pl.BlockSpec((tl, LANES), lambda i: (i, 0)),
                   pl.BlockSpec((tl, LANES), lambda i: (i, 0))),
        compiler_params=_cparams(("parallel",)),
        name="post_mixer",
    )(y_mix, proj, h, mem_k, mem_v, wo_mix, wo_mem, ln_g, ln_b, rw, rb)


SC_CORES = 2
SC_SUBCORES = 16
SC_WORKERS = SC_CORES * SC_SUBCORES
SC_CHUNK = 32


def _sc_gather(table, idx):
    v, d = table.shape
    b = idx.shape[0]
    per_w = b // SC_WORKERS
    n_chunks = per_w // SC_CHUNK
    assert per_w * SC_WORKERS == b and n_chunks * SC_CHUNK == per_w
    mesh = plsc.VectorSubcoreMesh(core_axis_name="c", subcore_axis_name="s")

    @functools.partial(
        pl.kernel, mesh=mesh,
        out_type=jax.ShapeDtypeStruct((b, d), F32),
        scratch_types=[pltpu.VMEM((SC_CHUNK,), jnp.int32),
                       pltpu.VMEM((SC_CHUNK, d), F32),
                       pltpu.SemaphoreType.DMA],
    )
    def gather_kernel(table_hbm, idx_hbm, out_hbm, idx_v, rows_v, sem):
        wid = lax.axis_index("s") * SC_CORES + lax.axis_index("c")
        base = wid * per_w

        @pl.loop(0, n_chunks)
        def _(j):
            off = pl.multiple_of(base + j * SC_CHUNK, 8)
            pltpu.sync_copy(idx_hbm.at[pl.ds(off, SC_CHUNK)], idx_v)
            pltpu.async_copy(table_hbm.at[idx_v], rows_v, sem).wait()
            pltpu.sync_copy(rows_v, out_hbm.at[pl.ds(off, SC_CHUNK)])

    return gather_kernel(table, idx)


def _expert_kernel(bexp_ref, nused_ref, x_ref, wgu_ref, bgu_ref, wd_ref, bd_ref, y_ref,
                   wgu_bf, wd_bf):
    i = pl.program_id(0)
    de = wd_ref.shape[2]

    @pl.when(i < nused_ref[0])
    def _():
        prev = bexp_ref[jnp.maximum(i - 1, 0)]

        @pl.when((i == 0) | (prev != bexp_ref[i]))
        def _():
            wgu_bf[...] = wgu_ref[0, 0].astype(BF16)
            wd_bf[...] = wd_ref[0, 0].astype(BF16)

        xb = x_ref[...].astype(BF16)
        gu = _dot(xb, wgu_bf[...]) + bgu_ref[0, 0]
        x_glu = jnp.minimum(gu[:, :de], SWIGLU_LIMIT)
        x_lin = jnp.clip(gu[:, de:], -SWIGLU_LIMIT, SWIGLU_LIMIT)
        act = x_glu * jax.nn.sigmoid(SWIGLU_ALPHA * x_glu) * (x_lin + 1.0)
        y_ref[...] = _dot(act.astype(BF16), wd_bf[...]) + bd_ref[0, 0]

    @pl.when(i >= nused_ref[0])
    def _():
        y_ref[...] = jnp.zeros_like(y_ref)


def _experts(xr, block_exp, n_used, layer, w_gu, b_gu, w_down, b_down, bm=MOE_BM):
    n_rows, d = xr.shape
    n_blocks = n_rows // bm
    nl, ne, _, de2 = w_gu.shape
    de = de2 // 2
    row_blk = lambda i, be, nu: jnp.minimum(i, nu[0] - 1)
    grid_spec = pltpu.PrefetchScalarGridSpec(
        num_scalar_prefetch=2,
        grid=(n_blocks,),
        in_specs=[pl.BlockSpec((bm, d), lambda i, be, nu: (row_blk(i, be, nu), 0)),
                  pl.BlockSpec((1, 1, d, de2), lambda i, be, nu: (layer, be[i], 0, 0)),
                  pl.BlockSpec((1, 1, 1, de2), lambda i, be, nu: (layer, be[i], 0, 0)),
                  pl.BlockSpec((1, 1, de, d), lambda i, be, nu: (layer, be[i], 0, 0)),
                  pl.BlockSpec((1, 1, 1, d), lambda i, be, nu: (layer, be[i], 0, 0))],
        out_specs=pl.BlockSpec((bm, d), lambda i, be, nu: (i, 0)),
        scratch_shapes=[pltpu.VMEM((d, de2), BF16),
                        pltpu.VMEM((de, d), BF16)],
    )
    return pl.pallas_call(
        _expert_kernel,
        out_shape=jax.ShapeDtypeStruct((n_rows, d), F32),
        grid_spec=grid_spec,
        compiler_params=_cparams(("arbitrary",)),
        name="experts",
    )(block_exp, n_used, xr, w_gu, b_gu.reshape(nl, ne, 1, de2), w_down,
      b_down.reshape(nl, ne, 1, d))


def _combine_kernel(y0_ref, y1_ref, y2_ref, y3_ref, h1_ref, gate_ref, g_ref, b_ref, o_ref):
    gate = gate_ref[...]
    acc = DEEPNORM_ALPHA * h1_ref[...]
    for k, y_ref in enumerate((y0_ref, y1_ref, y2_ref, y3_ref)):
        acc = acc + gate[:, k:k + 1] * y_ref[0]
    o_ref[...] = _layer_norm_rows(acc, g_ref[...], b_ref[...])


def _combine(yg, h1, gates, ln_g, ln_b, tl=ROW_TILE):
    t, d = h1.shape
    ysel = lambda k: pl.BlockSpec((1, tl, d), lambda i: (k, i, 0))
    return pl.pallas_call(
        _combine_kernel,
        out_shape=jax.ShapeDtypeStruct((t, d), F32),
        grid=(t // tl,),
        in_specs=[ysel(0), ysel(1), ysel(2), ysel(3),
                  pl.BlockSpec((tl, d), lambda i: (i, 0)),
                  pl.BlockSpec((tl, LANES), lambda i: (i, 0)),
                  pl.BlockSpec((1, d), lambda i: (0, 0)),
                  pl.BlockSpec((1, d), lambda i: (0, 0))],
        out_specs=pl.BlockSpec((tl, d), lambda i: (i, 0)),
        compiler_params=_cparams(("parallel",)),
        name="combine",
    )(yg, yg, yg, yg, h1, gates, ln_g, ln_b)


def _route(idx, bm=MOE_BM):
    t = idx.shape[0]
    n_assign = t * TOP_K
    n_rows = (-(-n_assign // bm) + N_EXPERTS) * bm
    n_blocks = n_rows // bm
    onehot = (idx[:, :, None] == jnp.arange(N_EXPERTS, dtype=jnp.int32)[None, None, :])
    sel = jnp.sum(onehot.astype(jnp.int32), axis=1)
    csum = jnp.cumsum(sel, axis=0)
    counts = csum[-1]
    rank = csum - sel
    padded = ((counts + bm - 1) // bm) * bm
    pad_end = jnp.cumsum(padded)
    pad_start = pad_end - padded
    pos = pad_start[idx] + jnp.take_along_axis(rank, idx, axis=1)
    tok = jnp.broadcast_to(jnp.arange(t, dtype=jnp.int32)[:, None], (t, TOP_K))
    row_tok = (jnp.arange(n_rows, dtype=jnp.int32) % t).at[pos.reshape(-1)].set(tok.reshape(-1))
    block_start = jnp.arange(n_blocks, dtype=jnp.int32) * bm
    block_exp = jnp.minimum(
        jnp.sum((block_start[:, None] >= pad_end[None, :]).astype(jnp.int32), axis=1),
        N_EXPERTS - 1).astype(jnp.int32)
    n_used = (pad_end[-1] // bm).astype(jnp.int32).reshape(1)
    return pos.T.reshape(-1).astype(jnp.int32), row_tok, block_exp, n_used


def _take_cols(w, cols):
    cols = np.asarray(cols, np.int32)
    out = jnp.take(w, jnp.asarray(np.maximum(cols, 0)), axis=-1)
    return jnp.where(jnp.asarray(cols >= 0), out, 0.0)


def _head_cols(offset, width, pad):
    cols = []
    for h in range(N_HEADS):
        cols += list(range(offset + h * width, offset + (h + 1) * width)) + [-1] * (pad - width)
    return cols


def _rope_head_cols(offset):
    half = DQK // 2
    slot = DQK_PAD // 2
    cols = []
    for h in range(N_HEADS):
        b = offset + h * DQK
        cols += list(range(b, b + half)) + [-1] * (slot - half)
        cols += list(range(b + half, b + DQK)) + [-1] * (slot - half)
    return cols


_ML_GATE_OFF = 2 * ML_QK + 2 * MIX_WIDTH
_ML_COLS = (_head_cols(0, DQK, DQK_PAD) + _head_cols(ML_QK, DQK, DQK_PAD)
            + _head_cols(2 * ML_QK, DV, DV_PAD) + _head_cols(2 * ML_QK + MIX_WIDTH, DV, DV_PAD)
            + list(range(_ML_GATE_OFF + 2 * N_HEADS, _ML_GATE_OFF + 2 * N_HEADS + XATTN_WIDTH))
            + list(range(_ML_GATE_OFF, _ML_GATE_OFF + 2 * N_HEADS)) + [-1] * (LANES - 2 * N_HEADS))
_RET_COLS = (_rope_head_cols(0) + _rope_head_cols(ML_QK)
             + _head_cols(2 * ML_QK, DV, DV_PAD) + _head_cols(2 * ML_QK + MIX_WIDTH, DV, DV_PAD)
             + list(range(2 * ML_QK + 2 * MIX_WIDTH, 2 * ML_QK + 2 * MIX_WIDTH + XATTN_WIDTH)))
_MIX_PAD_COLS = _head_cols(0, DV, DV_PAD)
_XQ_BLK_PADDED = (2 * HEADS_QK + 2 * HEADS_V) // XATTN_WIDTH
_XQ_BLK_S5 = MIX_WIDTH // XATTN_WIDTH


def kernel(x, mem, positions, mem_w_k, mem_w_v, l0_w_in, l0_s5_a_re, l0_s5_a_im, l0_s5_log_dt, l0_s5_b_re, l0_s5_b_im, l0_s5_c_re, l0_s5_c_im, l0_s5_d, l0_s5_w_glu, l0_s5_b_glu, l1_w_in, l1_ml_conv_q, l1_ml_conv_k, l1_ml_b_i, l1_ml_b_f, l1_ml_norm_g, l2_w_in, l2_ret_norm_g, l3_w_in, l3_s5_a_re, l3_s5_a_im, l3_s5_log_dt, l3_s5_b_re, l3_s5_b_im, l3_s5_c_re, l3_s5_c_im, l3_s5_d, l3_s5_w_glu, l3_s5_b_glu, w_out, ln1_g, ln1_b, ln2_g, ln2_b, router_w, router_b, exp_w_gu, exp_b_gu, exp_w_down, exp_b_down):
    bsz, seqlen, d = x.shape
    t = bsz * seqlen
    h = x.reshape(t, d)

    w_kv = jnp.concatenate([mem_w_k, mem_w_v], axis=1).astype(BF16)
    kv = _inproj(mem.reshape(bsz * N_MEM, d), w_kv).astype(BF16)
    mem_k = kv[:, :XATTN_WIDTH].reshape(bsz, N_MEM, XATTN_WIDTH)
    mem_v = kv[:, XATTN_WIDTH:].reshape(bsz, N_MEM, XATTN_WIDTH)

    half = DQK // 2
    inv = ROPE_BASE ** (-jnp.arange(0, DQK, 2, dtype=F32) / DQK)
    zpad = jnp.zeros((DQK_PAD // 2 - half,), F32)
    inv_pad = jnp.concatenate([inv, zpad, inv, zpad]).reshape(1, DQK_PAD)
    sgn_pad = jnp.concatenate([-jnp.ones((half,), F32), zpad, jnp.ones((half,), F32), zpad]
                              ).reshape(1, DQK_PAD)
    pos_f = positions.astype(F32).reshape(t, 1)

    s5_params = {
        0: (l0_s5_a_re, l0_s5_a_im, l0_s5_log_dt, l0_s5_b_re, l0_s5_b_im, l0_s5_c_re, l0_s5_c_im,
            l0_s5_d, l0_s5_w_glu, l0_s5_b_glu),
        3: (l3_s5_a_re, l3_s5_a_im, l3_s5_log_dt, l3_s5_b_re, l3_s5_b_im, l3_s5_c_re, l3_s5_c_im,
            l3_s5_d, l3_s5_w_glu, l3_s5_b_glu),
    }
    w_ins = (l0_w_in, l1_w_in, l2_w_in, l3_w_in)

    for i in range(DEPTH):
        kind = i % 3
        wo = w_out[i]
        if kind == 0:
            proj = _inproj(h, w_ins[i].astype(BF16))
            y_mix = _s5_mixer(proj, bsz, seqlen, *s5_params[i])
            wo_mix = wo[:MIX_WIDTH].astype(BF16)
            xq_blk = _XQ_BLK_S5
        elif kind == 1:
            proj = _inproj(h, _take_cols(w_ins[i], _ML_COLS).astype(BF16))
            cw = jnp.concatenate([_take_cols(l1_ml_conv_q, _head_cols(0, DQK, DQK_PAD)),
                                  _take_cols(l1_ml_conv_k, _head_cols(0, DQK, DQK_PAD))], axis=1)
            gbias = jnp.concatenate([l1_ml_b_i, l1_ml_b_f,
                                     jnp.zeros((LANES - 2 * N_HEADS,), F32)]).reshape(1, LANES)
            norm_g = _take_cols(l1_ml_norm_g, _MIX_PAD_COLS).reshape(1, HEADS_V)
            y_mix = _mlstm_mixer(proj, bsz, seqlen, cw, gbias, norm_g)
            wo_mix = _take_cols(wo[:MIX_WIDTH].T, _MIX_PAD_COLS).T.astype(BF16)
            xq_blk = _XQ_BLK_PADDED
        else:
            proj = _inproj(h, _take_cols(w_ins[i], _RET_COLS).astype(BF16))
            norm_g = _take_cols(l2_ret_norm_g, _MIX_PAD_COLS).reshape(1, HEADS_V)
            y_mix = _ret_mixer(proj, bsz, seqlen, pos_f, inv_pad, sgn_pad, norm_g)
            wo_mix = _take_cols(wo[:MIX_WIDTH].T, _MIX_PAD_COLS).T.astype(BF16)
            xq_blk = _XQ_BLK_PADDED
        wo_mem = wo[MIX_WIDTH:].astype(BF16)
        rw = jnp.pad(router_w[i], ((0, 0), (0, LANES - N_EXPERTS)))
        rb = jnp.concatenate([router_b[i], jnp.full((LANES - N_EXPERTS,), -1e30, F32)]
                             ).reshape(1, LANES)
        h1, idx, gates = _post_mixer(y_mix, proj, xq_blk, h, mem_k, mem_v, wo_mix, wo_mem,
                                     ln1_g[i].reshape(1, d), ln1_b[i].reshape(1, d), rw, rb, seqlen)
        pos_kmajor, row_tok, block_exp, n_used = _route(idx[:, :TOP_K])
        xr = _sc_gather(h1, row_tok)
        yr = _experts(xr, block_exp, n_used, i, exp_w_gu, exp_b_gu, exp_w_down, exp_b_down)
        yg = _sc_gather(yr, pos_kmajor).reshape(TOP_K, t, d)
        h = _combine(yg, h1, gates, ln2_g[i].reshape(1, d), ln2_b[i].reshape(1, d))
    return h.reshape(bsz, seqlen, d)
```

```python
import functools
import math

import numpy as np
import jax
import jax.numpy as jnp
from jax import lax
from jax.experimental import pallas as pl
from jax.experimental.pallas import tpu as pltpu
from jax.experimental.pallas import tpu_sc as plsc

F32 = jnp.float32
BF16 = jnp.bfloat16

D_MODEL = 1024
DEPTH = 4
N_MEM = 256
MIX_WIDTH = 768
XATTN_HEADS = 4
XATTN_WIDTH = 256
XATTN_HEAD_DIM = 64
S5_GROUP = 16
S5_GROUPS = 48
S5_STATE = 64
N_HEADS = 4
DQK = 96
DV = 192
ML_QK = 384
ML_CONV = 4
ML_CHUNK = 64
RET_CHUNK = 128
ROPE_BASE = 10000.0
N_EXPERTS = 32
TOP_K = 4
SWIGLU_LIMIT = 7.0
SWIGLU_ALPHA = 1.702
DEEPNORM_ALPHA = (2.0 * DEPTH) ** 0.25
LN_EPS = 1e-5

LANES = 128
SUBLANES = 8
DQK_PAD = 128
DV_PAD = 256
HEADS_QK = N_HEADS * DQK_PAD
HEADS_V = N_HEADS * DV_PAD
S5_LC = 16
S5_K = S5_LC * S5_GROUP
MOE_BM = 256
ROW_TILE = 256
VMEM_LIMIT = 56 * 1024 * 1024

_NT = (((1,), (1,)), ((), ()))
_TN = (((0,), (0,)), ((), ()))


def _cparams(sem):
    return pltpu.CompilerParams(dimension_semantics=sem, vmem_limit_bytes=VMEM_LIMIT)


def _dot(a, b):
    return jnp.dot(a, b, preferred_element_type=F32)


def _layer_norm_rows(z, g, b):
    mu = jnp.mean(z, axis=-1, keepdims=True)
    d = z - mu
    var = jnp.mean(d * d, axis=-1, keepdims=True)
    return d * lax.rsqrt(var + LN_EPS) * g + b


def _inproj_kernel(x_ref, w_ref, o_ref):
    xb = x_ref[...].astype(BF16)
    n = o_ref.shape[1]
    step = 512
    for c0 in range(0, n, step):
        c1 = min(c0 + step, n)
        o_ref[:, c0:c1] = _dot(xb, w_ref[:, c0:c1])


def _inproj(x, w_bf16, tm=ROW_TILE):
    t, d = x.shape
    n = w_bf16.shape[1]
    return pl.pallas_call(
        _inproj_kernel,
        out_shape=jax.ShapeDtypeStruct((t, n), F32),
        grid=(t // tm,),
        in_specs=[pl.BlockSpec((tm, d), lambda i: (i, 0)),
                  pl.BlockSpec((d, n), lambda i: (0, 0))],
        out_specs=pl.BlockSpec((tm, n), lambda i: (i, 0)),
        compiler_params=_cparams(("parallel",)),
        name="inproj",
    )(x, w_bf16)


S5_LANE_GROUPS = S5_K // S5_GROUP
S5_BLOCKS = MIX_WIDTH // S5_K
S5_SW = S5_LANE_GROUPS * S5_STATE
S5_LB = 512


def _block_diag(m):
    nb, g, a, b = m.shape
    eye = jnp.eye(g, dtype=m.dtype)
    return (m[:, :, :, None, :] * eye[None, :, None, :, None]).reshape(nb, g * a, g * b)


def _s5_prep(a_re, a_im, log_dt, b_re, b_im, c_re, c_im):
    hp = lax.Precision.HIGHEST
    lam_re = jnp.minimum(a_re.astype(F32), -1e-4)
    lam_im = a_im.astype(F32)
    dt = jnp.exp(log_dt.astype(F32))[:, None]
    mag = jnp.exp(dt * lam_re)
    ab_re = mag * jnp.cos(dt * lam_im)
    ab_im = mag * jnp.sin(dt * lam_im)
    den = lam_re * lam_re + lam_im * lam_im
    num_re = ab_re - 1.0
    coef_re = (num_re * lam_re + ab_im * lam_im) / den
    coef_im = (ab_im * lam_re - num_re * lam_im) / den
    bre = b_re.astype(F32)
    bim = b_im.astype(F32)
    bb_re = coef_re[..., None] * bre - coef_im[..., None] * bim
    bb_im = coef_re[..., None] * bim + coef_im[..., None] * bre
    pr = [jnp.ones_like(ab_re)]
    pi = [jnp.zeros_like(ab_im)]
    for _ in range(S5_LC):
        r, i = pr[-1], pi[-1]
        pr.append(r * ab_re - i * ab_im)
        pi.append(r * ab_im + i * ab_re)
    pw_re = jnp.stack(pr)
    pw_im = jnp.stack(pi)
    p_re = pw_re[:S5_LC, :, :, None] * bb_re[None] - pw_im[:S5_LC, :, :, None] * bb_im[None]
    p_im = pw_re[:S5_LC, :, :, None] * bb_im[None] + pw_im[:S5_LC, :, :, None] * bb_re[None]
    cre = c_re.astype(F32)
    cim = c_im.astype(F32)
    kmat = (jnp.einsum('ghp,tgpk->tgkh', cre, p_re, precision=hp)
            - jnp.einsum('ghp,tgpk->tgkh', cim, p_im, precision=hp))
    nb, lg = S5_BLOCKS, S5_LANE_GROUPS
    toep = _block_diag(kmat.reshape(S5_LC * nb, lg, S5_GROUP, S5_GROUP)
                       ).reshape(S5_LC, nb, S5_K, S5_K).transpose(1, 0, 2, 3)
    bmat = jnp.concatenate(
        [_block_diag(bb_re.transpose(0, 2, 1).reshape(nb, lg, S5_GROUP, S5_STATE)),
         _block_diag(bb_im.transpose(0, 2, 1).reshape(nb, lg, S5_GROUP, S5_STATE))], axis=2)
    cmat = jnp.concatenate(
        [_block_diag(cre.transpose(0, 2, 1).reshape(nb, lg, S5_STATE, S5_GROUP)),
         _block_diag(-cim.transpose(0, 2, 1).reshape(nb, lg, S5_STATE, S5_GROUP))], axis=1)
    lane = lambda v: v.reshape(nb, 1, S5_SW)
    avec = jnp.concatenate([lane(ab_re), lane(ab_im), lane(pw_re[S5_LC]), lane(pw_im[S5_LC])],
                           axis=1)
    return toep.astype(BF16), bmat.astype(BF16), cmat.astype(BF16), avec


def _s5_kernel(u_ref, t_ref, b_ref, c_ref, a_ref, y_ref, s_acc, x_prev, x_carry, u_half, y_half,
               *, lb):
    li = pl.program_id(1)
    bsz = u_ref.shape[0]
    ncb = lb // S5_LC
    m = bsz * ncb

    @pl.when(li == 0)
    def _():
        x_carry[...] = jnp.zeros_like(x_carry)

    ar = a_ref[0, 0:1, :]
    ai = a_ref[0, 1:2, :]
    alr = a_ref[0, 2:3, :]
    ali = a_ref[0, 3:4, :]

    def cmul(zr, zi, wr, wi):
        return wr * zr - wi * zi, wr * zi + wi * zr

    n_half = S5_K // LANES
    for hf in range(n_half):
        u_half[hf] = u_ref[:, :, hf * LANES:(hf + 1) * LANES].reshape(bsz * lb, LANES)
    xs = []
    for s in range(S5_LC):
        halves = [jnp.concatenate([u_half[hf, pl.ds(c * S5_LC + s, bsz, stride=lb), :]
                                   for c in range(ncb)], axis=0) for hf in range(n_half)]
        xs.append(jnp.concatenate(halves, axis=1).astype(BF16))
    bmat = b_ref[0]
    sr = jnp.zeros((m, S5_SW), F32)
    si = jnp.zeros((m, S5_SW), F32)
    for s in range(S5_LC):
        bu = _dot(xs[s], bmat)
        sr, si = cmul(sr, si, ar, ai)
        sr = sr + bu[:, :S5_SW]
        si = si + bu[:, S5_SW:]
    s_acc[:, :S5_SW] = sr
    s_acc[:, S5_SW:] = si

    xr = x_carry[:, :S5_SW]
    xi = x_carry[:, S5_SW:]
    for c in range(ncb):
        x_prev[c * bsz:(c + 1) * bsz, :S5_SW] = xr
        x_prev[c * bsz:(c + 1) * bsz, S5_SW:] = xi
        loc = s_acc[c * bsz:(c + 1) * bsz, :]
        xr, xi = cmul(xr, xi, alr, ali)
        xr = xr + loc[:, :S5_SW]
        xi = xi + loc[:, S5_SW:]
    x_carry[:, :S5_SW] = xr
    x_carry[:, S5_SW:] = xi

    cmat = c_ref[0]
    zr = x_prev[:, :S5_SW]
    zi = x_prev[:, S5_SW:]
    for j in range(S5_LC):
        zr, zi = cmul(zr, zi, ar, ai)
        yj = _dot(jnp.concatenate([zr, zi], axis=1).astype(BF16), cmat)
        for s in range(j + 1):
            yj = yj + _dot(xs[s], t_ref[0, j - s])
        for hf in range(n_half):
            for c in range(ncb):
                y_half[hf, pl.ds(c * S5_LC + j, bsz, stride=lb), :] = (
                    yj[c * bsz:(c + 1) * bsz, hf * LANES:(hf + 1) * LANES])
    for hf in range(n_half):
        y_ref[:, :, hf * LANES:(hf + 1) * LANES] = y_half[hf].reshape(bsz, lb, LANES)


def _s5_scan(proj3, mats, lb=S5_LB):
    toep, bmat, cmat, avec = mats
    bsz, seqlen, _ = proj3.shape
    return pl.pallas_call(
        functools.partial(_s5_kernel, lb=lb),
        out_shape=jax.ShapeDtypeStruct((bsz, seqlen, MIX_WIDTH), F32),
        grid=(S5_BLOCKS, seqlen // lb),
        in_specs=[pl.BlockSpec((bsz, lb, S5_K), lambda v, l: (0, l, v)),
                  pl.BlockSpec((1, S5_LC, S5_K, S5_K), lambda v, l: (v, 0, 0, 0)),
                  pl.BlockSpec((1, S5_K, 2 * S5_SW), lambda v, l: (v, 0, 0)),
                  pl.BlockSpec((1, 2 * S5_SW, S5_K), lambda v, l: (v, 0, 0)),
                  pl.BlockSpec((1, 4, S5_SW), lambda v, l: (v, 0, 0))],
        out_specs=pl.BlockSpec((bsz, lb, S5_K), lambda v, l: (0, l, v)),
        scratch_shapes=[pltpu.VMEM((bsz * (lb // S5_LC), 2 * S5_SW), F32),
                        pltpu.VMEM((bsz * (lb // S5_LC), 2 * S5_SW), F32),
                        pltpu.VMEM((bsz, 2 * S5_SW), F32),
                        pltpu.VMEM((S5_K // LANES, bsz * lb, LANES), F32),
                        pltpu.VMEM((S5_K // LANES, bsz * lb, LANES), F32)],
        compiler_params=_cparams(("parallel", "arbitrary")),
        name="s5_scan",
    )(proj3, toep, bmat, cmat, avec)


def _s5_post_kernel(y_ref, u_ref, d_ref, w_ref, b_ref, o_ref):
    y = y_ref[...] + d_ref[...] * u_ref[...]
    y = jax.nn.gelu(y)
    o_ref[...] = y * jax.nn.sigmoid(_dot(y.astype(BF16), w_ref[...]) + b_ref[...])


def _s5_post(y_ssm, proj, d_skip, w_glu_bf16, b_glu, tm=ROW_TILE):
    t = y_ssm.shape[0]
    w = MIX_WIDTH
    return pl.pallas_call(
        _s5_post_kernel,
        out_shape=jax.ShapeDtypeStruct((t, w), F32),
        grid=(t // tm,),
        in_specs=[pl.BlockSpec((tm, w), lambda i: (i, 0)),
                  pl.BlockSpec((tm, w), lambda i: (i, 0)),
                  pl.BlockSpec((1, w), lambda i: (0, 0)),
                  pl.BlockSpec((w, w), lambda i: (0, 0)),
                  pl.BlockSpec((1, w), lambda i: (0, 0))],
        out_specs=pl.BlockSpec((tm, w), lambda i: (i, 0)),
        compiler_params=_cparams(("parallel",)),
        name="s5_post",
    )(y_ssm, proj, d_skip.reshape(1, w), w_glu_bf16, b_glu.reshape(1, w))


def _s5_mixer(proj, bsz, seqlen, a_re, a_im, log_dt, b_re, b_im, c_re, c_im, d_skip, w_glu, b_glu):
    mats = _s5_prep(a_re, a_im, log_dt, b_re, b_im, c_re, c_im)
    y = _s5_scan(proj.reshape(bsz, seqlen, proj.shape[1]), mats)
    return _s5_post(y.reshape(bsz * seqlen, MIX_WIDTH), proj, d_skip, w_glu.astype(BF16), b_glu)


def _head_norm_padded(hv, g):
    lane = lax.broadcasted_iota(jnp.int32, hv.shape, 1)
    real = lane < DV
    mu = jnp.sum(hv, axis=-1, keepdims=True) * (1.0 / DV)
    d = jnp.where(real, hv - mu, 0.0)
    var = jnp.sum(d * d, axis=-1, keepdims=True) * (1.0 / DV)
    return d * lax.rsqrt(var + LN_EPS) * g


def _log_sigmoid(x):
    return jnp.minimum(x, 0.0) - jnp.log(1.0 + jnp.exp(-jnp.abs(x)))


def _mlstm_kernel(q_ref, k_ref, v_ref, o_ref, gt_ref, cw_ref, gb_ref, ng_ref,
                  y_ref, cbuf, c_st, n_st, m_st, *, tl):
    i = pl.program_id(1)

    @pl.when(i == 0)
    def _():
        cbuf[0:SUBLANES, :] = jnp.zeros((SUBLANES, 2 * HEADS_QK), F32)
        c_st[...] = jnp.zeros_like(c_st)
        n_st[...] = jnp.zeros_like(n_st)
        m_st[...] = jnp.zeros_like(m_st)

    cbuf[SUBLANES:SUBLANES + tl, 0:HEADS_QK] = q_ref[...]
    cbuf[SUBLANES:SUBLANES + tl, HEADS_QK:2 * HEADS_QK] = k_ref[...]
    acc = jnp.zeros((tl, 2 * HEADS_QK), F32)
    for w in range(ML_CONV):
        acc = acc + cbuf[pl.ds(SUBLANES - (ML_CONV - 1) + w, tl), :] * cw_ref[w:w + 1, :]
    qk = acc * jax.nn.sigmoid(acc)
    cbuf[0:SUBLANES, :] = cbuf[tl:tl + SUBLANES, :]

    gt = gt_ref[...] + gb_ref[...]
    lf = _log_sigmoid(gt)
    gt_t = gt.T
    lf_t = _log_sigmoid(gt_t)

    cl = ML_CHUNK
    row = lax.broadcasted_iota(jnp.int32, (cl, cl), 0)
    col = lax.broadcasted_iota(jnp.int32, (cl, cl), 1)
    tri = row >= col
    scale = DQK ** -0.5
    for cc in range(tl // cl):
        r0 = cc * cl
        for h in range(N_HEADS):
            ig_col = gt[r0:r0 + cl, h:h + 1]
            lf_col = lf[r0:r0 + cl, N_HEADS + h:N_HEADS + h + 1]
            ig_row = gt_t[h:h + 1, r0:r0 + cl]
            lf_row = lf_t[N_HEADS + h:N_HEADS + h + 1, r0:r0 + cl]
            bcum_col = jnp.sum(jnp.where(tri, lf_row, 0.0), axis=1, keepdims=True)
            bcum_row = jnp.sum(jnp.where(col >= row, lf_col, 0.0), axis=0, keepdims=True)
            btot = jnp.sum(lf_row, axis=1, keepdims=True)
            w_row = btot - bcum_row + ig_row
            m_loc = jnp.max(w_row, axis=1, keepdims=True)
            e_col = jnp.exp(btot - bcum_col + ig_col - m_loc)
            m_prev = m_st[h:h + 1, 0:1]
            c_prev = c_st[h]
            n_prev = n_st[h:h + 1, :]
            q = qk[r0:r0 + cl, h * DQK_PAD:(h + 1) * DQK_PAD] * scale
            k = qk[r0:r0 + cl, HEADS_QK + h * DQK_PAD:HEADS_QK + (h + 1) * DQK_PAD]
            v = v_ref[r0:r0 + cl, h * DV_PAD:(h + 1) * DV_PAD]
            qb = q.astype(BF16)
            kb = k.astype(BF16)
            vb = v.astype(BF16)
            dmat = jnp.where(tri, bcum_col - bcum_row + ig_row, -jnp.inf)
            g_col = bcum_col + m_prev
            m_row = jnp.maximum(g_col, jnp.max(dmat, axis=1, keepdims=True))
            inter = jnp.exp(g_col - m_row)
            s_qk = lax.dot_general(qb, kb, _NT, preferred_element_type=F32) * jnp.exp(dmat - m_row)
            num = inter * _dot(qb, c_prev.astype(BF16)) + _dot(s_qk.astype(BF16), vb)
            den = (inter * jnp.sum(q * n_prev, axis=1, keepdims=True)
                   + jnp.sum(s_qk, axis=1, keepdims=True))
            hv = num / jnp.maximum(jnp.abs(den), jnp.exp(-m_row))
            ke = k * e_col
            kv = lax.dot_general(ke.astype(BF16), vb, _TN, preferred_element_type=F32)
            nk = jnp.sum(ke, axis=0, keepdims=True)
            m_new = jnp.maximum(btot + m_prev, m_loc)
            sa = jnp.exp(btot + m_prev - m_new)
            sb = jnp.exp(m_loc - m_new)
            c_st[h] = sa * c_prev + sb * kv
            n_st[h:h + 1, :] = sa * n_prev + sb * nk
            m_st[h:h + 1, :] = jnp.broadcast_to(m_new, (1, LANES))
            hn = _head_norm_padded(hv, ng_ref[0:1, h * DV_PAD:(h + 1) * DV_PAD])
            og = o_ref[r0:r0 + cl, h * DV_PAD:(h + 1) * DV_PAD]
            y_ref[r0:r0 + cl, h * DV_PAD:(h + 1) * DV_PAD] = jax.nn.sigmoid(og) * hn


def _mlstm_mixer(proj, bsz, seqlen, cw, gbias, norm_g, tl=ROW_TILE):
    t = proj.shape[0]
    nl = seqlen // tl
    rows = lambda b, i: b * nl + i
    return pl.pallas_call(
        functools.partial(_mlstm_kernel, tl=tl),
        out_shape=jax.ShapeDtypeStruct((t, HEADS_V), F32),
        grid=(bsz, nl),
        in_specs=[pl.BlockSpec((tl, HEADS_QK), lambda b, i: (rows(b, i), 0)),
                  pl.BlockSpec((tl, HEADS_QK), lambda b, i: (rows(b, i), 1)),
                  pl.BlockSpec((tl, HEADS_V), lambda b, i: (rows(b, i), 1)),
                  pl.BlockSpec((tl, HEADS_V), lambda b, i: (rows(b, i), 2)),
                  pl.BlockSpec((tl, LANES), lambda b, i: (rows(b, i), 26)),
                  pl.BlockSpec((ML_CONV, 2 * HEADS_QK), lambda b, i: (0, 0)),
                  pl.BlockSpec((1, LANES), lambda b, i: (0, 0)),
                  pl.BlockSpec((1, HEADS_V), lambda b, i: (0, 0))],
        out_specs=pl.BlockSpec((tl, HEADS_V), lambda b, i: (rows(b, i), 0)),
        scratch_shapes=[pltpu.VMEM((tl + SUBLANES, 2 * HEADS_QK), F32),
                        pltpu.VMEM((N_HEADS, DQK_PAD, DV_PAD), F32),
                        pltpu.VMEM((SUBLANES, DQK_PAD), F32),
                        pltpu.VMEM((SUBLANES, LANES), F32)],
        compiler_params=_cparams(("parallel", "arbitrary")),
        name="mlstm",
    )(proj, proj, proj, proj, proj, cw, gbias, norm_g)


def _ret_log_gamma(h):
    return float(np.log(np.float32(1.0) - np.power(np.float32(2.0), np.float32(-5.0 - h))))


def _ret_kernel(q_ref, k_ref, v_ref, g_ref, pos_ref, inv_ref, sgn_ref, ng_ref,
                y_ref, s_st, *, tl):
    i = pl.program_id(1)

    @pl.when(i == 0)
    def _():
        s_st[...] = jnp.zeros_like(s_st)

    ang = pos_ref[...] * inv_ref[...]
    cos_t = jnp.cos(ang)
    sin_t = jnp.sin(ang) * sgn_ref[...]
    cl = RET_CHUNK
    row = lax.broadcasted_iota(jnp.int32, (cl, cl), 0)
    col = lax.broadcasted_iota(jnp.int32, (cl, cl), 1)
    rel = (row - col).astype(F32)
    jcol = lax.broadcasted_iota(jnp.int32, (cl, 1), 0).astype(F32)
    kscale = DQK ** -0.5
    for h in range(N_HEADS):
        lg = _ret_log_gamma(h)
        decay = jnp.where(rel >= 0, jnp.exp(jnp.maximum(rel, 0.0) * lg), 0.0)
        zeta = jnp.exp((cl - 1 - jcol) * lg)
        xi = jnp.exp((jcol + 1.0) * lg)
        chunk_decay = float(np.exp(np.float32(cl) * np.float32(lg)))
        qh = q_ref[:, h * DQK_PAD:(h + 1) * DQK_PAD]
        kh = k_ref[:, h * DQK_PAD:(h + 1) * DQK_PAD]
        qh = qh * cos_t + pltpu.roll(qh, DQK_PAD // 2, 1) * sin_t
        kh = (kh * cos_t + pltpu.roll(kh, DQK_PAD // 2, 1) * sin_t) * kscale
        for cc in range(tl // cl):
            r0 = cc * cl
            qb = qh[r0:r0 + cl].astype(BF16)
            k = kh[r0:r0 + cl]
            kb = k.astype(BF16)
            v = v_ref[r0:r0 + cl, h * DV_PAD:(h + 1) * DV_PAD]
            vb = v.astype(BF16)
            s_prev = s_st[h]
            s = lax.dot_general(qb, kb, _NT, preferred_element_type=F32) * decay
            intra = _dot(s.astype(BF16), vb)
            cross = _dot(qb, s_prev.astype(BF16)) * xi
            r = lax.dot_general((k * zeta).astype(BF16), vb, _TN, preferred_element_type=F32)
            s_st[h] = chunk_decay * s_prev + r
            hn = _head_norm_padded(intra + cross, ng_ref[0:1, h * DV_PAD:(h + 1) * DV_PAD])
            gate = g_ref[r0:r0 + cl, h * DV_PAD:(h + 1) * DV_PAD]
            y_ref[r0:r0 + cl, h * DV_PAD:(h + 1) * DV_PAD] = gate * jax.nn.sigmoid(gate) * hn


def _ret_mixer(proj, bsz, seqlen, pos_f, inv_pad, sgn_pad, norm_g, tl=ROW_TILE):
    t = proj.shape[0]
    nl = seqlen // tl
    rows = lambda b, i: b * nl + i
    return pl.pallas_call(
        functools.partial(_ret_kernel, tl=tl),
        out_shape=jax.ShapeDtypeStruct((t, HEADS_V), F32),
        grid=(bsz, nl),
        in_specs=[pl.BlockSpec((tl, HEADS_QK), lambda b, i: (rows(b, i), 0)),
                  pl.BlockSpec((tl, HEADS_QK), lambda b, i: (rows(b, i), 1)),
                  pl.BlockSpec((tl, HEADS_V), lambda b, i: (rows(b, i), 1)),
                  pl.BlockSpec((tl, HEADS_V), lambda b, i: (rows(b, i), 2)),
                  pl.BlockSpec((tl, 1), lambda b, i: (rows(b, i), 0)),
                  pl.BlockSpec((1, DQK_PAD), lambda b, i: (0, 0)),
                  pl.BlockSpec((1, DQK_PAD), lambda b, i: (0, 0)),
                  pl.BlockSpec((1, HEADS_V), lambda b, i: (0, 0))],
        out_specs=pl.BlockSpec((tl, HEADS_V), lambda b, i: (rows(b, i), 0)),
        scratch_shapes=[pltpu.VMEM((N_HEADS, DQK_PAD, DV_PAD), F32)],
        compiler_params=_cparams(("parallel", "arbitrary")),
        name="retention",
    )(proj, proj, proj, proj, pos_f, inv_pad, sgn_pad, norm_g)


def _post_kernel(ym_ref, xq_ref, h_ref, mk_ref, mv_ref, wom_ref, wox_ref, g_ref, b_ref,
                 rw_ref, rb_ref, h1_ref, idx_ref, gate_ref):
    tl = xq_ref.shape[0]
    xq = xq_ref[...] * (XATTN_HEAD_DIM ** -0.5)
    lane = lax.broadcasted_iota(jnp.int32, (tl, XATTN_WIDTH), 1)
    head = lane // XATTN_HEAD_DIM
    mk = mk_ref[0]
    mv = mv_ref[0]
    ymem = jnp.zeros((tl, XATTN_WIDTH), F32)
    for hh in range(XATTN_HEADS):
        sel = head == hh
        qh = jnp.where(sel, xq, 0.0).astype(BF16)
        s = lax.dot_general(qh, mk, _NT, preferred_element_type=F32)
        s = s - jnp.max(s, axis=-1, keepdims=True)
        p = jnp.exp(s)
        p = p / jnp.sum(p, axis=-1, keepdims=True)
        ymem = jnp.where(sel, _dot(p.astype(BF16), mv), ymem)
    y = _dot(ym_ref[...].astype(BF16), wom_ref[...]) + _dot(ymem.astype(BF16), wox_ref[...])
    h1 = _layer_norm_rows(DEEPNORM_ALPHA * h_ref[...] + y, g_ref[...], b_ref[...])
    h1_ref[...] = h1
    logits = jnp.dot(h1, rw_ref[...], preferred_element_type=F32,
                     precision=lax.Precision.HIGHEST) + rb_ref[...]
    ln = lax.broadcasted_iota(jnp.int32, logits.shape, 1)
    vals = logits
    tv, ti = [], []
    for _ in range(TOP_K):
        m = jnp.max(vals, axis=-1, keepdims=True)
        ix = jnp.min(jnp.where(vals == m, ln, LANES), axis=-1, keepdims=True)
        tv.append(m)
        ti.append(ix)
        vals = jnp.where(ln == ix, -jnp.inf, vals)
    ex = [jnp.exp(v - tv[0]) for v in tv]
    tot = ex[0] + ex[1] + ex[2] + ex[3]
    idx_out = jnp.zeros(logits.shape, jnp.int32)
    gate_out = jnp.zeros(logits.shape, F32)
    for k in range(TOP_K):
        idx_out = jnp.where(ln == k, ti[k], idx_out)
        gate_out = jnp.where(ln == k, ex[k] / tot, gate_out)
    idx_ref[...] = idx_out
    gate_ref[...] = gate_out


def _post_mixer(y_mix, proj, xq_blk, h, mem_k, mem_v, wo_mix, wo_mem, ln_g, ln_b, rw, rb,
                seqlen, tl=ROW_TILE):
    t, cm = y_mix.shape
    nl = seqlen // tl
    d = D_MODEL
    full = lambda a, b: pl.BlockSpec((a, b), lambda i: (0, 0))
    return pl.pallas_call(
        _post_kernel,
        out_shape=(jax.ShapeDtypeStruct((t, d), F32),
                   jax.ShapeDtypeStruct((t, LANES), jnp.int32),
                   jax.ShapeDtypeStruct((t, LANES), F32)),
        grid=(t // tl,),
        in_specs=[pl.BlockSpec((tl, cm), lambda i: (i, 0)),
                  pl.BlockSpec((tl, XATTN_WIDTH), lambda i: (i, xq_blk)),
                  pl.BlockSpec((tl, d), lambda i: (i, 0)),
                  pl.BlockSpec((1, N_MEM, XATTN_WIDTH), lambda i: (i // nl, 0, 0)),
                  pl.BlockSpec((1, N_MEM, XATTN_WIDTH), lambda i: (i // nl, 0, 0)),
                  full(cm, d), full(XATTN_WIDTH, d), full(1, d), full(1, d),
                  full(d, LANES), full(1, LANES)],
        out_specs=(pl.BlockSpec((tl, d), lambda i: (i, 0)),
                   pl.BlockSpec((tl, LANES), lambda i: (i, 0)),
                   pl.BlockSpec((tl, LANES), lambda i: (i, 0))),
        compiler_params=_cparams(("parallel",)),
        name="post_mixer",
    )(y_mix, proj, h, mem_k, mem_v, wo_mix, wo_mem, ln_g, ln_b, rw, rb)


SC_CORES = 2
SC_SUBCORES = 16
SC_WORKERS = SC_CORES * SC_SUBCORES
SC_CHUNK = 32


def _sc_gather(table, idx):
    v, d = table.shape
    b = idx.shape[0]
    per_w = b // SC_WORKERS
    n_chunks = per_w // SC_CHUNK
    assert per_w * SC_WORKERS == b and n_chunks * SC_CHUNK == per_w and n_chunks % 2 == 0
    mesh = plsc.VectorSubcoreMesh(core_axis_name="c", subcore_axis_name="s")

    @functools.partial(
        pl.kernel, mesh=mesh,
        out_type=jax.ShapeDtypeStruct((b, d), F32),
        scratch_types=[pltpu.VMEM((per_w,), jnp.int32),
                       pltpu.VMEM((SC_CHUNK, d), F32),
                       pltpu.VMEM((SC_CHUNK, d), F32),
                       pltpu.SemaphoreType.DMA,
                       pltpu.SemaphoreType.DMA],
    )
    def gather_kernel(table_hbm, idx_hbm, out_hbm, idx_v, rows0, rows1, sem0, sem1):
        wid = lax.axis_index("s") * SC_CORES + lax.axis_index("c")
        base = wid * per_w
        pltpu.sync_copy(idx_hbm.at[pl.ds(pl.multiple_of(base, 8), per_w)], idx_v)
        ring = ((rows0, sem0), (rows1, sem1))

        def gather(c, buf, sem):
            rows = idx_v.at[pl.ds(pl.multiple_of(c * SC_CHUNK, 8), SC_CHUNK)]
            return pltpu.make_async_copy(table_hbm.at[rows], buf, sem)

        for c0, (buf, sem) in enumerate(ring):
            gather(c0, buf, sem).start()

        @pl.loop(0, n_chunks, step=2)
        def _(c):
            for k, (buf, sem) in enumerate(ring):
                cc = c + k
                gather(cc, buf, sem).wait()
                off = pl.multiple_of(base + cc * SC_CHUNK, 8)
                pltpu.sync_copy(buf, out_hbm.at[pl.ds(off, SC_CHUNK)])

                @pl.when(cc + 2 < n_chunks)
                def _():
                    gather(cc + 2, buf, sem).start()

    return gather_kernel(table, idx)


def _sc_dispatch(x, pos_kmajor, n_rows):
    t, d = x.shape
    per_w = t // SC_WORKERS
    n_chunks = per_w // SC_CHUNK
    assert per_w * SC_WORKERS == t and n_chunks * SC_CHUNK == per_w
    mesh = plsc.VectorSubcoreMesh(core_axis_name="c", subcore_axis_name="s")

    @functools.partial(
        pl.kernel, mesh=mesh,
        out_type=jax.ShapeDtypeStruct((n_rows, d), F32),
        scratch_types=[pltpu.VMEM((SC_CHUNK,), jnp.int32),
                       pltpu.VMEM((SC_CHUNK, d), F32)],
    )
    def dispatch_kernel(x_hbm, pos_hbm, out_hbm, idx_v, rows_v):
        wid = lax.axis_index("s") * SC_CORES + lax.axis_index("c")
        base = wid * per_w

        @pl.loop(0, n_chunks)
        def _(j):
            off = pl.multiple_of(base + j * SC_CHUNK, 8)
            pltpu.sync_copy(x_hbm.at[pl.ds(off, SC_CHUNK)], rows_v)
            for k in range(TOP_K):
                pltpu.sync_copy(pos_hbm.at[pl.ds(pl.multiple_of(k * t + off, 8), SC_CHUNK)], idx_v)
                pltpu.sync_copy(rows_v, out_hbm.at[idx_v])

    return dispatch_kernel(x, pos_kmajor)


def _expert_kernel(bexp_ref, nvalid_ref, nused_ref, x_ref, wgu_ref, bgu_ref, wd_ref, bd_ref, y_ref,
                   wgu_bf, wd_bf):
    i = pl.program_id(0)
    de = wd_ref.shape[2]

    @pl.when(i < nused_ref[0])
    def _():
        prev = bexp_ref[jnp.maximum(i - 1, 0)]

        @pl.when((i == 0) | (prev != bexp_ref[i]))
        def _():
            wgu_bf[...] = wgu_ref[0, 0].astype(BF16)
            wd_bf[...] = wd_ref[0, 0].astype(BF16)

        rows = lax.broadcasted_iota(jnp.int32, x_ref.shape, 0)
        xb = jnp.where(rows < nvalid_ref[i], x_ref[...], 0.0).astype(BF16)
        gu = _dot(xb, wgu_bf[...]) + bgu_ref[0, 0]
        x_glu = jnp.minimum(gu[:, :de], SWIGLU_LIMIT)
        x_lin = jnp.clip(gu[:, de:], -SWIGLU_LIMIT, SWIGLU_LIMIT)
        act = x_glu * jax.nn.sigmoid(SWIGLU_ALPHA * x_glu) * (x_lin + 1.0)
        y_ref[...] = _dot(act.astype(BF16), wd_bf[...]) + bd_ref[0, 0]

    @pl.when(i >= nused_ref[0])
    def _():
        y_ref[...] = jnp.zeros_like(y_ref)


def _experts(xr, block_exp, n_valid, n_used, layer, w_gu, b_gu, w_down, b_down, bm=MOE_BM):
    n_rows, d = xr.shape
    n_blocks = n_rows // bm
    nl, ne, _, de2 = w_gu.shape
    de = de2 // 2
    row_blk = lambda i, nu: jnp.minimum(i, nu[0] - 1)
    grid_spec = pltpu.PrefetchScalarGridSpec(
        num_scalar_prefetch=3,
        grid=(n_blocks,),
        in_specs=[pl.BlockSpec((bm, d), lambda i, be, nv, nu: (row_blk(i, nu), 0)),
                  pl.BlockSpec((1, 1, d, de2), lambda i, be, nv, nu: (layer, be[i], 0, 0)),
                  pl.BlockSpec((1, 1, 1, de2), lambda i, be, nv, nu: (layer, be[i], 0, 0)),
                  pl.BlockSpec((1, 1, de, d), lambda i, be, nv, nu: (layer, be[i], 0, 0)),
                  pl.BlockSpec((1, 1, 1, d), lambda i, be, nv, nu: (layer, be[i], 0, 0))],
        out_specs=pl.BlockSpec((bm, d), lambda i, be, nv, nu: (i, 0)),
        scratch_shapes=[pltpu.VMEM((d, de2), BF16),
                        pltpu.VMEM((de, d), BF16)],
    )
    return pl.pallas_call(
        _expert_kernel,
        out_shape=jax.ShapeDtypeStruct((n_rows, d), F32),
        grid_spec=grid_spec,
        compiler_params=_cparams(("arbitrary",)),
        name="experts",
    )(block_exp, n_valid, n_used, xr, w_gu, b_gu.reshape(nl, ne, 1, de2), w_down,
      b_down.reshape(nl, ne, 1, d))


def _combine_kernel(y0_ref, y1_ref, y2_ref, y3_ref, h1_ref, gate_ref, g_ref, b_ref, o_ref):
    gate = gate_ref[...]
    acc = DEEPNORM_ALPHA * h1_ref[...]
    for k, y_ref in enumerate((y0_ref, y1_ref, y2_ref, y3_ref)):
        acc = acc + gate[:, k:k + 1] * y_ref[0]
    o_ref[...] = _layer_norm_rows(acc, g_ref[...], b_ref[...])


def _combine(yg, h1, gates, ln_g, ln_b, tl=ROW_TILE):
    t, d = h1.shape
    ysel = lambda k: pl.BlockSpec((1, tl, d), lambda i: (k, i, 0))
    return pl.pallas_call(
        _combine_kernel,
        out_shape=jax.ShapeDtypeStruct((t, d), F32),
        grid=(t // tl,),
        in_specs=[ysel(0), ysel(1), ysel(2), ysel(3),
                  pl.BlockSpec((tl, d), lambda i: (i, 0)),
                  pl.BlockSpec((tl, LANES), lambda i: (i, 0)),
                  pl.BlockSpec((1, d), lambda i: (0, 0)),
                  pl.BlockSpec((1, d), lambda i: (0, 0))],
        out_specs=pl.BlockSpec((tl, d), lambda i: (i, 0)),
        compiler_params=_cparams(("parallel",)),
        name="combine",
    )(yg, yg, yg, yg, h1, gates, ln_g, ln_b)


def _route(idx, bm=MOE_BM):
    t = idx.shape[0]
    n_assign = t * TOP_K
    n_rows = (-(-n_assign // bm) + N_EXPERTS) * bm
    n_blocks = n_rows // bm
    onehot = (idx[:, :, None] == jnp.arange(N_EXPERTS, dtype=jnp.int32)[None, None, :])
    sel = jnp.sum(onehot.astype(jnp.int32), axis=1)
    csum = jnp.cumsum(sel, axis=0)
    counts = csum[-1]
    rank = csum - sel
    padded = ((counts + bm - 1) // bm) * bm
    pad_end = jnp.cumsum(padded)
    pad_start = pad_end - padded
    pos = pad_start[idx] + jnp.take_along_axis(rank, idx, axis=1)
    block_start = jnp.arange(n_blocks, dtype=jnp.int32) * bm
    block_exp = jnp.minimum(
        jnp.sum((block_start[:, None] >= pad_end[None, :]).astype(jnp.int32), axis=1),
        N_EXPERTS - 1).astype(jnp.int32)
    n_valid = jnp.clip(pad_start[block_exp] + counts[block_exp] - block_start, 0, bm).astype(jnp.int32)
    n_used = (pad_end[-1] // bm).astype(jnp.int32).reshape(1)
    return pos.T.reshape(-1).astype(jnp.int32), block_exp, n_valid, n_used, n_rows


def _take_cols(w, cols):
    cols = np.asarray(cols, np.int32)
    out = jnp.take(w, jnp.asarray(np.maximum(cols, 0)), axis=-1)
    return jnp.where(jnp.asarray(cols >= 0), out, 0.0)


def _head_cols(offset, width, pad):
    cols = []
    for h in range(N_HEADS):
        cols += list(range(offset + h * width, offset + (h + 1) * width)) + [-1] * (pad - width)
    return cols


def _rope_head_cols(offset):
    half = DQK // 2
    slot = DQK_PAD // 2
    cols = []
    for h in range(N_HEADS):
        b = offset + h * DQK
        cols += list(range(b, b + half)) + [-1] * (slot - half)
        cols += list(range(b + half, b + DQK)) + [-1] * (slot - half)
    return cols


_ML_GATE_OFF = 2 * ML_QK + 2 * MIX_WIDTH
_ML_COLS = (_head_cols(0, DQK, DQK_PAD) + _head_cols(ML_QK, DQK, DQK_PAD)
            + _head_cols(2 * ML_QK, DV, DV_PAD) + _head_cols(2 * ML_QK + MIX_WIDTH, DV, DV_PAD)
            + list(range(_ML_GATE_OFF + 2 * N_HEADS, _ML_GATE_OFF + 2 * N_HEADS + XATTN_WIDTH))
            + list(range(_ML_GATE_OFF, _ML_GATE_OFF + 2 * N_HEADS)) + [-1] * (LANES - 2 * N_HEADS))
_RET_COLS = (_rope_head_cols(0) + _rope_head_cols(ML_QK)
             + _head_cols(2 * ML_QK, DV, DV_PAD) + _head_cols(2 * ML_QK + MIX_WIDTH, DV, DV_PAD)
             + list(range(2 * ML_QK + 2 * MIX_WIDTH, 2 * ML_QK + 2 * MIX_WIDTH + XATTN_WIDTH)))
_MIX_PAD_COLS = _head_cols(0, DV, DV_PAD)
_XQ_BLK_PADDED = (2 * HEADS_QK + 2 * HEADS_V) // XATTN_WIDTH
_XQ_BLK_S5 = MIX_WIDTH // XATTN_WIDTH


def kernel(x, mem, positions, mem_w_k, mem_w_v, l0_w_in, l0_s5_a_re, l0_s5_a_im, l0_s5_log_dt, l0_s5_b_re, l0_s5_b_im, l0_s5_c_re, l0_s5_c_im, l0_s5_d, l0_s5_w_glu, l0_s5_b_glu, l1_w_in, l1_ml_conv_q, l1_ml_conv_k, l1_ml_b_i, l1_ml_b_f, l1_ml_norm_g, l2_w_in, l2_ret_norm_g, l3_w_in, l3_s5_a_re, l3_s5_a_im, l3_s5_log_dt, l3_s5_b_re, l3_s5_b_im, l3_s5_c_re, l3_s5_c_im, l3_s5_d, l3_s5_w_glu, l3_s5_b_glu, w_out, ln1_g, ln1_b, ln2_g, ln2_b, router_w, router_b, exp_w_gu, exp_b_gu, exp_w_down, exp_b_down):
    bsz, seqlen, d = x.shape
    t = bsz * seqlen
    h = x.reshape(t, d)

    w_kv = jnp.concatenate([mem_w_k, mem_w_v], axis=1).astype(BF16)
    kv = _inproj(mem.reshape(bsz * N_MEM, d), w_kv).astype(BF16)
    mem_k = kv[:, :XATTN_WIDTH].reshape(bsz, N_MEM, XATTN_WIDTH)
    mem_v = kv[:, XATTN_WIDTH:].reshape(bsz, N_MEM, XATTN_WIDTH)

    half = DQK // 2
    inv = ROPE_BASE ** (-jnp.arange(0, DQK, 2, dtype=F32) / DQK)
    zpad = jnp.zeros((DQK_PAD // 2 - half,), F32)
    inv_pad = jnp.concatenate([inv, zpad, inv, zpad]).reshape(1, DQK_PAD)
    sgn_pad = jnp.concatenate([-jnp.ones((half,), F32), zpad, jnp.ones((half,), F32), zpad]
                              ).reshape(1, DQK_PAD)
    pos_f = positions.astype(F32).reshape(t, 1)

    s5_params = {
        0: (l0_s5_a_re, l0_s5_a_im, l0_s5_log_dt, l0_s5_b_re, l0_s5_b_im, l0_s5_c_re, l0_s5_c_im,
            l0_s5_d, l0_s5_w_glu, l0_s5_b_glu),
        3: (l3_s5_a_re, l3_s5_a_im, l3_s5_log_dt, l3_s5_b_re, l3_s5_b_im, l3_s5_c_re, l3_s5_c_im,
            l3_s5_d, l3_s5_w_glu, l3_s5_b_glu),
    }
    w_ins = (l0_w_in, l1_w_in, l2_w_in, l3_w_in)

    for i in range(DEPTH):
        kind = i % 3
        wo = w_out[i]
        if kind == 0:
            proj = _inproj(h, w_ins[i].astype(BF16))
            y_mix = _s5_mixer(proj, bsz, seqlen, *s5_params[i])
            wo_mix = wo[:MIX_WIDTH].astype(BF16)
            xq_blk = _XQ_BLK_S5
        elif kind == 1:
            proj = _inproj(h, _take_cols(w_ins[i], _ML_COLS).astype(BF16))
            cw = jnp.concatenate([_take_cols(l1_ml_conv_q, _head_cols(0, DQK, DQK_PAD)),
                                  _take_cols(l1_ml_conv_k, _head_cols(0, DQK, DQK_PAD))], axis=1)
            gbias = jnp.concatenate([l1_ml_b_i, l1_ml_b_f,
                                     jnp.zeros((LANES - 2 * N_HEADS,), F32)]).reshape(1, LANES)
            norm_g = _take_cols(l1_ml_norm_g, _MIX_PAD_COLS).reshape(1, HEADS_V)
            y_mix = _mlstm_mixer(proj, bsz, seqlen, cw, gbias, norm_g)
            wo_mix = _take_cols(wo[:MIX_WIDTH].T, _MIX_PAD_COLS).T.astype(BF16)
            xq_blk = _XQ_BLK_PADDED
        else:
            proj = _inproj(h, _take_cols(w_ins[i], _RET_COLS).astype(BF16))
            norm_g = _take_cols(l2_ret_norm_g, _MIX_PAD_COLS).reshape(1, HEADS_V)
            y_mix = _ret_mixer(proj, bsz, seqlen, pos_f, inv_pad, sgn_pad, norm_g)
            wo_mix = _take_cols(wo[:MIX_WIDTH].T, _MIX_PAD_COLS).T.astype(BF16)
            xq_blk = _XQ_BLK_PADDED
        wo_mem = wo[MIX_WIDTH:].astype(BF16)
        rw = jnp.pad(router_w[i], ((0, 0), (0, LANES - N_EXPERTS)))
        rb = jnp.concatenate([router_b[i], jnp.full((LANES - N_EXPERTS,), -1e30, F32)]
                             ).reshape(1, LANES)
        h1, idx, gates = _post_mixer(y_mix, proj, xq_blk, h, mem_k, mem_v, wo_mix, wo_mem,
                                     ln1_g[i].reshape(1, d), ln1_b[i].reshape(1, d), rw, rb, seqlen)
        pos_kmajor, block_exp, n_valid, n_used, n_rows = _route(idx[:, :TOP_K])
        xr = _sc_dispatch(h1, pos_kmajor, n_rows)
        yr = _experts(xr, block_exp, n_valid, n_used, i, exp_w_gu, exp_b_gu, exp_w_down, exp_b_down)
        yg = _sc_gather(yr, pos_kmajor).reshape(TOP_K, t, d)
        h = _combine(yg, h1, gates, ln2_g[i].reshape(1, d), ln2_b[i].reshape(1, d))
    return h.reshape(bsz, seqlen, d)
```

```python
import functools
import math

import numpy as np
import jax
import jax.numpy as jnp
from jax import lax
from jax.experimental import pallas as pl
from jax.experimental.pallas import tpu as pltpu
from jax.experimental.pallas import tpu_sc as plsc

F32 = jnp.float32
BF16 = jnp.bfloat16

D_MODEL = 1024
DEPTH = 4
N_MEM = 256
MIX_WIDTH = 768
XATTN_HEADS = 4
XATTN_WIDTH = 256
XATTN_HEAD_DIM = 64
S5_GROUP = 16
S5_GROUPS = 48
S5_STATE = 64
N_HEADS = 4
DQK = 96
DV = 192
ML_QK = 384
ML_CONV = 4
ML_CHUNK = 64
RET_CHUNK = 128
ROPE_BASE = 10000.0
N_EXPERTS = 32
TOP_K = 4
SWIGLU_LIMIT = 7.0
SWIGLU_ALPHA = 1.702
DEEPNORM_ALPHA = (2.0 * DEPTH) ** 0.25
LN_EPS = 1e-5

LANES = 128
SUBLANES = 8
DQK_PAD = 128
DV_PAD = 256
HEADS_QK = N_HEADS * DQK_PAD
HEADS_V = N_HEADS * DV_PAD
S5_LC = 16
S5_K = S5_LC * S5_GROUP
MOE_BM = 512
ROW_TILE = 256
VMEM_LIMIT = 56 * 1024 * 1024

_NT = (((1,), (1,)), ((), ()))
_TN = (((0,), (0,)), ((), ()))


def _cparams(sem):
    return pltpu.CompilerParams(dimension_semantics=sem, vmem_limit_bytes=VMEM_LIMIT)


def _dot(a, b):
    return jnp.dot(a, b, preferred_element_type=F32)


def _layer_norm_rows(z, g, b):
    mu = jnp.mean(z, axis=-1, keepdims=True)
    d = z - mu
    var = jnp.mean(d * d, axis=-1, keepdims=True)
    return d * lax.rsqrt(var + LN_EPS) * g + b


def _inproj_kernel(x_ref, w_ref, o_ref):
    xb = x_ref[...].astype(BF16)
    n = o_ref.shape[1]
    step = 512
    for c0 in range(0, n, step):
        c1 = min(c0 + step, n)
        o_ref[:, c0:c1] = _dot(xb, w_ref[:, c0:c1])


def _inproj(x, w_bf16, tm=ROW_TILE):
    t, d = x.shape
    n = w_bf16.shape[1]
    return pl.pallas_call(
        _inproj_kernel,
        out_shape=jax.ShapeDtypeStruct((t, n), F32),
        grid=(t // tm,),
        in_specs=[pl.BlockSpec((tm, d), lambda i: (i, 0)),
                  pl.BlockSpec((d, n), lambda i: (0, 0))],
        out_specs=pl.BlockSpec((tm, n), lambda i: (i, 0)),
        compiler_params=_cparams(("parallel",)),
        name="inproj",
    )(x, w_bf16)


S5_LANE_GROUPS = S5_K // S5_GROUP
S5_BLOCKS = MIX_WIDTH // S5_K
S5_SW = S5_LANE_GROUPS * S5_STATE
S5_LB = 512


def _block_diag(m):
    nb, g, a, b = m.shape
    eye = jnp.eye(g, dtype=m.dtype)
    return (m[:, :, :, None, :] * eye[None, :, None, :, None]).reshape(nb, g * a, g * b)


def _s5_prep(a_re, a_im, log_dt, b_re, b_im, c_re, c_im):
    hp = lax.Precision.HIGHEST
    lam_re = jnp.minimum(a_re.astype(F32), -1e-4)
    lam_im = a_im.astype(F32)
    dt = jnp.exp(log_dt.astype(F32))[:, None]
    mag = jnp.exp(dt * lam_re)
    ab_re = mag * jnp.cos(dt * lam_im)
    ab_im = mag * jnp.sin(dt * lam_im)
    den = lam_re * lam_re + lam_im * lam_im
    num_re = ab_re - 1.0
    coef_re = (num_re * lam_re + ab_im * lam_im) / den
    coef_im = (ab_im * lam_re - num_re * lam_im) / den
    bre = b_re.astype(F32)
    bim = b_im.astype(F32)
    bb_re = coef_re[..., None] * bre - coef_im[..., None] * bim
    bb_im = coef_re[..., None] * bim + coef_im[..., None] * bre
    pr = [jnp.ones_like(ab_re)]
    pi = [jnp.zeros_like(ab_im)]
    for _ in range(S5_LC):
        r, i = pr[-1], pi[-1]
        pr.append(r * ab_re - i * ab_im)
        pi.append(r * ab_im + i * ab_re)
    pw_re = jnp.stack(pr)
    pw_im = jnp.stack(pi)
    p_re = pw_re[:S5_LC, :, :, None] * bb_re[None] - pw_im[:S5_LC, :, :, None] * bb_im[None]
    p_im = pw_re[:S5_LC, :, :, None] * bb_im[None] + pw_im[:S5_LC, :, :, None] * bb_re[None]
    cre = c_re.astype(F32)
    cim = c_im.astype(F32)
    kmat = (jnp.einsum('ghp,tgpk->tgkh', cre, p_re, precision=hp)
            - jnp.einsum('ghp,tgpk->tgkh', cim, p_im, precision=hp))
    nb, lg = S5_BLOCKS, S5_LANE_GROUPS
    toep = _block_diag(kmat.reshape(S5_LC * nb, lg, S5_GROUP, S5_GROUP)
                       ).reshape(S5_LC, nb, S5_K, S5_K).transpose(1, 0, 2, 3)
    bmat = jnp.concatenate(
        [_block_diag(bb_re.transpose(0, 2, 1).reshape(nb, lg, S5_GROUP, S5_STATE)),
         _block_diag(bb_im.transpose(0, 2, 1).reshape(nb, lg, S5_GROUP, S5_STATE))], axis=2)
    cmat = jnp.concatenate(
        [_block_diag(cre.transpose(0, 2, 1).reshape(nb, lg, S5_STATE, S5_GROUP)),
         _block_diag(-cim.transpose(0, 2, 1).reshape(nb, lg, S5_STATE, S5_GROUP))], axis=1)
    lane = lambda v: v.reshape(nb, 1, S5_SW)
    avec = jnp.concatenate([lane(ab_re), lane(ab_im), lane(pw_re[S5_LC]), lane(pw_im[S5_LC])],
                           axis=1)
    return toep.astype(BF16), bmat.astype(BF16), cmat.astype(BF16), avec


def _s5_kernel(u_ref, t_ref, b_ref, c_ref, a_ref, y_ref, s_acc, x_prev, x_carry, u_half, y_half,
               *, lb):
    li = pl.program_id(1)
    bsz = u_ref.shape[0]
    ncb = lb // S5_LC
    m = bsz * ncb

    @pl.when(li == 0)
    def _():
        x_carry[...] = jnp.zeros_like(x_carry)

    ar = a_ref[0, 0:1, :]
    ai = a_ref[0, 1:2, :]
    alr = a_ref[0, 2:3, :]
    ali = a_ref[0, 3:4, :]

    def cmul(zr, zi, wr, wi):
        return wr * zr - wi * zi, wr * zi + wi * zr

    n_half = S5_K // LANES
    for hf in range(n_half):
        u_half[hf] = u_ref[:, :, hf * LANES:(hf + 1) * LANES].reshape(bsz * lb, LANES)
    xs = []
    for s in range(S5_LC):
        halves = [jnp.concatenate([u_half[hf, pl.ds(c * S5_LC + s, bsz, stride=lb), :]
                                   for c in range(ncb)], axis=0) for hf in range(n_half)]
        xs.append(jnp.concatenate(halves, axis=1).astype(BF16))
    bmat = b_ref[0]
    sr = jnp.zeros((m, S5_SW), F32)
    si = jnp.zeros((m, S5_SW), F32)
    for s in range(S5_LC):
        bu = _dot(xs[s], bmat)
        sr, si = cmul(sr, si, ar, ai)
        sr = sr + bu[:, :S5_SW]
        si = si + bu[:, S5_SW:]
    s_acc[:, :S5_SW] = sr
    s_acc[:, S5_SW:] = si

    xr = x_carry[:, :S5_SW]
    xi = x_carry[:, S5_SW:]
    for c in range(ncb):
        x_prev[c * bsz:(c + 1) * bsz, :S5_SW] = xr
        x_prev[c * bsz:(c + 1) * bsz, S5_SW:] = xi
        loc = s_acc[c * bsz:(c + 1) * bsz, :]
        xr, xi = cmul(xr, xi, alr, ali)
        xr = xr + loc[:, :S5_SW]
        xi = xi + loc[:, S5_SW:]
    x_carry[:, :S5_SW] = xr
    x_carry[:, S5_SW:] = xi

    cmat = c_ref[0]
    zr = x_prev[:, :S5_SW]
    zi = x_prev[:, S5_SW:]
    for j in range(S5_LC):
        zr, zi = cmul(zr, zi, ar, ai)
        yj = _dot(jnp.concatenate([zr, zi], axis=1).astype(BF16), cmat)
        for s in range(j + 1):
            yj = yj + _dot(xs[s], t_ref[0, j - s])
        for hf in range(n_half):
            for c in range(ncb):
                y_half[hf, pl.ds(c * S5_LC + j, bsz, stride=lb), :] = (
                    yj[c * bsz:(c + 1) * bsz, hf * LANES:(hf + 1) * LANES])
    for hf in range(n_half):
        y_ref[:, :, hf * LANES:(hf + 1) * LANES] = y_half[hf].reshape(bsz, lb, LANES)


def _s5_scan(proj3, mats, lb=S5_LB):
    toep, bmat, cmat, avec = mats
    bsz, seqlen, _ = proj3.shape
    return pl.pallas_call(
        functools.partial(_s5_kernel, lb=lb),
        out_shape=jax.ShapeDtypeStruct((bsz, seqlen, MIX_WIDTH), F32),
        grid=(S5_BLOCKS, seqlen // lb),
        in_specs=[pl.BlockSpec((bsz, lb, S5_K), lambda v, l: (0, l, v)),
                  pl.BlockSpec((1, S5_LC, S5_K, S5_K), lambda v, l: (v, 0, 0, 0)),
                  pl.BlockSpec((1, S5_K, 2 * S5_SW), lambda v, l: (v, 0, 0)),
                  pl.BlockSpec((1, 2 * S5_SW, S5_K), lambda v, l: (v, 0, 0)),
                  pl.BlockSpec((1, 4, S5_SW), lambda v, l: (v, 0, 0))],
        out_specs=pl.BlockSpec((bsz, lb, S5_K), lambda v, l: (0, l, v)),
        scratch_shapes=[pltpu.VMEM((bsz * (lb // S5_LC), 2 * S5_SW), F32),
                        pltpu.VMEM((bsz * (lb // S5_LC), 2 * S5_SW), F32),
                        pltpu.VMEM((bsz, 2 * S5_SW), F32),
                        pltpu.VMEM((S5_K // LANES, bsz * lb, LANES), F32),
                        pltpu.VMEM((S5_K // LANES, bsz * lb, LANES), F32)],
        compiler_params=_cparams(("parallel", "arbitrary")),
        name="s5_scan",
    )(proj3, toep, bmat, cmat, avec)


def _s5_post_kernel(y_ref, u_ref, d_ref, w_ref, b_ref, o_ref):
    y = y_ref[...] + d_ref[...] * u_ref[...]
    y = jax.nn.gelu(y)
    o_ref[...] = y * jax.nn.sigmoid(_dot(y.astype(BF16), w_ref[...]) + b_ref[...])


def _s5_post(y_ssm, proj, d_skip, w_glu_bf16, b_glu, tm=ROW_TILE):
    t = y_ssm.shape[0]
    w = MIX_WIDTH
    return pl.pallas_call(
        _s5_post_kernel,
        out_shape=jax.ShapeDtypeStruct((t, w), F32),
        grid=(t // tm,),
        in_specs=[pl.BlockSpec((tm, w), lambda i: (i, 0)),
                  pl.BlockSpec((tm, w), lambda i: (i, 0)),
                  pl.BlockSpec((1, w), lambda i: (0, 0)),
                  pl.BlockSpec((w, w), lambda i: (0, 0)),
                  pl.BlockSpec((1, w), lambda i: (0, 0))],
        out_specs=pl.BlockSpec((tm, w), lambda i: (i, 0)),
        compiler_params=_cparams(("parallel",)),
        name="s5_post",
    )(y_ssm, proj, d_skip.reshape(1, w), w_glu_bf16, b_glu.reshape(1, w))


def _s5_mixer(proj, bsz, seqlen, a_re, a_im, log_dt, b_re, b_im, c_re, c_im, d_skip, w_glu, b_glu):
    mats = _s5_prep(a_re, a_im, log_dt, b_re, b_im, c_re, c_im)
    y = _s5_scan(proj.reshape(bsz, seqlen, proj.shape[1]), mats)
    return _s5_post(y.reshape(bsz * seqlen, MIX_WIDTH), proj, d_skip, w_glu.astype(BF16), b_glu)


def _head_norm_padded(hv, g):
    lane = lax.broadcasted_iota(jnp.int32, hv.shape, 1)
    real = lane < DV
    mu = jnp.sum(hv, axis=-1, keepdims=True) * (1.0 / DV)
    d = jnp.where(real, hv - mu, 0.0)
    var = jnp.sum(d * d, axis=-1, keepdims=True) * (1.0 / DV)
    return d * lax.rsqrt(var + LN_EPS) * g


def _log_sigmoid(x):
    return jnp.minimum(x, 0.0) - jnp.log(1.0 + jnp.exp(-jnp.abs(x)))


def _mlstm_kernel(q_ref, k_ref, v_ref, o_ref, gt_ref, cw_ref, gb_ref, ng_ref,
                  y_ref, cbuf, c_st, n_st, m_st, *, tl):
    i = pl.program_id(1)

    @pl.when(i == 0)
    def _():
        cbuf[0:SUBLANES, :] = jnp.zeros((SUBLANES, 2 * HEADS_QK), F32)
        c_st[...] = jnp.zeros_like(c_st)
        n_st[...] = jnp.zeros_like(n_st)
        m_st[...] = jnp.zeros_like(m_st)

    cbuf[SUBLANES:SUBLANES + tl, 0:HEADS_QK] = q_ref[...]
    cbuf[SUBLANES:SUBLANES + tl, HEADS_QK:2 * HEADS_QK] = k_ref[...]
    acc = jnp.zeros((tl, 2 * HEADS_QK), F32)
    for w in range(ML_CONV):
        acc = acc + cbuf[pl.ds(SUBLANES - (ML_CONV - 1) + w, tl), :] * cw_ref[w:w + 1, :]
    qk = acc * jax.nn.sigmoid(acc)
    cbuf[0:SUBLANES, :] = cbuf[tl:tl + SUBLANES, :]

    gt = gt_ref[...] + gb_ref[...]
    lf = _log_sigmoid(gt)
    gt_t = gt.T
    lf_t = _log_sigmoid(gt_t)

    cl = ML_CHUNK
    row = lax.broadcasted_iota(jnp.int32, (cl, cl), 0)
    col = lax.broadcasted_iota(jnp.int32, (cl, cl), 1)
    tri = row >= col
    scale = DQK ** -0.5
    for cc in range(tl // cl):
        r0 = cc * cl
        for h in range(N_HEADS):
            ig_col = gt[r0:r0 + cl, h:h + 1]
            lf_col = lf[r0:r0 + cl, N_HEADS + h:N_HEADS + h + 1]
            ig_row = gt_t[h:h + 1, r0:r0 + cl]
            lf_row = lf_t[N_HEADS + h:N_HEADS + h + 1, r0:r0 + cl]
            bcum_col = jnp.sum(jnp.where(tri, lf_row, 0.0), axis=1, keepdims=True)
            bcum_row = jnp.sum(jnp.where(col >= row, lf_col, 0.0), axis=0, keepdims=True)
            btot = jnp.sum(lf_row, axis=1, keepdims=True)
            w_row = btot - bcum_row + ig_row
            m_loc = jnp.max(w_row, axis=1, keepdims=True)
            e_col = jnp.exp(btot - bcum_col + ig_col - m_loc)
            m_prev = m_st[h:h + 1, 0:1]
            c_prev = c_st[h]
            n_prev = n_st[h:h + 1, :]
            q = qk[r0:r0 + cl, h * DQK_PAD:(h + 1) * DQK_PAD] * scale
            k = qk[r0:r0 + cl, HEADS_QK + h * DQK_PAD:HEADS_QK + (h + 1) * DQK_PAD]
            v = v_ref[r0:r0 + cl, h * DV_PAD:(h + 1) * DV_PAD]
            qb = q.astype(BF16)
            kb = k.astype(BF16)
            vb = v.astype(BF16)
            dmat = jnp.where(tri, bcum_col - bcum_row + ig_row, -jnp.inf)
            g_col = bcum_col + m_prev
            m_row = jnp.maximum(g_col, jnp.max(dmat, axis=1, keepdims=True))
            inter = jnp.exp(g_col - m_row)
            s_qk = lax.dot_general(qb, kb, _NT, preferred_element_type=F32) * jnp.exp(dmat - m_row)
            num = inter * _dot(qb, c_prev.astype(BF16)) + _dot(s_qk.astype(BF16), vb)
            den = (inter * jnp.sum(q * n_prev, axis=1, keepdims=True)
                   + jnp.sum(s_qk, axis=1, keepdims=True))
            hv = num / jnp.maximum(jnp.abs(den), jnp.exp(-m_row))
            ke = k * e_col
            kv = lax.dot_general(ke.astype(BF16), vb, _TN, preferred_element_type=F32)
            nk = jnp.sum(ke, axis=0, keepdims=True)
            m_new = jnp.maximum(btot + m_prev, m_loc)
            sa = jnp.exp(btot + m_prev - m_new)
            sb = jnp.exp(m_loc - m_new)
            c_st[h] = sa * c_prev + sb * kv
            n_st[h:h + 1, :] = sa * n_prev + sb * nk
            m_st[h:h + 1, :] = jnp.broadcast_to(m_new, (1, LANES))
            hn = _head_norm_padded(hv, ng_ref[0:1, h * DV_PAD:(h + 1) * DV_PAD])
            og = o_ref[r0:r0 + cl, h * DV_PAD:(h + 1) * DV_PAD]
            y_ref[r0:r0 + cl, h * DV_PAD:(h + 1) * DV_PAD] = jax.nn.sigmoid(og) * hn


def _mlstm_mixer(proj, bsz, seqlen, cw, gbias, norm_g, tl=ROW_TILE):
    t = proj.shape[0]
    nl = seqlen // tl
    rows = lambda b, i: b * nl + i
    return pl.pallas_call(
        functools.partial(_mlstm_kernel, tl=tl),
        out_shape=jax.ShapeDtypeStruct((t, HEADS_V), F32),
        grid=(bsz, nl),
        in_specs=[pl.BlockSpec((tl, HEADS_QK), lambda b, i: (rows(b, i), 0)),
                  pl.BlockSpec((tl, HEADS_QK), lambda b, i: (rows(b, i), 1)),
                  pl.BlockSpec((tl, HEADS_V), lambda b, i: (rows(b, i), 1)),
                  pl.BlockSpec((tl, HEADS_V), lambda b, i: (rows(b, i), 2)),
                  pl.BlockSpec((tl, LANES), lambda b, i: (rows(b, i), 26)),
                  pl.BlockSpec((ML_CONV, 2 * HEADS_QK), lambda b, i: (0, 0)),
                  pl.BlockSpec((1, LANES), lambda b, i: (0, 0)),
                  pl.BlockSpec((1, HEADS_V), lambda b, i: (0, 0))],
        out_specs=pl.BlockSpec((tl, HEADS_V), lambda b, i: (rows(b, i), 0)),
        scratch_shapes=[pltpu.VMEM((tl + SUBLANES, 2 * HEADS_QK), F32),
                        pltpu.VMEM((N_HEADS, DQK_PAD, DV_PAD), F32),
                        pltpu.VMEM((SUBLANES, DQK_PAD), F32),
                        pltpu.VMEM((SUBLANES, LANES), F32)],
        compiler_params=_cparams(("parallel", "arbitrary")),
        name="mlstm",
    )(proj, proj, proj, proj, proj, cw, gbias, norm_g)


def _ret_log_gamma(h):
    return float(np.log(np.float32(1.0) - np.power(np.float32(2.0), np.float32(-5.0 - h))))


def _ret_kernel(q_ref, k_ref, v_ref, g_ref, pos_ref, inv_ref, sgn_ref, ng_ref,
                y_ref, s_st, *, tl):
    i = pl.program_id(1)

    @pl.when(i == 0)
    def _():
        s_st[...] = jnp.zeros_like(s_st)

    ang = pos_ref[...] * inv_ref[...]
    cos_t = jnp.cos(ang)
    sin_t = jnp.sin(ang) * sgn_ref[...]
    cl = RET_CHUNK
    row = lax.broadcasted_iota(jnp.int32, (cl, cl), 0)
    col = lax.broadcasted_iota(jnp.int32, (cl, cl), 1)
    rel = (row - col).astype(F32)
    jcol = lax.broadcasted_iota(jnp.int32, (cl, 1), 0).astype(F32)
    kscale = DQK ** -0.5
    for h in range(N_HEADS):
        lg = _ret_log_gamma(h)
        decay = jnp.where(rel >= 0, jnp.exp(jnp.maximum(rel, 0.0) * lg), 0.0)
        zeta = jnp.exp((cl - 1 - jcol) * lg)
        xi = jnp.exp((jcol + 1.0) * lg)
        chunk_decay = float(np.exp(np.float32(cl) * np.float32(lg)))
        qh = q_ref[:, h * DQK_PAD:(h + 1) * DQK_PAD]
        kh = k_ref[:, h * DQK_PAD:(h + 1) * DQK_PAD]
        qh = qh * cos_t + pltpu.roll(qh, DQK_PAD // 2, 1) * sin_t
        kh = (kh * cos_t + pltpu.roll(kh, DQK_PAD // 2, 1) * sin_t) * kscale
        for cc in range(tl // cl):
            r0 = cc * cl
            qb = qh[r0:r0 + cl].astype(BF16)
            k = kh[r0:r0 + cl]
            kb = k.astype(BF16)
            v = v_ref[r0:r0 + cl, h * DV_PAD:(h + 1) * DV_PAD]
            vb = v.astype(BF16)
            s_prev = s_st[h]
            s = lax.dot_general(qb, kb, _NT, preferred_element_type=F32) * decay
            intra = _dot(s.astype(BF16), vb)
            cross = _dot(qb, s_prev.astype(BF16)) * xi
            r = lax.dot_general((k * zeta).astype(BF16), vb, _TN, preferred_element_type=F32)
            s_st[h] = chunk_decay * s_prev + r
            hn = _head_norm_padded(intra + cross, ng_ref[0:1, h * DV_PAD:(h + 1) * DV_PAD])
            gate = g_ref[r0:r0 + cl, h * DV_PAD:(h + 1) * DV_PAD]
            y_ref[r0:r0 + cl, h * DV_PAD:(h + 1) * DV_PAD] = gate * jax.nn.sigmoid(gate) * hn


def _ret_mixer(proj, bsz, seqlen, pos_f, inv_pad, sgn_pad, norm_g, tl=ROW_TILE):
    t = proj.shape[0]
    nl = seqlen // tl
    rows = lambda b, i: b * nl + i
    return pl.pallas_call(
        functools.partial(_ret_kernel, tl=tl),
        out_shape=jax.ShapeDtypeStruct((t, HEADS_V), F32),
        grid=(bsz, nl),
        in_specs=[pl.BlockSpec((tl, HEADS_QK), lambda b, i: (rows(b, i), 0)),
                  pl.BlockSpec((tl, HEADS_QK), lambda b, i: (rows(b, i), 1)),
                  pl.BlockSpec((tl, HEADS_V), lambda b, i: (rows(b, i), 1)),
                  pl.BlockSpec((tl, HEADS_V), lambda b, i: (rows(b, i), 2)),
                  pl.BlockSpec((tl, 1), lambda b, i: (rows(b, i), 0)),
                  pl.BlockSpec((1, DQK_PAD), lambda b, i: (0, 0)),
                  pl.BlockSpec((1, DQK_PAD), lambda b, i: (0, 0)),
                  pl.BlockSpec((1, HEADS_V), lambda b, i: (0, 0))],
        out_specs=pl.BlockSpec((tl, HEADS_V), lambda b, i: (rows(b, i), 0)),
        scratch_shapes=[pltpu.VMEM((N_HEADS, DQK_PAD, DV_PAD), F32)],
        compiler_params=_cparams(("parallel", "arbitrary")),
        name="retention",
    )(proj, proj, proj, proj, pos_f, inv_pad, sgn_pad, norm_g)


def _post_kernel(ym_ref, xq_ref, h_ref, mk_ref, mv_ref, wom_ref, wox_ref, g_ref, b_ref,
                 rwh_ref, rwl_ref, rb_ref, h1_ref, idx_ref, gate_ref):
    tl = xq_ref.shape[0]
    xq = xq_ref[...] * (XATTN_HEAD_DIM ** -0.5)
    lane = lax.broadcasted_iota(jnp.int32, (tl, XATTN_WIDTH), 1)
    head = lane // XATTN_HEAD_DIM
    mk = mk_ref[0]
    mv = mv_ref[0]
    ymem = jnp.zeros((tl, XATTN_WIDTH), F32)
    for hh in range(XATTN_HEADS):
        sel = head == hh
        qh = jnp.where(sel, xq, 0.0).astype(BF16)
        s = lax.dot_general(qh, mk, _NT, preferred_element_type=F32)
        s = s - jnp.max(s, axis=-1, keepdims=True)
        p = jnp.exp(s)
        p = p / jnp.sum(p, axis=-1, keepdims=True)
        ymem = jnp.where(sel, _dot(p.astype(BF16), mv), ymem)
    y = _dot(ym_ref[...].astype(BF16), wom_ref[...]) + _dot(ymem.astype(BF16), wox_ref[...])
    h1 = _layer_norm_rows(DEEPNORM_ALPHA * h_ref[...] + y, g_ref[...], b_ref[...])
    h1_ref[...] = h1
    h_hi = h1.astype(BF16)
    h_lo = (h1 - h_hi.astype(F32)).astype(BF16)
    logits = (_dot(h_hi, rwh_ref[...]) + _dot(h_lo, rwh_ref[...]) + _dot(h_hi, rwl_ref[...])
              + rb_ref[...])
    ln = lax.broadcasted_iota(jnp.int32, logits.shape, 1)
    vals = logits
    tv, ti = [], []
    for _ in range(TOP_K):
        m = jnp.max(vals, axis=-1, keepdims=True)
        ix = jnp.min(jnp.where(vals == m, ln, LANES), axis=-1, keepdims=True)
        tv.append(m)
        ti.append(ix)
        vals = jnp.where(ln == ix, -jnp.inf, vals)
    ex = [jnp.exp(v - tv[0]) for v in tv]
    tot = ex[0] + ex[1] + ex[2] + ex[3]
    idx_out = jnp.zeros(logits.shape, jnp.int32)
    gate_out = jnp.zeros(logits.shape, F32)
    for k in range(TOP_K):
        idx_out = jnp.where(ln == k, ti[k], idx_out)
        gate_out = jnp.where(ln == k, ex[k] / tot, gate_out)
    idx_ref[...] = idx_out
    gate_ref[...] = gate_out


def _post_mixer(y_mix, proj, xq_blk, h, mem_k, mem_v, wo_mix, wo_mem, ln_g, ln_b, rw, rb,
                seqlen, tl=4 * ROW_TILE):
    t, cm = y_mix.shape
    nl = seqlen // tl
    d = D_MODEL
    rw_hi = rw.astype(BF16)
    rw_lo = (rw - rw_hi.astype(F32)).astype(BF16)
    full = lambda a, b: pl.BlockSpec((a, b), lambda i: (0, 0))
    return pl.pallas_call(
        _post_kernel,
        out_shape=(jax.ShapeDtypeStruct((t, d), F32),
                   jax.ShapeDtypeStruct((t, LANES), jnp.int32),
                   jax.ShapeDtypeStruct((t, LANES), F32)),
        grid=(t // tl,),
        in_specs=[pl.BlockSpec((tl, cm), lambda i: (i, 0)),
                  pl.BlockSpec((tl, XATTN_WIDTH), lambda i: (i, xq_blk)),
                  pl.BlockSpec((tl, d), lambda i: (i, 0)),
                  pl.BlockSpec((1, N_MEM, XATTN_WIDTH), lambda i: (i // nl, 0, 0)),
                  pl.BlockSpec((1, N_MEM, XATTN_WIDTH), lambda i: (i // nl, 0, 0)),
                  full(cm, d), full(XATTN_WIDTH, d), full(1, d), full(1, d),
                  full(d, LANES), full(d, LANES), full(1, LANES)],
        out_specs=(pl.BlockSpec((tl, d), lambda i: (i, 0)),
                   pl.BlockSpec((tl, LANES), lambda i: (i, 0)),
                   pl.BlockSpec((tl, LANES), lambda i: (i, 0))),
        compiler_params=_cparams(("parallel",)),
        name="post_mixer",
    )(y_mix, proj, h, mem_k, mem_v, wo_mix, wo_mem, ln_g, ln_b, rw_hi, rw_lo, rb)


SC_CORES = 2
SC_SUBCORES = 16
SC_WORKERS = SC_CORES * SC_SUBCORES
SC_CHUNK = 32


def _sc_gather(table, idx):
    v, d = table.shape
    b = idx.shape[0]
    per_w = b // SC_WORKERS
    n_chunks = per_w // SC_CHUNK
    assert per_w * SC_WORKERS == b and n_chunks * SC_CHUNK == per_w and n_chunks % 2 == 0
    mesh = plsc.VectorSubcoreMesh(core_axis_name="c", subcore_axis_name="s")

    @functools.partial(
        pl.kernel, mesh=mesh,
        out_type=jax.ShapeDtypeStruct((b, d), F32),
        scratch_types=[pltpu.VMEM((per_w,), jnp.int32),
                       pltpu.VMEM((SC_CHUNK, d), F32),
                       pltpu.VMEM((SC_CHUNK, d), F32),
                       pltpu.SemaphoreType.DMA,
                       pltpu.SemaphoreType.DMA],
    )
    def gather_kernel(table_hbm, idx_hbm, out_hbm, idx_v, rows0, rows1, sem0, sem1):
        wid = lax.axis_index("s") * SC_CORES + lax.axis_index("c")
        base = wid * per_w
        pltpu.sync_copy(idx_hbm.at[pl.ds(pl.multiple_of(base, 8), per_w)], idx_v)
        ring = ((rows0, sem0), (rows1, sem1))

        def gather(c, buf, sem):
            rows = idx_v.at[pl.ds(pl.multiple_of(c * SC_CHUNK, 8), SC_CHUNK)]
            return pltpu.make_async_copy(table_hbm.at[rows], buf, sem)

        for c0, (buf, sem) in enumerate(ring):
            gather(c0, buf, sem).start()

        @pl.loop(0, n_chunks, step=2)
        def _(c):
            for k, (buf, sem) in enumerate(ring):
                cc = c + k
                gather(cc, buf, sem).wait()
                off = pl.multiple_of(base + cc * SC_CHUNK, 8)
                pltpu.sync_copy(buf, out_hbm.at[pl.ds(off, SC_CHUNK)])

                @pl.when(cc + 2 < n_chunks)
                def _():
                    gather(cc + 2, buf, sem).start()

    return gather_kernel(table, idx)


def _sc_dispatch(x, pos_kmajor, n_rows):
    t, d = x.shape
    per_w = t // SC_WORKERS
    n_chunks = per_w // SC_CHUNK
    assert per_w * SC_WORKERS == t and n_chunks * SC_CHUNK == per_w
    mesh = plsc.VectorSubcoreMesh(core_axis_name="c", subcore_axis_name="s")

    @functools.partial(
        pl.kernel, mesh=mesh,
        out_type=jax.ShapeDtypeStruct((n_rows, d), F32),
        scratch_types=[pltpu.VMEM((SC_CHUNK,), jnp.int32),
                       pltpu.VMEM((SC_CHUNK, d), F32)],
    )
    def dispatch_kernel(x_hbm, pos_hbm, out_hbm, idx_v, rows_v):
        wid = lax.axis_index("s") * SC_CORES + lax.axis_index("c")
        base = wid * per_w

        @pl.loop(0, n_chunks)
        def _(j):
            off = pl.multiple_of(base + j * SC_CHUNK, 8)
            pltpu.sync_copy(x_hbm.at[pl.ds(off, SC_CHUNK)], rows_v)
            for k in range(TOP_K):
                pltpu.sync_copy(pos_hbm.at[pl.ds(pl.multiple_of(k * t + off, 8), SC_CHUNK)], idx_v)
                pltpu.sync_copy(rows_v, out_hbm.at[idx_v])

    return dispatch_kernel(x, pos_kmajor)


def _expert_kernel(bexp_ref, nvalid_ref, nused_ref, x_ref, wgu_ref, bgu_ref, wd_ref, bd_ref, y_ref,
                   wgu_bf, wd_bf):
    i = pl.program_id(0)
    de = wd_ref.shape[2]

    @pl.when(i < nused_ref[0])
    def _():
        prev = bexp_ref[jnp.maximum(i - 1, 0)]

        @pl.when((i == 0) | (prev != bexp_ref[i]))
        def _():
            wgu_bf[...] = wgu_ref[0, 0].astype(BF16)
            wd_bf[...] = wd_ref[0, 0].astype(BF16)

        rows = lax.broadcasted_iota(jnp.int32, x_ref.shape, 0)
        xb = jnp.where(rows < nvalid_ref[i], x_ref[...], 0.0).astype(BF16)
        gu = _dot(xb, wgu_bf[...]) + bgu_ref[0, 0]
        x_glu = jnp.minimum(gu[:, :de], SWIGLU_LIMIT)
        x_lin = jnp.clip(gu[:, de:], -SWIGLU_LIMIT, SWIGLU_LIMIT)
        act = x_glu * jax.nn.sigmoid(SWIGLU_ALPHA * x_glu) * (x_lin + 1.0)
        y_ref[...] = _dot(act.astype(BF16), wd_bf[...]) + bd_ref[0, 0]

    @pl.when(i >= nused_ref[0])
    def _():
        y_ref[...] = jnp.zeros_like(y_ref)


def _experts(xr, block_exp, n_valid, n_used, layer, w_gu, b_gu, w_down, b_down, bm=MOE_BM):
    n_rows, d = xr.shape
    n_blocks = n_rows // bm
    nl, ne, _, de2 = w_gu.shape
    de = de2 // 2
    row_blk = lambda i, nu: jnp.minimum(i, nu[0] - 1)
    grid_spec = pltpu.PrefetchScalarGridSpec(
        num_scalar_prefetch=3,
        grid=(n_blocks,),
        in_specs=[pl.BlockSpec((bm, d), lambda i, be, nv, nu: (row_blk(i, nu), 0)),
                  pl.BlockSpec((1, 1, d, de2), lambda i, be, nv, nu: (layer, be[i], 0, 0)),
                  pl.BlockSpec((1, 1, 1, de2), lambda i, be, nv, nu: (layer, be[i], 0, 0)),
                  pl.BlockSpec((1, 1, de, d), lambda i, be, nv, nu: (layer, be[i], 0, 0)),
                  pl.BlockSpec((1, 1, 1, d), lambda i, be, nv, nu: (layer, be[i], 0, 0))],
        out_specs=pl.BlockSpec((bm, d), lambda i, be, nv, nu: (i, 0)),
        scratch_shapes=[pltpu.VMEM((d, de2), BF16),
                        pltpu.VMEM((de, d), BF16)],
    )
    return pl.pallas_call(
        _expert_kernel,
        out_shape=jax.ShapeDtypeStruct((n_rows, d), F32),
        grid_spec=grid_spec,
        compiler_params=_cparams(("arbitrary",)),
        name="experts",
    )(block_exp, n_valid, n_used, xr, w_gu, b_gu.reshape(nl, ne, 1, de2), w_down,
      b_down.reshape(nl, ne, 1, d))


def _combine_kernel(y0_ref, y1_ref, y2_ref, y3_ref, h1_ref, gate_ref, g_ref, b_ref, o_ref):
    gate = gate_ref[...]
    acc = DEEPNORM_ALPHA * h1_ref[...]
    for k, y_ref in enumerate((y0_ref, y1_ref, y2_ref, y3_ref)):
        acc = acc + gate[:, k:k + 1] * y_ref[0]
    o_ref[...] = _layer_norm_rows(acc, g_ref[...], b_ref[...])


def _combine(yg, h1, gates, ln_g, ln_b, tl=ROW_TILE):
    t, d = h1.shape
    ysel = lambda k: pl.BlockSpec((1, tl, d), lambda i: (k, i, 0))
    return pl.pallas_call(
        _combine_kernel,
        out_shape=jax.ShapeDtypeStruct((t, d), F32),
        grid=(t // tl,),
        in_specs=[ysel(0), ysel(1), ysel(2), ysel(3),
                  pl.BlockSpec((tl, d), lambda i: (i, 0)),
                  pl.BlockSpec((tl, LANES), lambda i: (i, 0)),
                  pl.BlockSpec((1, d), lambda i: (0, 0)),
                  pl.BlockSpec((1, d), lambda i: (0, 0))],
        out_specs=pl.BlockSpec((tl, d), lambda i: (i, 0)),
        compiler_params=_cparams(("parallel",)),
        name="combine",
    )(yg, yg, yg, yg, h1, gates, ln_g, ln_b)


def _route(idx, bm=MOE_BM):
    t = idx.shape[0]
    n_assign = t * TOP_K
    n_rows = (-(-n_assign // bm) + N_EXPERTS) * bm
    n_blocks = n_rows // bm
    onehot = (idx[:, :, None] == jnp.arange(N_EXPERTS, dtype=jnp.int32)[None, None, :])
    sel = jnp.sum(onehot.astype(jnp.int32), axis=1)
    csum = jnp.cumsum(sel, axis=0)
    counts = csum[-1]
    rank = csum - sel
    padded = ((counts + bm - 1) // bm) * bm
    pad_end = jnp.cumsum(padded)
    pad_start = pad_end - padded
    pos = pad_start[idx] + jnp.take_along_axis(rank, idx, axis=1)
    block_start = jnp.arange(n_blocks, dtype=jnp.int32) * bm
    block_exp = jnp.minimum(
        jnp.sum((block_start[:, None] >= pad_end[None, :]).astype(jnp.int32), axis=1),
        N_EXPERTS - 1).astype(jnp.int32)
    n_valid = jnp.clip(pad_start[block_exp] + counts[block_exp] - block_start, 0, bm).astype(jnp.int32)
    n_used = (pad_end[-1] // bm).astype(jnp.int32).reshape(1)
    return pos.T.reshape(-1).astype(jnp.int32), block_exp, n_valid, n_used, n_rows


def _take_cols(w, cols):
    cols = np.asarray(cols, np.int32)
    out = jnp.take(w, jnp.asarray(np.maximum(cols, 0)), axis=-1)
    return jnp.where(jnp.asarray(cols >= 0), out, 0.0)


def _head_cols(offset, width, pad):
    cols = []
    for h in range(N_HEADS):
        cols += list(range(offset + h * width, offset + (h + 1) * width)) + [-1] * (pad - width)
    return cols


def _rope_head_cols(offset):
    half = DQK // 2
    slot = DQK_PAD // 2
    cols = []
    for h in range(N_HEADS):
        b = offset + h * DQK
        cols += list(range(b, b + half)) + [-1] * (slot - half)
        cols += list(range(b + half, b + DQK)) + [-1] * (slot - half)
    return cols


_ML_GATE_OFF = 2 * ML_QK + 2 * MIX_WIDTH
_ML_COLS = (_head_cols(0, DQK, DQK_PAD) + _head_cols(ML_QK, DQK, DQK_PAD)
            + _head_cols(2 * ML_QK, DV, DV_PAD) + _head_cols(2 * ML_QK + MIX_WIDTH, DV, DV_PAD)
            + list(range(_ML_GATE_OFF + 2 * N_HEADS, _ML_GATE_OFF + 2 * N_HEADS + XATTN_WIDTH))
            + list(range(_ML_GATE_OFF, _ML_GATE_OFF + 2 * N_HEADS)) + [-1] * (LANES - 2 * N_HEADS))
_RET_COLS = (_rope_head_cols(0) + _rope_head_cols(ML_QK)
             + _head_cols(2 * ML_QK, DV, DV_PAD) + _head_cols(2 * ML_QK + MIX_WIDTH, DV, DV_PAD)
             + list(range(2 * ML_QK + 2 * MIX_WIDTH, 2 * ML_QK + 2 * MIX_WIDTH + XATTN_WIDTH)))
_MIX_PAD_COLS = _head_cols(0, DV, DV_PAD)
_XQ_BLK_PADDED = (2 * HEADS_QK + 2 * HEADS_V) // XATTN_WIDTH
_XQ_BLK_S5 = MIX_WIDTH // XATTN_WIDTH


def kernel(x, mem, positions, mem_w_k, mem_w_v, l0_w_in, l0_s5_a_re, l0_s5_a_im, l0_s5_log_dt, l0_s5_b_re, l0_s5_b_im, l0_s5_c_re, l0_s5_c_im, l0_s5_d, l0_s5_w_glu, l0_s5_b_glu, l1_w_in, l1_ml_conv_q, l1_ml_conv_k, l1_ml_b_i, l1_ml_b_f, l1_ml_norm_g, l2_w_in, l2_ret_norm_g, l3_w_in, l3_s5_a_re, l3_s5_a_im, l3_s5_log_dt, l3_s5_b_re, l3_s5_b_im, l3_s5_c_re, l3_s5_c_im, l3_s5_d, l3_s5_w_glu, l3_s5_b_glu, w_out, ln1_g, ln1_b, ln2_g, ln2_b, router_w, router_b, exp_w_gu, exp_b_gu, exp_w_down, exp_b_down):
    bsz, seqlen, d = x.shape
    t = bsz * seqlen
    h = x.reshape(t, d)

    w_kv = jnp.concatenate([mem_w_k, mem_w_v], axis=1).astype(BF16)
    kv = _inproj(mem.reshape(bsz * N_MEM, d), w_kv).astype(BF16)
    mem_k = kv[:, :XATTN_WIDTH].reshape(bsz, N_MEM, XATTN_WIDTH)
    mem_v = kv[:, XATTN_WIDTH:].reshape(bsz, N_MEM, XATTN_WIDTH)

    half = DQK // 2
    inv = ROPE_BASE ** (-jnp.arange(0, DQK, 2, dtype=F32) / DQK)
    zpad = jnp.zeros((DQK_PAD // 2 - half,), F32)
    inv_pad = jnp.concatenate([inv, zpad, inv, zpad]).reshape(1, DQK_PAD)
    sgn_pad = jnp.concatenate([-jnp.ones((half,), F32), zpad, jnp.ones((half,), F32), zpad]
                              ).reshape(1, DQK_PAD)
    pos_f = positions.astype(F32).reshape(t, 1)

    s5_params = {
        0: (l0_s5_a_re, l0_s5_a_im, l0_s5_log_dt, l0_s5_b_re, l0_s5_b_im, l0_s5_c_re, l0_s5_c_im,
            l0_s5_d, l0_s5_w_glu, l0_s5_b_glu),
        3: (l3_s5_a_re, l3_s5_a_im, l3_s5_log_dt, l3_s5_b_re, l3_s5_b_im, l3_s5_c_re, l3_s5_c_im,
            l3_s5_d, l3_s5_w_glu, l3_s5_b_glu),
    }
    w_ins = (l0_w_in, l1_w_in, l2_w_in, l3_w_in)

    for i in range(DEPTH):
        kind = i % 3
        wo = w_out[i]
        if kind == 0:
            proj = _inproj(h, w_ins[i].astype(BF16))
            y_mix = _s5_mixer(proj, bsz, seqlen, *s5_params[i])
            wo_mix = wo[:MIX_WIDTH].astype(BF16)
            xq_blk = _XQ_BLK_S5
        elif kind == 1:
            proj = _inproj(h, _take_cols(w_ins[i], _ML_COLS).astype(BF16))
            cw = jnp.concatenate([_take_cols(l1_ml_conv_q, _head_cols(0, DQK, DQK_PAD)),
                                  _take_cols(l1_ml_conv_k, _head_cols(0, DQK, DQK_PAD))], axis=1)
            gbias = jnp.concatenate([l1_ml_b_i, l1_ml_b_f,
                                     jnp.zeros((LANES - 2 * N_HEADS,), F32)]).reshape(1, LANES)
            norm_g = _take_cols(l1_ml_norm_g, _MIX_PAD_COLS).reshape(1, HEADS_V)
            y_mix = _mlstm_mixer(proj, bsz, seqlen, cw, gbias, norm_g)
            wo_mix = _take_cols(wo[:MIX_WIDTH].T, _MIX_PAD_COLS).T.astype(BF16)
            xq_blk = _XQ_BLK_PADDED
        else:
            proj = _inproj(h, _take_cols(w_ins[i], _RET_COLS).astype(BF16))
            norm_g = _take_cols(l2_ret_norm_g, _MIX_PAD_COLS).reshape(1, HEADS_V)
            y_mix = _ret_mixer(proj, bsz, seqlen, pos_f, inv_pad, sgn_pad, norm_g)
            wo_mix = _take_cols(wo[:MIX_WIDTH].T, _MIX_PAD_COLS).T.astype(BF16)
            xq_blk = _XQ_BLK_PADDED
        wo_mem = wo[MIX_WIDTH:].astype(BF16)
        rw = jnp.pad(router_w[i], ((0, 0), (0, LANES - N_EXPERTS)))
        rb = jnp.concatenate([router_b[i], jnp.full((LANES - N_EXPERTS,), -1e30, F32)]
                             ).reshape(1, LANES)
        h1, idx, gates = _post_mixer(y_mix, proj, xq_blk, h, mem_k, mem_v, wo_mix, wo_mem,
                                     ln1_g[i].reshape(1, d), ln1_b[i].reshape(1, d), rw, rb, seqlen)
        pos_kmajor, block_exp, n_valid, n_used, n_rows = _route(idx[:, :TOP_K])
        xr = _sc_dispatch(h1, pos_kmajor, n_rows)
        yr = _experts(xr, block_exp, n_valid, n_used, i, exp_w_gu, exp_b_gu, exp_w_down, exp_b_down)
        yg = _sc_gather(yr, pos_kmajor).reshape(TOP_K, t, d)
        h = _combine(yg, h1, gates, ln2_g[i].reshape(1, d), ln2_b[i].reshape(1, d))
    return h.reshape(bsz, seqlen, d)
```

```python
import functools
import math

import numpy as np
import jax
import jax.numpy as jnp
from jax import lax
from jax.experimental import pallas as pl
from jax.experimental.pallas import tpu as pltpu
from jax.experimental.pallas import tpu_sc as plsc

F32 = jnp.float32
BF16 = jnp.bfloat16

D_MODEL = 1024
DEPTH = 4
N_MEM = 256
MIX_WIDTH = 768
XATTN_HEADS = 4
XATTN_WIDTH = 256
XATTN_HEAD_DIM = 64
S5_GROUP = 16
S5_GROUPS = 48
S5_STATE = 64
N_HEADS = 4
DQK = 96
DV = 192
ML_QK = 384
ML_CONV = 4
ML_CHUNK = 64
RET_CHUNK = 128
ROPE_BASE = 10000.0
N_EXPERTS = 32
TOP_K = 4
SWIGLU_LIMIT = 7.0
SWIGLU_ALPHA = 1.702
DEEPNORM_ALPHA = (2.0 * DEPTH) ** 0.25
LN_EPS = 1e-5

LANES = 128
SUBLANES = 8
DQK_PAD = 128
DV_PAD = 256
HEADS_QK = N_HEADS * DQK_PAD
HEADS_V = N_HEADS * DV_PAD
S5_LC = 16
S5_K = S5_LC * S5_GROUP
MOE_BM = 512
ROW_TILE = 256
VMEM_LIMIT = 56 * 1024 * 1024

_NT = (((1,), (1,)), ((), ()))
_TN = (((0,), (0,)), ((), ()))


def _cparams(sem):
    return pltpu.CompilerParams(dimension_semantics=sem, vmem_limit_bytes=VMEM_LIMIT)


def _dot(a, b):
    return jnp.dot(a, b, preferred_element_type=F32)


def _layer_norm_rows(z, g, b):
    mu = jnp.mean(z, axis=-1, keepdims=True)
    d = z - mu
    var = jnp.mean(d * d, axis=-1, keepdims=True)
    return d * lax.rsqrt(var + LN_EPS) * g + b


def _inproj_kernel(x_ref, w_ref, o_ref):
    xb = x_ref[...].astype(BF16)
    n = o_ref.shape[1]
    step = 512
    for c0 in range(0, n, step):
        c1 = min(c0 + step, n)
        o_ref[:, c0:c1] = _dot(xb, w_ref[:, c0:c1])


def _inproj(x, w_bf16, tm=ROW_TILE):
    t, d = x.shape
    n = w_bf16.shape[1]
    return pl.pallas_call(
        _inproj_kernel,
        out_shape=jax.ShapeDtypeStruct((t, n), F32),
        grid=(t // tm,),
        in_specs=[pl.BlockSpec((tm, d), lambda i: (i, 0)),
                  pl.BlockSpec((d, n), lambda i: (0, 0))],
        out_specs=pl.BlockSpec((tm, n), lambda i: (i, 0)),
        compiler_params=_cparams(("parallel",)),
        name="inproj",
    )(x, w_bf16)


S5_LANE_GROUPS = S5_K // S5_GROUP
S5_BLOCKS = MIX_WIDTH // S5_K
S5_SW = S5_LANE_GROUPS * S5_STATE
S5_LB = 512
S5_FOLD = 4


def _block_diag(m):
    nb, g, a, b = m.shape
    eye = jnp.eye(g, dtype=m.dtype)
    return (m[:, :, :, None, :] * eye[None, :, None, :, None]).reshape(nb, g * a, g * b)


def _s5_prep(a_re, a_im, log_dt, b_re, b_im, c_re, c_im):
    hp = lax.Precision.HIGHEST
    lam_re = jnp.minimum(a_re.astype(F32), -1e-4)
    lam_im = a_im.astype(F32)
    dt = jnp.exp(log_dt.astype(F32))[:, None]
    mag = jnp.exp(dt * lam_re)
    ab_re = mag * jnp.cos(dt * lam_im)
    ab_im = mag * jnp.sin(dt * lam_im)
    den = lam_re * lam_re + lam_im * lam_im
    num_re = ab_re - 1.0
    coef_re = (num_re * lam_re + ab_im * lam_im) / den
    coef_im = (ab_im * lam_re - num_re * lam_im) / den
    bre = b_re.astype(F32)
    bim = b_im.astype(F32)
    bb_re = coef_re[..., None] * bre - coef_im[..., None] * bim
    bb_im = coef_re[..., None] * bim + coef_im[..., None] * bre
    pr = [jnp.ones_like(ab_re)]
    pi = [jnp.zeros_like(ab_im)]
    for _ in range(S5_LC):
        r, i = pr[-1], pi[-1]
        pr.append(r * ab_re - i * ab_im)
        pi.append(r * ab_im + i * ab_re)
    pw_re = jnp.stack(pr)
    pw_im = jnp.stack(pi)
    p_re = pw_re[:S5_LC, :, :, None] * bb_re[None] - pw_im[:S5_LC, :, :, None] * bb_im[None]
    p_im = pw_re[:S5_LC, :, :, None] * bb_im[None] + pw_im[:S5_LC, :, :, None] * bb_re[None]
    cre = c_re.astype(F32)
    cim = c_im.astype(F32)
    kmat = (jnp.einsum('ghp,tgpk->tgkh', cre, p_re, precision=hp)
            - jnp.einsum('ghp,tgpk->tgkh', cim, p_im, precision=hp))
    nb, lg = S5_BLOCKS, S5_LANE_GROUPS
    toep = _block_diag(kmat.reshape(S5_LC * nb, lg, S5_GROUP, S5_GROUP)
                       ).reshape(S5_LC, nb, S5_K, S5_K).transpose(1, 0, 2, 3)
    def lane_blocks(m_re, m_im, rows_first):
        if rows_first:
            shp = (nb, lg, S5_GROUP, S5_STATE)
            return jnp.concatenate([_block_diag(m_re.transpose(0, 2, 1).reshape(shp)),
                                    _block_diag(m_im.transpose(0, 2, 1).reshape(shp))], axis=2)
        shp = (nb, lg, S5_STATE, S5_GROUP)
        return jnp.concatenate([_block_diag(m_re.reshape(shp)), _block_diag(m_im.reshape(shp))],
                               axis=1)

    bmat = jnp.stack([lane_blocks(p_re[S5_FOLD - 1 - r], p_im[S5_FOLD - 1 - r], True)
                      for r in range(S5_FOLD)], axis=1)
    crt = cre.transpose(0, 2, 1)
    cit = cim.transpose(0, 2, 1)
    cms = []
    for r in range(S5_FOLD):
        ar_ = pw_re[r + 1][:, :, None]
        ai_ = pw_im[r + 1][:, :, None]
        cms.append(lane_blocks(crt * ar_ - cit * ai_, -(crt * ai_ + cit * ar_), False))
    cmat = jnp.stack(cms, axis=1)
    lane = lambda v: v.reshape(nb, 1, S5_SW)
    avec = jnp.concatenate([lane(pw_re[S5_FOLD]), lane(pw_im[S5_FOLD]),
                            lane(pw_re[S5_LC]), lane(pw_im[S5_LC])], axis=1)
    return toep.astype(BF16), bmat.astype(BF16), cmat.astype(BF16), avec


def _s5_kernel(u_ref, t_ref, b_ref, c_ref, a_ref, y_ref, s_acc, x_prev, x_carry, u_half, y_half,
               *, lb):
    li = pl.program_id(1)
    bsz = u_ref.shape[0]
    ncb = lb // S5_LC
    m = bsz * ncb

    @pl.when(li == 0)
    def _():
        x_carry[...] = jnp.zeros_like(x_carry)

    ar = a_ref[0, 0:1, :]
    ai = a_ref[0, 1:2, :]
    alr = a_ref[0, 2:3, :]
    ali = a_ref[0, 3:4, :]

    def cmul(zr, zi, wr, wi):
        return wr * zr - wi * zi, wr * zi + wi * zr

    n_half = S5_K // LANES
    for hf in range(n_half):
        u_half[hf] = u_ref[:, :, hf * LANES:(hf + 1) * LANES].reshape(bsz * lb, LANES)
    xs = []
    for s in range(S5_LC):
        halves = [jnp.concatenate([u_half[hf, pl.ds(c * S5_LC + s, bsz, stride=lb), :]
                                   for c in range(ncb)], axis=0) for hf in range(n_half)]
        xs.append(jnp.concatenate(halves, axis=1).astype(BF16))
    sr = jnp.zeros((m, S5_SW), F32)
    si = jnp.zeros((m, S5_SW), F32)
    for q in range(S5_LC // S5_FOLD):
        bu = _dot(xs[q * S5_FOLD], b_ref[0, 0])
        for r in range(1, S5_FOLD):
            bu = bu + _dot(xs[q * S5_FOLD + r], b_ref[0, r])
        if q > 0:
            sr, si = cmul(sr, si, ar, ai)
        sr = sr + bu[:, :S5_SW]
        si = si + bu[:, S5_SW:]
    s_acc[:, :S5_SW] = sr
    s_acc[:, S5_SW:] = si

    xr = x_carry[:, :S5_SW]
    xi = x_carry[:, S5_SW:]
    for c in range(ncb):
        x_prev[c * bsz:(c + 1) * bsz, :S5_SW] = xr
        x_prev[c * bsz:(c + 1) * bsz, S5_SW:] = xi
        loc = s_acc[c * bsz:(c + 1) * bsz, :]
        xr, xi = cmul(xr, xi, alr, ali)
        xr = xr + loc[:, :S5_SW]
        xi = xi + loc[:, S5_SW:]
    x_carry[:, :S5_SW] = xr
    x_carry[:, S5_SW:] = xi

    zr = x_prev[:, :S5_SW]
    zi = x_prev[:, S5_SW:]
    for j in range(S5_LC):
        if j > 0 and j % S5_FOLD == 0:
            zr, zi = cmul(zr, zi, ar, ai)
        if j % S5_FOLD == 0:
            zb = jnp.concatenate([zr, zi], axis=1).astype(BF16)
        yj = _dot(zb, c_ref[0, j % S5_FOLD])
        for s in range(j + 1):
            yj = yj + _dot(xs[s], t_ref[0, j - s])
        for hf in range(n_half):
            for c in range(ncb):
                y_half[hf, pl.ds(c * S5_LC + j, bsz, stride=lb), :] = (
                    yj[c * bsz:(c + 1) * bsz, hf * LANES:(hf + 1) * LANES])
    for hf in range(n_half):
        y_ref[:, :, hf * LANES:(hf + 1) * LANES] = y_half[hf].reshape(bsz, lb, LANES)


def _s5_scan(proj3, mats, lb=S5_LB):
    toep, bmat, cmat, avec = mats
    bsz, seqlen, _ = proj3.shape
    return pl.pallas_call(
        functools.partial(_s5_kernel, lb=lb),
        out_shape=jax.ShapeDtypeStruct((bsz, seqlen, MIX_WIDTH), F32),
        grid=(S5_BLOCKS, seqlen // lb),
        in_specs=[pl.BlockSpec((bsz, lb, S5_K), lambda v, l: (0, l, v)),
                  pl.BlockSpec((1, S5_LC, S5_K, S5_K), lambda v, l: (v, 0, 0, 0)),
                  pl.BlockSpec((1, S5_FOLD, S5_K, 2 * S5_SW), lambda v, l: (v, 0, 0, 0)),
                  pl.BlockSpec((1, S5_FOLD, 2 * S5_SW, S5_K), lambda v, l: (v, 0, 0, 0)),
                  pl.BlockSpec((1, 4, S5_SW), lambda v, l: (v, 0, 0))],
        out_specs=pl.BlockSpec((bsz, lb, S5_K), lambda v, l: (0, l, v)),
        scratch_shapes=[pltpu.VMEM((bsz * (lb // S5_LC), 2 * S5_SW), F32),
                        pltpu.VMEM((bsz * (lb // S5_LC), 2 * S5_SW), F32),
                        pltpu.VMEM((bsz, 2 * S5_SW), F32),
                        pltpu.VMEM((S5_K // LANES, bsz * lb, LANES), F32),
                        pltpu.VMEM((S5_K // LANES, bsz * lb, LANES), F32)],
        compiler_params=_cparams(("parallel", "arbitrary")),
        name="s5_scan",
    )(proj3, toep, bmat, cmat, avec)


def _s5_post_kernel(y_ref, u_ref, d_ref, w_ref, b_ref, o_ref):
    y = y_ref[...] + d_ref[...] * u_ref[...]
    y = jax.nn.gelu(y)
    o_ref[...] = y * jax.nn.sigmoid(_dot(y.astype(BF16), w_ref[...]) + b_ref[...])


def _s5_post(y_ssm, proj, d_skip, w_glu_bf16, b_glu, tm=ROW_TILE):
    t = y_ssm.shape[0]
    w = MIX_WIDTH
    return pl.pallas_call(
        _s5_post_kernel,
        out_shape=jax.ShapeDtypeStruct((t, w), F32),
        grid=(t // tm,),
        in_specs=[pl.BlockSpec((tm, w), lambda i: (i, 0)),
                  pl.BlockSpec((tm, w), lambda i: (i, 0)),
                  pl.BlockSpec((1, w), lambda i: (0, 0)),
                  pl.BlockSpec((w, w), lambda i: (0, 0)),
                  pl.BlockSpec((1, w), lambda i: (0, 0))],
        out_specs=pl.BlockSpec((tm, w), lambda i: (i, 0)),
        compiler_params=_cparams(("parallel",)),
        name="s5_post",
    )(y_ssm, proj, d_skip.reshape(1, w), w_glu_bf16, b_glu.reshape(1, w))


def _s5_mixer(proj, bsz, seqlen, a_re, a_im, log_dt, b_re, b_im, c_re, c_im, d_skip, w_glu, b_glu):
    mats = _s5_prep(a_re, a_im, log_dt, b_re, b_im, c_re, c_im)
    y = _s5_scan(proj.reshape(bsz, seqlen, proj.shape[1]), mats)
    return _s5_post(y.reshape(bsz * seqlen, MIX_WIDTH), proj, d_skip, w_glu.astype(BF16), b_glu)


def _head_norm_padded(hv, g):
    lane = lax.broadcasted_iota(jnp.int32, hv.shape, 1)
    real = lane < DV
    mu = jnp.sum(hv, axis=-1, keepdims=True) * (1.0 / DV)
    d = jnp.where(real, hv - mu, 0.0)
    var = jnp.sum(d * d, axis=-1, keepdims=True) * (1.0 / DV)
    return d * lax.rsqrt(var + LN_EPS) * g


def _log_sigmoid(x):
    return jnp.minimum(x, 0.0) - jnp.log(1.0 + jnp.exp(-jnp.abs(x)))


def _mlstm_kernel(q_ref, k_ref, v_ref, o_ref, gt_ref, cw_ref, gb_ref, ng_ref,
                  y_ref, cbuf, c_st, n_st, m_st, *, tl):
    i = pl.program_id(1)

    @pl.when(i == 0)
    def _():
        cbuf[0:SUBLANES, :] = jnp.zeros((SUBLANES, 2 * HEADS_QK), F32)
        c_st[...] = jnp.zeros_like(c_st)
        n_st[...] = jnp.zeros_like(n_st)
        m_st[...] = jnp.zeros_like(m_st)

    cbuf[SUBLANES:SUBLANES + tl, 0:HEADS_QK] = q_ref[...]
    cbuf[SUBLANES:SUBLANES + tl, HEADS_QK:2 * HEADS_QK] = k_ref[...]
    acc = jnp.zeros((tl, 2 * HEADS_QK), F32)
    for w in range(ML_CONV):
        acc = acc + cbuf[pl.ds(SUBLANES - (ML_CONV - 1) + w, tl), :] * cw_ref[w:w + 1, :]
    qk = acc * jax.nn.sigmoid(acc)
    cbuf[0:SUBLANES, :] = cbuf[tl:tl + SUBLANES, :]

    gt = gt_ref[...] + gb_ref[...]
    lf = _log_sigmoid(gt)
    gt_t = gt.T
    lf_t = _log_sigmoid(gt_t)

    cl = ML_CHUNK
    row = lax.broadcasted_iota(jnp.int32, (cl, cl), 0)
    col = lax.broadcasted_iota(jnp.int32, (cl, cl), 1)
    tri = row >= col
    scale = DQK ** -0.5
    for cc in range(tl // cl):
        r0 = cc * cl
        for h in range(N_HEADS):
            ig_col = gt[r0:r0 + cl, h:h + 1]
            lf_col = lf[r0:r0 + cl, N_HEADS + h:N_HEADS + h + 1]
            ig_row = gt_t[h:h + 1, r0:r0 + cl]
            lf_row = lf_t[N_HEADS + h:N_HEADS + h + 1, r0:r0 + cl]
            bcum_col = jnp.sum(jnp.where(tri, lf_row, 0.0), axis=1, keepdims=True)
            bcum_row = jnp.sum(jnp.where(col >= row, lf_col, 0.0), axis=0, keepdims=True)
            btot = jnp.sum(lf_row, axis=1, keepdims=True)
            w_row = btot - bcum_row + ig_row
            m_loc = jnp.max(w_row, axis=1, keepdims=True)
            e_col = jnp.exp(btot - bcum_col + ig_col - m_loc)
            m_prev = m_st[h:h + 1, 0:1]
            c_prev = c_st[h]
            n_prev = n_st[h:h + 1, :]
            q = qk[r0:r0 + cl, h * DQK_PAD:(h + 1) * DQK_PAD] * scale
            k = qk[r0:r0 + cl, HEADS_QK + h * DQK_PAD:HEADS_QK + (h + 1) * DQK_PAD]
            v = v_ref[r0:r0 + cl, h * DV_PAD:(h + 1) * DV_PAD]
            qb = q.astype(BF16)
            kb = k.astype(BF16)
            vb = v.astype(BF16)
            dmat = jnp.where(tri, bcum_col - bcum_row + ig_row, -jnp.inf)
            g_col = bcum_col + m_prev
            m_row = jnp.maximum(g_col, jnp.max(dmat, axis=1, keepdims=True))
            inter = jnp.exp(g_col - m_row)
            s_qk = lax.dot_general(qb, kb, _NT, preferred_element_type=F32) * jnp.exp(dmat - m_row)
            num = inter * _dot(qb, c_prev.astype(BF16)) + _dot(s_qk.astype(BF16), vb)
            den = (inter * jnp.sum(q * n_prev, axis=1, keepdims=True)
                   + jnp.sum(s_qk, axis=1, keepdims=True))
            hv = num / jnp.maximum(jnp.abs(den), jnp.exp(-m_row))
            ke = k * e_col
            kv = lax.dot_general(ke.astype(BF16), vb, _TN, preferred_element_type=F32)
            nk = jnp.sum(ke, axis=0, keepdims=True)
            m_new = jnp.maximum(btot + m_prev, m_loc)
            sa = jnp.exp(btot + m_prev - m_new)
            sb = jnp.exp(m_loc - m_new)
            c_st[h] = sa * c_prev + sb * kv
            n_st[h:h + 1, :] = sa * n_prev + sb * nk
            m_st[h:h + 1, :] = jnp.broadcast_to(m_new, (1, LANES))
            hn = _head_norm_padded(hv, ng_ref[0:1, h * DV_PAD:(h + 1) * DV_PAD])
            og = o_ref[r0:r0 + cl, h * DV_PAD:(h + 1) * DV_PAD]
            y_ref[r0:r0 + cl, h * DV_PAD:(h + 1) * DV_PAD] = jax.nn.sigmoid(og) * hn


def _mlstm_mixer(proj, bsz, seqlen, cw, gbias, norm_g, tl=ROW_TILE):
    t = proj.shape[0]
    nl = seqlen // tl
    rows = lambda b, i: b * nl + i
    return pl.pallas_call(
        functools.partial(_mlstm_kernel, tl=tl),
        out_shape=jax.ShapeDtypeStruct((t, HEADS_V), F32),
        grid=(bsz, nl),
        in_specs=[pl.BlockSpec((tl, HEADS_QK), lambda b, i: (rows(b, i), 0)),
                  pl.BlockSpec((tl, HEADS_QK), lambda b, i: (rows(b, i), 1)),
                  pl.BlockSpec((tl, HEADS_V), lambda b, i: (rows(b, i), 1)),
                  pl.BlockSpec((tl, HEADS_V), lambda b, i: (rows(b, i), 2)),
                  pl.BlockSpec((tl, LANES), lambda b, i: (rows(b, i), 26)),
                  pl.BlockSpec((ML_CONV, 2 * HEADS_QK), lambda b, i: (0, 0)),
                  pl.BlockSpec((1, LANES), lambda b, i: (0, 0)),
                  pl.BlockSpec((1, HEADS_V), lambda b, i: (0, 0))],
        out_specs=pl.BlockSpec((tl, HEADS_V), lambda b, i: (rows(b, i), 0)),
        scratch_shapes=[pltpu.VMEM((tl + SUBLANES, 2 * HEADS_QK), F32),
                        pltpu.VMEM((N_HEADS, DQK_PAD, DV_PAD), F32),
                        pltpu.VMEM((SUBLANES, DQK_PAD), F32),
                        pltpu.VMEM((SUBLANES, LANES), F32)],
        compiler_params=_cparams(("parallel", "arbitrary")),
        name="mlstm",
    )(proj, proj, proj, proj, proj, cw, gbias, norm_g)


def _ret_log_gamma(h):
    return float(np.log(np.float32(1.0) - np.power(np.float32(2.0), np.float32(-5.0 - h))))


def _ret_kernel(q_ref, k_ref, v_ref, g_ref, pos_ref, inv_ref, sgn_ref, ng_ref,
                y_ref, s_st, *, tl):
    i = pl.program_id(1)

    @pl.when(i == 0)
    def _():
        s_st[...] = jnp.zeros_like(s_st)

    ang = pos_ref[...] * inv_ref[...]
    cos_t = jnp.cos(ang)
    sin_t = jnp.sin(ang) * sgn_ref[...]
    cl = RET_CHUNK
    row = lax.broadcasted_iota(jnp.int32, (cl, cl), 0)
    col = lax.broadcasted_iota(jnp.int32, (cl, cl), 1)
    rel = (row - col).astype(F32)
    jcol = lax.broadcasted_iota(jnp.int32, (cl, 1), 0).astype(F32)
    kscale = DQK ** -0.5
    for h in range(N_HEADS):
        lg = _ret_log_gamma(h)
        decay = jnp.where(rel >= 0, jnp.exp(jnp.maximum(rel, 0.0) * lg), 0.0)
        zeta = jnp.exp((cl - 1 - jcol) * lg)
        xi = jnp.exp((jcol + 1.0) * lg)
        chunk_decay = float(np.exp(np.float32(cl) * np.float32(lg)))
        qh = q_ref[:, h * DQK_PAD:(h + 1) * DQK_PAD]
        kh = k_ref[:, h * DQK_PAD:(h + 1) * DQK_PAD]
        qh = qh * cos_t + pltpu.roll(qh, DQK_PAD // 2, 1) * sin_t
        kh = (kh * cos_t + pltpu.roll(kh, DQK_PAD // 2, 1) * sin_t) * kscale
        for cc in range(tl // cl):
            r0 = cc * cl
            qb = qh[r0:r0 + cl].astype(BF16)
            k = kh[r0:r0 + cl]
            kb = k.astype(BF16)
            v = v_ref[r0:r0 + cl, h * DV_PAD:(h + 1) * DV_PAD]
            vb = v.astype(BF16)
            s_prev = s_st[h]
            s = lax.dot_general(qb, kb, _NT, preferred_element_type=F32) * decay
            intra = _dot(s.astype(BF16), vb)
            cross = _dot(qb, s_prev.astype(BF16)) * xi
            r = lax.dot_general((k * zeta).astype(BF16), vb, _TN, preferred_element_type=F32)
            s_st[h] = chunk_decay * s_prev + r
            hn = _head_norm_padded(intra + cross, ng_ref[0:1, h * DV_PAD:(h + 1) * DV_PAD])
            gate = g_ref[r0:r0 + cl, h * DV_PAD:(h + 1) * DV_PAD]
            y_ref[r0:r0 + cl, h * DV_PAD:(h + 1) * DV_PAD] = gate * jax.nn.sigmoid(gate) * hn


def _ret_mixer(proj, bsz, seqlen, pos_f, inv_pad, sgn_pad, norm_g, tl=ROW_TILE):
    t = proj.shape[0]
    nl = seqlen // tl
    rows = lambda b, i: b * nl + i
    return pl.pallas_call(
        functools.partial(_ret_kernel, tl=tl),
        out_shape=jax.ShapeDtypeStruct((t, HEADS_V), F32),
        grid=(bsz, nl),
        in_specs=[pl.BlockSpec((tl, HEADS_QK), lambda b, i: (rows(b, i), 0)),
                  pl.BlockSpec((tl, HEADS_QK), lambda b, i: (rows(b, i), 1)),
                  pl.BlockSpec((tl, HEADS_V), lambda b, i: (rows(b, i), 1)),
                  pl.BlockSpec((tl, HEADS_V), lambda b, i: (rows(b, i), 2)),
                  pl.BlockSpec((tl, 1), lambda b, i: (rows(b, i), 0)),
                  pl.BlockSpec((1, DQK_PAD), lambda b, i: (0, 0)),
                  pl.BlockSpec((1, DQK_PAD), lambda b, i: (0, 0)),
                  pl.BlockSpec((1, HEADS_V), lambda b, i: (0, 0))],
        out_specs=pl.BlockSpec((tl, HEADS_V), lambda b, i: (rows(b, i), 0)),
        scratch_shapes=[pltpu.VMEM((N_HEADS, DQK_PAD, DV_PAD), F32)],
        compiler_params=_cparams(("parallel", "arbitrary")),
        name="retention",
    )(proj, proj, proj, proj, pos_f, inv_pad, sgn_pad, norm_g)


def _pack_rows(x):
    half = x.shape[1] // 2
    lo = pltpu.bitcast(x[:, :half].astype(BF16).astype(F32), jnp.uint32)
    hi = pltpu.bitcast(x[:, half:].astype(BF16).astype(F32), jnp.uint32)
    return hi | (lo >> 16)


def _unpack_rows(u):
    lo = pltpu.bitcast(u << 16, F32)
    hi = pltpu.bitcast(u & jnp.uint32(0xFFFF0000), F32)
    return jnp.concatenate([lo, hi], axis=1)


def _post_kernel(ym_ref, xq_ref, h_ref, mk_ref, mv_ref, wom_ref, wox_ref, g_ref, b_ref,
                 rwh_ref, rwl_ref, rb_ref, h1_ref, h1p_ref, idx_ref, gate_ref):
    tl = xq_ref.shape[0]
    xq = xq_ref[...] * (XATTN_HEAD_DIM ** -0.5)
    lane = lax.broadcasted_iota(jnp.int32, (tl, XATTN_WIDTH), 1)
    head = lane // XATTN_HEAD_DIM
    mk = mk_ref[0]
    mv = mv_ref[0]
    ymem = jnp.zeros((tl, XATTN_WIDTH), F32)
    for hh in range(XATTN_HEADS):
        sel = head == hh
        qh = jnp.where(sel, xq, 0.0).astype(BF16)
        s = lax.dot_general(qh, mk, _NT, preferred_element_type=F32)
        s = s - jnp.max(s, axis=-1, keepdims=True)
        p = jnp.exp(s)
        p = p / jnp.sum(p, axis=-1, keepdims=True)
        ymem = jnp.where(sel, _dot(p.astype(BF16), mv), ymem)
    y = _dot(ym_ref[...].astype(BF16), wom_ref[...]) + _dot(ymem.astype(BF16), wox_ref[...])
    h1 = _layer_norm_rows(DEEPNORM_ALPHA * h_ref[...] + y, g_ref[...], b_ref[...])
    h1_ref[...] = h1
    h1p_ref[...] = _pack_rows(h1)
    h_hi = h1.astype(BF16)
    h_lo = (h1 - h_hi.astype(F32)).astype(BF16)
    logits = (_dot(h_hi, rwh_ref[...]) + _dot(h_lo, rwh_ref[...]) + _dot(h_hi, rwl_ref[...])
              + rb_ref[...])
    ln = lax.broadcasted_iota(jnp.int32, logits.shape, 1)
    vals = logits
    tv, ti = [], []
    for _ in range(TOP_K):
        m = jnp.max(vals, axis=-1, keepdims=True)
        ix = jnp.min(jnp.where(vals == m, ln, LANES), axis=-1, keepdims=True)
        tv.append(m)
        ti.append(ix)
        vals = jnp.where(ln == ix, -jnp.inf, vals)
    ex = [jnp.exp(v - tv[0]) for v in tv]
    tot = ex[0] + ex[1] + ex[2] + ex[3]
    idx_out = jnp.zeros(logits.shape, jnp.int32)
    gate_out = jnp.zeros(logits.shape, F32)
    for k in range(TOP_K):
        idx_out = jnp.where(ln == k, ti[k], idx_out)
        gate_out = jnp.where(ln == k, ex[k] / tot, gate_out)
    idx_ref[...] = idx_out
    gate_ref[...] = gate_out


def _post_mixer(y_mix, proj, xq_blk, h, mem_k, mem_v, wo_mix, wo_mem, ln_g, ln_b, rw, rb,
                seqlen, tl=4 * ROW_TILE):
    t, cm = y_mix.shape
    nl = seqlen // tl
    d = D_MODEL
    rw_hi = rw.astype(BF16)
    rw_lo = (rw - rw_hi.astype(F32)).astype(BF16)
    full = lambda a, b: pl.BlockSpec((a, b), lambda i: (0, 0))
    return pl.pallas_call(
        _post_kernel,
        out_shape=(jax.ShapeDtypeStruct((t, d), F32),
                   jax.ShapeDtypeStruct((t, d // 2), jnp.uint32),
                   jax.ShapeDtypeStruct((t, LANES), jnp.int32),
                   jax.ShapeDtypeStruct((t, LANES), F32)),
        grid=(t // tl,),
        in_specs=[pl.BlockSpec((tl, cm), lambda i: (i, 0)),
                  pl.BlockSpec((tl, XATTN_WIDTH), lambda i: (i, xq_blk)),
                  pl.BlockSpec((tl, d), lambda i: (i, 0)),
                  pl.BlockSpec((1, N_MEM, XATTN_WIDTH), lambda i: (i // nl, 0, 0)),
                  pl.BlockSpec((1, N_MEM, XATTN_WIDTH), lambda i: (i // nl, 0, 0)),
                  full(cm, d), full(XATTN_WIDTH, d), full(1, d), full(1, d),
                  full(d, LANES), full(d, LANES), full(1, LANES)],
        out_specs=(pl.BlockSpec((tl, d), lambda i: (i, 0)),
                   pl.BlockSpec((tl, d // 2), lambda i: (i, 0)),
                   pl.BlockSpec((tl, LANES), lambda i: (i, 0)),
                   pl.BlockSpec((tl, LANES), lambda i: (i, 0))),
        compiler_params=_cparams(("parallel",)),
        name="post_mixer",
    )(y_mix, proj, h, mem_k, mem_v, wo_mix, wo_mem, ln_g, ln_b, rw_hi, rw_lo, rb)


SC_CORES = 2
SC_SUBCORES = 16
SC_WORKERS = SC_CORES * SC_SUBCORES
SC_CHUNK = 64


def _sc_gather(table, idx):
    v, d = table.shape
    b = idx.shape[0]
    per_w = b // SC_WORKERS
    n_chunks = per_w // SC_CHUNK
    assert per_w * SC_WORKERS == b and n_chunks * SC_CHUNK == per_w and n_chunks % 2 == 0
    mesh = plsc.VectorSubcoreMesh(core_axis_name="c", subcore_axis_name="s")

    @functools.partial(
        pl.kernel, mesh=mesh,
        out_type=jax.ShapeDtypeStruct((b, d), table.dtype),
        scratch_types=[pltpu.VMEM((per_w,), jnp.int32),
                       pltpu.VMEM((SC_CHUNK, d), table.dtype),
                       pltpu.VMEM((SC_CHUNK, d), table.dtype),
                       pltpu.SemaphoreType.DMA,
                       pltpu.SemaphoreType.DMA],
    )
    def gather_kernel(table_hbm, idx_hbm, out_hbm, idx_v, rows0, rows1, sem0, sem1):
        wid = lax.axis_index("s") * SC_CORES + lax.axis_index("c")
        base = wid * per_w
        pltpu.sync_copy(idx_hbm.at[pl.ds(pl.multiple_of(base, 8), per_w)], idx_v)
        ring = ((rows0, sem0), (rows1, sem1))

        def gather(c, buf, sem):
            rows = idx_v.at[pl.ds(pl.multiple_of(c * SC_CHUNK, 8), SC_CHUNK)]
            return pltpu.make_async_copy(table_hbm.at[rows], buf, sem)

        for c0, (buf, sem) in enumerate(ring):
            gather(c0, buf, sem).start()

        @pl.loop(0, n_chunks, step=2)
        def _(c):
            for k, (buf, sem) in enumerate(ring):
                cc = c + k
                gather(cc, buf, sem).wait()
                off = pl.multiple_of(base + cc * SC_CHUNK, 8)
                pltpu.sync_copy(buf, out_hbm.at[pl.ds(off, SC_CHUNK)])

                @pl.when(cc + 2 < n_chunks)
                def _():
                    gather(cc + 2, buf, sem).start()

    return gather_kernel(table, idx)


def _sc_dispatch(x, pos_kmajor, n_rows):
    t, d = x.shape
    per_w = t // SC_WORKERS
    n_chunks = per_w // SC_CHUNK
    assert per_w * SC_WORKERS == t and n_chunks * SC_CHUNK == per_w
    mesh = plsc.VectorSubcoreMesh(core_axis_name="c", subcore_axis_name="s")

    @functools.partial(
        pl.kernel, mesh=mesh,
        out_type=jax.ShapeDtypeStruct((n_rows, d), x.dtype),
        scratch_types=[pltpu.VMEM((SC_CHUNK,), jnp.int32),
                       pltpu.VMEM((SC_CHUNK, d), x.dtype)],
    )
    def dispatch_kernel(x_hbm, pos_hbm, out_hbm, idx_v, rows_v):
        wid = lax.axis_index("s") * SC_CORES + lax.axis_index("c")
        base = wid * per_w

        @pl.loop(0, n_chunks)
        def _(j):
            off = pl.multiple_of(base + j * SC_CHUNK, 8)
            pltpu.sync_copy(x_hbm.at[pl.ds(off, SC_CHUNK)], rows_v)
            for k in range(TOP_K):
                pltpu.sync_copy(pos_hbm.at[pl.ds(pl.multiple_of(k * t + off, 8), SC_CHUNK)], idx_v)
                pltpu.sync_copy(rows_v, out_hbm.at[idx_v])

    return dispatch_kernel(x, pos_kmajor)


def _expert_kernel(bexp_ref, nvalid_ref, nused_ref, x_ref, wgu_ref, bgu_ref, wd_ref, bd_ref, y_ref,
                   wgu_bf, wd_bf):
    i = pl.program_id(0)
    de = wd_ref.shape[2]

    @pl.when(i < nused_ref[0])
    def _():
        prev = bexp_ref[jnp.maximum(i - 1, 0)]

        @pl.when((i == 0) | (prev != bexp_ref[i]))
        def _():
            wgu_bf[...] = wgu_ref[0, 0].astype(BF16)
            wd_bf[...] = wd_ref[0, 0].astype(BF16)

        rows = lax.broadcasted_iota(jnp.int32, x_ref.shape, 0)
        xb = _unpack_rows(jnp.where(rows < nvalid_ref[i], x_ref[...], jnp.uint32(0))).astype(BF16)
        gu = _dot(xb, wgu_bf[...]) + bgu_ref[0, 0]
        x_glu = jnp.minimum(gu[:, :de], SWIGLU_LIMIT)
        x_lin = jnp.clip(gu[:, de:], -SWIGLU_LIMIT, SWIGLU_LIMIT)
        act = x_glu * jax.nn.sigmoid(SWIGLU_ALPHA * x_glu) * (x_lin + 1.0)
        y_ref[...] = _pack_rows(_dot(act.astype(BF16), wd_bf[...]) + bd_ref[0, 0])

    @pl.when(i >= nused_ref[0])
    def _():
        y_ref[...] = jnp.zeros_like(y_ref)


def _experts(xr, block_exp, n_valid, n_used, layer, w_gu, b_gu, w_down, b_down, bm=MOE_BM):
    n_rows, dp = xr.shape
    d = 2 * dp
    n_blocks = n_rows // bm
    nl, ne, _, de2 = w_gu.shape
    de = de2 // 2
    row_blk = lambda i, nu: jnp.minimum(i, nu[0] - 1)
    grid_spec = pltpu.PrefetchScalarGridSpec(
        num_scalar_prefetch=3,
        grid=(n_blocks,),
        in_specs=[pl.BlockSpec((bm, dp), lambda i, be, nv, nu: (row_blk(i, nu), 0)),
                  pl.BlockSpec((1, 1, d, de2), lambda i, be, nv, nu: (layer, be[i], 0, 0)),
                  pl.BlockSpec((1, 1, 1, de2), lambda i, be, nv, nu: (layer, be[i], 0, 0)),
                  pl.BlockSpec((1, 1, de, d), lambda i, be, nv, nu: (layer, be[i], 0, 0)),
                  pl.BlockSpec((1, 1, 1, d), lambda i, be, nv, nu: (layer, be[i], 0, 0))],
        out_specs=pl.BlockSpec((bm, dp), lambda i, be, nv, nu: (i, 0)),
        scratch_shapes=[pltpu.VMEM((d, de2), BF16),
                        pltpu.VMEM((de, d), BF16)],
    )
    return pl.pallas_call(
        _expert_kernel,
        out_shape=jax.ShapeDtypeStruct((n_rows, dp), jnp.uint32),
        grid_spec=grid_spec,
        compiler_params=_cparams(("arbitrary",)),
        name="experts",
    )(block_exp, n_valid, n_used, xr, w_gu, b_gu.reshape(nl, ne, 1, de2), w_down,
      b_down.reshape(nl, ne, 1, d))


def _combine_kernel(y0_ref, y1_ref, y2_ref, y3_ref, h1_ref, gate_ref, g_ref, b_ref, o_ref):
    gate = gate_ref[...]
    acc = DEEPNORM_ALPHA * h1_ref[...]
    for k, y_ref in enumerate((y0_ref, y1_ref, y2_ref, y3_ref)):
        acc = acc + gate[:, k:k + 1] * _unpack_rows(y_ref[0])
    o_ref[...] = _layer_norm_rows(acc, g_ref[...], b_ref[...])


def _combine(yg, h1, gates, ln_g, ln_b, tl=ROW_TILE):
    t, d = h1.shape
    ysel = lambda k: pl.BlockSpec((1, tl, d // 2), lambda i: (k, i, 0))
    return pl.pallas_call(
        _combine_kernel,
        out_shape=jax.ShapeDtypeStruct((t, d), F32),
        grid=(t // tl,),
        in_specs=[ysel(0), ysel(1), ysel(2), ysel(3),
                  pl.BlockSpec((tl, d), lambda i: (i, 0)),
                  pl.BlockSpec((tl, LANES), lambda i: (i, 0)),
                  pl.BlockSpec((1, d), lambda i: (0, 0)),
                  pl.BlockSpec((1, d), lambda i: (0, 0))],
        out_specs=pl.BlockSpec((tl, d), lambda i: (i, 0)),
        compiler_params=_cparams(("parallel",)),
        name="combine",
    )(yg, yg, yg, yg, h1, gates, ln_g, ln_b)


def _route(idx, bm=MOE_BM):
    t = idx.shape[0]
    n_assign = t * TOP_K
    n_rows = (-(-n_assign // bm) + N_EXPERTS) * bm
    n_blocks = n_rows // bm
    onehot = (idx[:, :, None] == jnp.arange(N_EXPERTS, dtype=jnp.int32)[None, None, :])
    sel = jnp.sum(onehot.astype(jnp.int32), axis=1)
    csum = jnp.cumsum(sel, axis=0)
    counts = csum[-1]
    rank = csum - sel
    padded = ((counts + bm - 1) // bm) * bm
    pad_end = jnp.cumsum(padded)
    pad_start = pad_end - padded
    pos = pad_start[idx] + jnp.take_along_axis(rank, idx, axis=1)
    block_start = jnp.arange(n_blocks, dtype=jnp.int32) * bm
    block_exp = jnp.minimum(
        jnp.sum((block_start[:, None] >= pad_end[None, :]).astype(jnp.int32), axis=1),
        N_EXPERTS - 1).astype(jnp.int32)
    n_valid = jnp.clip(pad_start[block_exp] + counts[block_exp] - block_start, 0, bm).astype(jnp.int32)
    n_used = (pad_end[-1] // bm).astype(jnp.int32).reshape(1)
    return pos.T.reshape(-1).astype(jnp.int32), block_exp, n_valid, n_used, n_rows


def _take_cols(w, cols):
    cols = np.asarray(cols, np.int32)
    out = jnp.take(w, jnp.asarray(np.maximum(cols, 0)), axis=-1)
    return jnp.where(jnp.asarray(cols >= 0), out, 0.0)


def _head_cols(offset, width, pad):
    cols = []
    for h in range(N_HEADS):
        cols += list(range(offset + h * width, offset + (h + 1) * width)) + [-1] * (pad - width)
    return cols


def _rope_head_cols(offset):
    half = DQK // 2
    slot = DQK_PAD // 2
    cols = []
    for h in range(N_HEADS):
        b = offset + h * DQK
        cols += list(range(b, b + half)) + [-1] * (slot - half)
        cols += list(range(b + half, b + DQK)) + [-1] * (slot - half)
    return cols


_ML_GATE_OFF = 2 * ML_QK + 2 * MIX_WIDTH
_ML_COLS = (_head_cols(0, DQK, DQK_PAD) + _head_cols(ML_QK, DQK, DQK_PAD)
            + _head_cols(2 * ML_QK, DV, DV_PAD) + _head_cols(2 * ML_QK + MIX_WIDTH, DV, DV_PAD)
            + list(range(_ML_GATE_OFF + 2 * N_HEADS, _ML_GATE_OFF + 2 * N_HEADS + XATTN_WIDTH))
            + list(range(_ML_GATE_OFF, _ML_GATE_OFF + 2 * N_HEADS)) + [-1] * (LANES - 2 * N_HEADS))
_RET_COLS = (_rope_head_cols(0) + _rope_head_cols(ML_QK)
             + _head_cols(2 * ML_QK, DV, DV_PAD) + _head_cols(2 * ML_QK + MIX_WIDTH, DV, DV_PAD)
             + list(range(2 * ML_QK + 2 * MIX_WIDTH, 2 * ML_QK + 2 * MIX_WIDTH + XATTN_WIDTH)))
_MIX_PAD_COLS = _head_cols(0, DV, DV_PAD)
_XQ_BLK_PADDED = (2 * HEADS_QK + 2 * HEADS_V) // XATTN_WIDTH
_XQ_BLK_S5 = MIX_WIDTH // XATTN_WIDTH


def kernel(x, mem, positions, mem_w_k, mem_w_v, l0_w_in, l0_s5_a_re, l0_s5_a_im, l0_s5_log_dt, l0_s5_b_re, l0_s5_b_im, l0_s5_c_re, l0_s5_c_im, l0_s5_d, l0_s5_w_glu, l0_s5_b_glu, l1_w_in, l1_ml_conv_q, l1_ml_conv_k, l1_ml_b_i, l1_ml_b_f, l1_ml_norm_g, l2_w_in, l2_ret_norm_g, l3_w_in, l3_s5_a_re, l3_s5_a_im, l3_s5_log_dt, l3_s5_b_re, l3_s5_b_im, l3_s5_c_re, l3_s5_c_im, l3_s5_d, l3_s5_w_glu, l3_s5_b_glu, w_out, ln1_g, ln1_b, ln2_g, ln2_b, router_w, router_b, exp_w_gu, exp_b_gu, exp_w_down, exp_b_down):
    bsz, seqlen, d = x.shape
    t = bsz * seqlen
    h = x.reshape(t, d)

    w_kv = jnp.concatenate([mem_w_k, mem_w_v], axis=1).astype(BF16)
    kv = _inproj(mem.reshape(bsz * N_MEM, d), w_kv).astype(BF16)
    mem_k = kv[:, :XATTN_WIDTH].reshape(bsz, N_MEM, XATTN_WIDTH)
    mem_v = kv[:, XATTN_WIDTH:].reshape(bsz, N_MEM, XATTN_WIDTH)

    half = DQK // 2
    inv = ROPE_BASE ** (-jnp.arange(0, DQK, 2, dtype=F32) / DQK)
    zpad = jnp.zeros((DQK_PAD // 2 - half,), F32)
    inv_pad = jnp.concatenate([inv, zpad, inv, zpad]).reshape(1, DQK_PAD)
    sgn_pad = jnp.concatenate([-jnp.ones((half,), F32), zpad, jnp.ones((half,), F32), zpad]
                              ).reshape(1, DQK_PAD)
    pos_f = positions.astype(F32).reshape(t, 1)

    s5_params = {
        0: (l0_s5_a_re, l0_s5_a_im, l0_s5_log_dt, l0_s5_b_re, l0_s5_b_im, l0_s5_c_re, l0_s5_c_im,
            l0_s5_d, l0_s5_w_glu, l0_s5_b_glu),
        3: (l3_s5_a_re, l3_s5_a_im, l3_s5_log_dt, l3_s5_b_re, l3_s5_b_im, l3_s5_c_re, l3_s5_c_im,
            l3_s5_d, l3_s5_w_glu, l3_s5_b_glu),
    }
    w_ins = (l0_w_in, l1_w_in, l2_w_in, l3_w_in)

    for i in range(DEPTH):
        kind = i % 3
        wo = w_out[i]
        if kind == 0:
            proj = _inproj(h, w_ins[i].astype(BF16))
            y_mix = _s5_mixer(proj, bsz, seqlen, *s5_params[i])
            wo_mix = wo[:MIX_WIDTH].astype(BF16)
            xq_blk = _XQ_BLK_S5
        elif kind == 1:
            proj = _inproj(h, _take_cols(w_ins[i], _ML_COLS).astype(BF16))
            cw = jnp.concatenate([_take_cols(l1_ml_conv_q, _head_cols(0, DQK, DQK_PAD)),
                                  _take_cols(l1_ml_conv_k, _head_cols(0, DQK, DQK_PAD))], axis=1)
            gbias = jnp.concatenate([l1_ml_b_i, l1_ml_b_f,
                                     jnp.zeros((LANES - 2 * N_HEADS,), F32)]).reshape(1, LANES)
            norm_g = _take_cols(l1_ml_norm_g, _MIX_PAD_COLS).reshape(1, HEADS_V)
            y_mix = _mlstm_mixer(proj, bsz, seqlen, cw, gbias, norm_g)
            wo_mix = _take_cols(wo[:MIX_WIDTH].T, _MIX_PAD_COLS).T.astype(BF16)
            xq_blk = _XQ_BLK_PADDED
        else:
            proj = _inproj(h, _take_cols(w_ins[i], _RET_COLS).astype(BF16))
            norm_g = _take_cols(l2_ret_norm_g, _MIX_PAD_COLS).reshape(1, HEADS_V)
            y_mix = _ret_mixer(proj, bsz, seqlen, pos_f, inv_pad, sgn_pad, norm_g)
            wo_mix = _take_cols(wo[:MIX_WIDTH].T, _MIX_PAD_COLS).T.astype(BF16)
            xq_blk = _XQ_BLK_PADDED
        wo_mem = wo[MIX_WIDTH:].astype(BF16)
        rw = jnp.pad(router_w[i], ((0, 0), (0, LANES - N_EXPERTS)))
        rb = jnp.concatenate([router_b[i], jnp.full((LANES - N_EXPERTS,), -1e30, F32)]
                             ).reshape(1, LANES)
        h1, h1p, idx, gates = _post_mixer(y_mix, proj, xq_blk, h, mem_k, mem_v, wo_mix, wo_mem,
                                     ln1_g[i].reshape(1, d), ln1_b[i].reshape(1, d), rw, rb, seqlen)
        pos_kmajor, block_exp, n_valid, n_used, n_rows = _route(idx[:, :TOP_K])
        xr = _sc_dispatch(h1p, pos_kmajor, n_rows)
        yr = _experts(xr, block_exp, n_valid, n_used, i, exp_w_gu, exp_b_gu, exp_w_down, exp_b_down)
        yg = _sc_gather(yr, pos_kmajor).reshape(TOP_K, t, d // 2)
        h = _combine(yg, h1, gates, ln2_g[i].reshape(1, d), ln2_b[i].reshape(1, d))
    return h.reshape(bsz, seqlen, d)
```

```python
import functools

import numpy as np
import jax
import jax.numpy as jnp
from jax import lax
from jax.experimental import pallas as pl
from jax.experimental.pallas import tpu as pltpu
from jax.experimental.pallas import tpu_sc as plsc

F32 = jnp.float32
BF16 = jnp.bfloat16

D_MODEL = 1024
DEPTH = 4
N_MEM = 256
MIX_WIDTH = 768
XATTN_HEADS = 4
XATTN_WIDTH = 256
XATTN_HEAD_DIM = 64
S5_GROUP = 16
S5_GROUPS = 48
S5_STATE = 64
N_HEADS = 4
DQK = 96
DV = 192
ML_QK = 384
ML_CONV = 4
ML_CHUNK = 64
RET_CHUNK = 128
ROPE_BASE = 10000.0
N_EXPERTS = 32
TOP_K = 4
SWIGLU_LIMIT = 7.0
SWIGLU_ALPHA = 1.702
DEEPNORM_ALPHA = (2.0 * DEPTH) ** 0.25
LN_EPS = 1e-5

LANES = 128
SUBLANES = 8
DQK_PAD = 128
DV_PAD = 256
HEADS_QK = N_HEADS * DQK_PAD
HEADS_V = N_HEADS * DV_PAD
S5_LC = 16
S5_K = S5_LC * S5_GROUP
MOE_BM = 512
ROW_TILE = 256
VMEM_LIMIT = 56 * 1024 * 1024

_NT = (((1,), (1,)), ((), ()))
_TN = (((0,), (0,)), ((), ()))


def _cparams(sem):
    return pltpu.CompilerParams(dimension_semantics=sem, vmem_limit_bytes=VMEM_LIMIT)


def _dot(a, b):
    return jnp.dot(a, b, preferred_element_type=F32)


def _layer_norm_rows(z, g, b):
    mu = jnp.mean(z, axis=-1, keepdims=True)
    d = z - mu
    var = jnp.mean(d * d, axis=-1, keepdims=True)
    return d * lax.rsqrt(var + LN_EPS) * g + b


def _inproj_kernel(x_ref, w_ref, o_ref):
    xb = x_ref[...].astype(BF16)
    n = o_ref.shape[1]
    step = 512
    for c0 in range(0, n, step):
        c1 = min(c0 + step, n)
        o_ref[:, c0:c1] = _dot(xb, w_ref[:, c0:c1])


def _inproj(x, w_bf16, tm=ROW_TILE):
    t, d = x.shape
    n = w_bf16.shape[1]
    return pl.pallas_call(
        _inproj_kernel,
        out_shape=jax.ShapeDtypeStruct((t, n), F32),
        grid=(t // tm,),
        in_specs=[pl.BlockSpec((tm, d), lambda i: (i, 0)),
                  pl.BlockSpec((d, n), lambda i: (0, 0))],
        out_specs=pl.BlockSpec((tm, n), lambda i: (i, 0)),
        compiler_params=_cparams(("parallel",)),
        name="inproj",
    )(x, w_bf16)


S5_LANE_GROUPS = S5_K // S5_GROUP
S5_BLOCKS = MIX_WIDTH // S5_K
S5_SW = S5_LANE_GROUPS * S5_STATE
S5_LB = 512


def _block_diag(m):
    nb, g, a, b = m.shape
    eye = jnp.eye(g, dtype=m.dtype)
    return (m[:, :, :, None, :] * eye[None, :, None, :, None]).reshape(nb, g * a, g * b)


def _s5_prep(a_re, a_im, log_dt, b_re, b_im, c_re, c_im):
    hp = lax.Precision.HIGHEST
    lam_re = jnp.minimum(a_re.astype(F32), -1e-4)
    lam_im = a_im.astype(F32)
    dt = jnp.exp(log_dt.astype(F32))[:, None]
    mag = jnp.exp(dt * lam_re)
    ab_re = mag * jnp.cos(dt * lam_im)
    ab_im = mag * jnp.sin(dt * lam_im)
    den = lam_re * lam_re + lam_im * lam_im
    num_re = ab_re - 1.0
    coef_re = (num_re * lam_re + ab_im * lam_im) / den
    coef_im = (ab_im * lam_re - num_re * lam_im) / den
    bre = b_re.astype(F32)
    bim = b_im.astype(F32)
    bb_re = coef_re[..., None] * bre - coef_im[..., None] * bim
    bb_im = coef_re[..., None] * bim + coef_im[..., None] * bre
    pr = [jnp.ones_like(ab_re)]
    pi = [jnp.zeros_like(ab_im)]
    for _ in range(S5_LC):
        r, i = pr[-1], pi[-1]
        pr.append(r * ab_re - i * ab_im)
        pi.append(r * ab_im + i * ab_re)
    pw_re = jnp.stack(pr)
    pw_im = jnp.stack(pi)
    p_re = pw_re[:S5_LC, :, :, None] * bb_re[None] - pw_im[:S5_LC, :, :, None] * bb_im[None]
    p_im = pw_re[:S5_LC, :, :, None] * bb_im[None] + pw_im[:S5_LC, :, :, None] * bb_re[None]
    cre = c_re.astype(F32)
    cim = c_im.astype(F32)
    kmat = (jnp.einsum('ghp,tgpk->tgkh', cre, p_re, precision=hp)
            - jnp.einsum('ghp,tgpk->tgkh', cim, p_im, precision=hp))
    nb, lg = S5_BLOCKS, S5_LANE_GROUPS
    toep = _block_diag(kmat.reshape(S5_LC * nb, lg, S5_GROUP, S5_GROUP)
                       ).reshape(S5_LC, nb, S5_K, S5_K).transpose(1, 0, 2, 3)
    bmat = jnp.concatenate(
        [_block_diag(bb_re.transpose(0, 2, 1).reshape(nb, lg, S5_GROUP, S5_STATE)),
         _block_diag(bb_im.transpose(0, 2, 1).reshape(nb, lg, S5_GROUP, S5_STATE))], axis=2)
    cmat = jnp.concatenate(
        [_block_diag(cre.transpose(0, 2, 1).reshape(nb, lg, S5_STATE, S5_GROUP)),
         _block_diag(-cim.transpose(0, 2, 1).reshape(nb, lg, S5_STATE, S5_GROUP))], axis=1)
    lane = lambda v: v.reshape(nb, 1, S5_SW)
    avec = jnp.concatenate([lane(ab_re), lane(ab_im), lane(pw_re[S5_LC]), lane(pw_im[S5_LC])],
                           axis=1)
    return toep.astype(BF16), bmat.astype(BF16), cmat.astype(BF16), avec


def _s5_kernel(u_ref, t_ref, b_ref, c_ref, a_ref, y_ref, s_acc, x_prev, x_carry, u_half, y_half,
               *, lb):
    li = pl.program_id(1)
    bsz = u_ref.shape[0]
    ncb = lb // S5_LC
    m = bsz * ncb

    @pl.when(li == 0)
    def _():
        x_carry[...] = jnp.zeros_like(x_carry)

    ar = a_ref[0, 0:1, :]
    ai = a_ref[0, 1:2, :]
    alr = a_ref[0, 2:3, :]
    ali = a_ref[0, 3:4, :]

    def cmul(zr, zi, wr, wi):
        return wr * zr - wi * zi, wr * zi + wi * zr

    n_half = S5_K // LANES
    for hf in range(n_half):
        u_half[hf] = u_ref[:, :, hf * LANES:(hf + 1) * LANES].reshape(bsz * lb, LANES)
    xs = []
    for s in range(S5_LC):
        halves = [jnp.concatenate([u_half[hf, pl.ds(c * S5_LC + s, bsz, stride=lb), :]
                                   for c in range(ncb)], axis=0) for hf in range(n_half)]
        xs.append(jnp.concatenate(halves, axis=1).astype(BF16))
    bmat = b_ref[0]
    sr = jnp.zeros((m, S5_SW), F32)
    si = jnp.zeros((m, S5_SW), F32)
    for s in range(S5_LC):
        bu = _dot(xs[s], bmat)
        sr, si = cmul(sr, si, ar, ai)
        sr = sr + bu[:, :S5_SW]
        si = si + bu[:, S5_SW:]
    s_acc[:, :S5_SW] = sr
    s_acc[:, S5_SW:] = si

    xr = x_carry[:, :S5_SW]
    xi = x_carry[:, S5_SW:]
    for c in range(ncb):
        x_prev[c * bsz:(c + 1) * bsz, :S5_SW] = xr
        x_prev[c * bsz:(c + 1) * bsz, S5_SW:] = xi
        loc = s_acc[c * bsz:(c + 1) * bsz, :]
        xr, xi = cmul(xr, xi, alr, ali)
        xr = xr + loc[:, :S5_SW]
        xi = xi + loc[:, S5_SW:]
    x_carry[:, :S5_SW] = xr
    x_carry[:, S5_SW:] = xi

    cmat = c_ref[0]
    zr = x_prev[:, :S5_SW]
    zi = x_prev[:, S5_SW:]
    for j in range(S5_LC):
        zr, zi = cmul(zr, zi, ar, ai)
        yj = _dot(jnp.concatenate([zr, zi], axis=1).astype(BF16), cmat)
        for s in range(j + 1):
            yj = yj + _dot(xs[s], t_ref[0, j - s])
        for hf in range(n_half):
            for c in range(ncb):
                y_half[hf, pl.ds(c * S5_LC + j, bsz, stride=lb), :] = (
                    yj[c * bsz:(c + 1) * bsz, hf * LANES:(hf + 1) * LANES])
    for hf in range(n_half):
        y_ref[:, :, hf * LANES:(hf + 1) * LANES] = y_half[hf].reshape(bsz, lb, LANES)


def _s5_scan(proj3, mats, lb=S5_LB):
    toep, bmat, cmat, avec = mats
    bsz, seqlen, _ = proj3.shape
    return pl.pallas_call(
        functools.partial(_s5_kernel, lb=lb),
        out_shape=jax.ShapeDtypeStruct((bsz, seqlen, MIX_WIDTH), F32),
        grid=(S5_BLOCKS, seqlen // lb),
        in_specs=[pl.BlockSpec((bsz, lb, S5_K), lambda v, l: (0, l, v)),
                  pl.BlockSpec((1, S5_LC, S5_K, S5_K), lambda v, l: (v, 0, 0, 0)),
                  pl.BlockSpec((1, S5_K, 2 * S5_SW), lambda v, l: (v, 0, 0)),
                  pl.BlockSpec((1, 2 * S5_SW, S5_K), lambda v, l: (v, 0, 0)),
                  pl.BlockSpec((1, 4, S5_SW), lambda v, l: (v, 0, 0))],
        out_specs=pl.BlockSpec((bsz, lb, S5_K), lambda v, l: (0, l, v)),
        scratch_shapes=[pltpu.VMEM((bsz * (lb // S5_LC), 2 * S5_SW), F32),
                        pltpu.VMEM((bsz * (lb // S5_LC), 2 * S5_SW), F32),
                        pltpu.VMEM((bsz, 2 * S5_SW), F32),
                        pltpu.VMEM((S5_K // LANES, bsz * lb, LANES), F32),
                        pltpu.VMEM((S5_K // LANES, bsz * lb, LANES), F32)],
        compiler_params=_cparams(("parallel", "arbitrary")),
        name="s5_scan",
    )(proj3, toep, bmat, cmat, avec)


def _s5_post_kernel(y_ref, u_ref, d_ref, w_ref, b_ref, o_ref):
    y = y_ref[...] + d_ref[...] * u_ref[...]
    y = jax.nn.gelu(y)
    o_ref[...] = y * jax.nn.sigmoid(_dot(y.astype(BF16), w_ref[...]) + b_ref[...])


def _s5_post(y_ssm, proj, d_skip, w_glu_bf16, b_glu, tm=ROW_TILE):
    t = y_ssm.shape[0]
    w = MIX_WIDTH
    return pl.pallas_call(
        _s5_post_kernel,
        out_shape=jax.ShapeDtypeStruct((t, w), F32),
        grid=(t // tm,),
        in_specs=[pl.BlockSpec((tm, w), lambda i: (i, 0)),
                  pl.BlockSpec((tm, w), lambda i: (i, 0)),
                  pl.BlockSpec((1, w), lambda i: (0, 0)),
                  pl.BlockSpec((w, w), lambda i: (0, 0)),
                  pl.BlockSpec((1, w), lambda i: (0, 0))],
        out_specs=pl.BlockSpec((tm, w), lambda i: (i, 0)),
        compiler_params=_cparams(("parallel",)),
        name="s5_post",
    )(y_ssm, proj, d_skip.reshape(1, w), w_glu_bf16, b_glu.reshape(1, w))


def _s5_mixer(proj, bsz, seqlen, a_re, a_im, log_dt, b_re, b_im, c_re, c_im, d_skip, w_glu, b_glu):
    mats = _s5_prep(a_re, a_im, log_dt, b_re, b_im, c_re, c_im)
    y = _s5_scan(proj.reshape(bsz, seqlen, proj.shape[1]), mats)
    return _s5_post(y.reshape(bsz * seqlen, MIX_WIDTH), proj, d_skip, w_glu.astype(BF16), b_glu)


def _head_norm_padded(hv, g):
    lane = lax.broadcasted_iota(jnp.int32, hv.shape, 1)
    real = lane < DV
    mu = jnp.sum(hv, axis=-1, keepdims=True) * (1.0 / DV)
    d = jnp.where(real, hv - mu, 0.0)
    var = jnp.sum(d * d, axis=-1, keepdims=True) * (1.0 / DV)
    return d * lax.rsqrt(var + LN_EPS) * g


def _log_sigmoid(x):
    return jnp.minimum(x, 0.0) - jnp.log(1.0 + jnp.exp(-jnp.abs(x)))


def _mlstm_kernel(q_ref, k_ref, v_ref, o_ref, gt_ref, cw_ref, gb_ref, ng_ref,
                  y_ref, cbuf, c_st, n_st, m_st, *, tl):
    i = pl.program_id(1)

    @pl.when(i == 0)
    def _():
        cbuf[0:SUBLANES, :] = jnp.zeros((SUBLANES, 2 * HEADS_QK), F32)
        c_st[...] = jnp.zeros_like(c_st)
        n_st[...] = jnp.zeros_like(n_st)
        m_st[...] = jnp.zeros_like(m_st)

    cbuf[SUBLANES:SUBLANES + tl, 0:HEADS_QK] = q_ref[...]
    cbuf[SUBLANES:SUBLANES + tl, HEADS_QK:2 * HEADS_QK] = k_ref[...]
    acc = jnp.zeros((tl, 2 * HEADS_QK), F32)
    for w in range(ML_CONV):
        acc = acc + cbuf[pl.ds(SUBLANES - (ML_CONV - 1) + w, tl), :] * cw_ref[w:w + 1, :]
    qk = acc * jax.nn.sigmoid(acc)
    cbuf[0:SUBLANES, :] = cbuf[tl:tl + SUBLANES, :]

    gt = gt_ref[...] + gb_ref[...]
    lf = _log_sigmoid(gt)
    gt_t = gt.T
    lf_t = _log_sigmoid(gt_t)

    cl = ML_CHUNK
    row = lax.broadcasted_iota(jnp.int32, (cl, cl), 0)
    col = lax.broadcasted_iota(jnp.int32, (cl, cl), 1)
    tri = row >= col
    scale = DQK ** -0.5
    for cc in range(tl // cl):
        r0 = cc * cl
        for h in range(N_HEADS):
            ig_col = gt[r0:r0 + cl, h:h + 1]
            lf_col = lf[r0:r0 + cl, N_HEADS + h:N_HEADS + h + 1]
            ig_row = gt_t[h:h + 1, r0:r0 + cl]
            lf_row = lf_t[N_HEADS + h:N_HEADS + h + 1, r0:r0 + cl]
            bcum_col = jnp.sum(jnp.where(tri, lf_row, 0.0), axis=1, keepdims=True)
            bcum_row = jnp.sum(jnp.where(col >= row, lf_col, 0.0), axis=0, keepdims=True)
            btot = jnp.sum(lf_row, axis=1, keepdims=True)
            w_row = btot - bcum_row + ig_row
            m_loc = jnp.max(w_row, axis=1, keepdims=True)
            e_col = jnp.exp(btot - bcum_col + ig_col - m_loc)
            m_prev = m_st[h:h + 1, 0:1]
            c_prev = c_st[h]
            n_prev = n_st[h:h + 1, :]
            q = qk[r0:r0 + cl, h * DQK_PAD:(h + 1) * DQK_PAD] * scale
            k = qk[r0:r0 + cl, HEADS_QK + h * DQK_PAD:HEADS_QK + (h + 1) * DQK_PAD]
            v = v_ref[r0:r0 + cl, h * DV_PAD:(h + 1) * DV_PAD]
            qb = q.astype(BF16)
            kb = k.astype(BF16)
            vb = v.astype(BF16)
            dmat = jnp.where(tri, bcum_col - bcum_row + ig_row, -jnp.inf)
            g_col = bcum_col + m_prev
            m_row = jnp.maximum(g_col, jnp.max(dmat, axis=1, keepdims=True))
            inter = jnp.exp(g_col - m_row)
            s_qk = lax.dot_general(qb, kb, _NT, preferred_element_type=F32) * jnp.exp(dmat - m_row)
            num = inter * _dot(qb, c_prev.astype(BF16)) + _dot(s_qk.astype(BF16), vb)
            den = (inter * jnp.sum(q * n_prev, axis=1, keepdims=True)
                   + jnp.sum(s_qk, axis=1, keepdims=True))
            hv = num / jnp.maximum(jnp.abs(den), jnp.exp(-m_row))
            ke = k * e_col
            kv = lax.dot_general(ke.astype(BF16), vb, _TN, preferred_element_type=F32)
            nk = jnp.sum(ke, axis=0, keepdims=True)
            m_new = jnp.maximum(btot + m_prev, m_loc)
            sa = jnp.exp(btot + m_prev - m_new)
            sb = jnp.exp(m_loc - m_new)
            c_st[h] = sa * c_prev + sb * kv
            n_st[h:h + 1, :] = sa * n_prev + sb * nk
            m_st[h:h + 1, :] = jnp.broadcast_to(m_new, (1, LANES))
            hn = _head_norm_padded(hv, ng_ref[0:1, h * DV_PAD:(h + 1) * DV_PAD])
            og = o_ref[r0:r0 + cl, h * DV_PAD:(h + 1) * DV_PAD]
            y_ref[r0:r0 + cl, h * DV_PAD:(h + 1) * DV_PAD] = jax.nn.sigmoid(og) * hn


_ML_GATE_BLK = (2 * HEADS_QK + 2 * HEADS_V + XATTN_WIDTH) // LANES


def _mlstm_mixer(proj, bsz, seqlen, cw, gbias, norm_g, tl=ROW_TILE):
    t = proj.shape[0]
    nl = seqlen // tl
    rows = lambda b, i: b * nl + i
    return pl.pallas_call(
        functools.partial(_mlstm_kernel, tl=tl),
        out_shape=jax.ShapeDtypeStruct((t, HEADS_V), F32),
        grid=(bsz, nl),
        in_specs=[pl.BlockSpec((tl, HEADS_QK), lambda b, i: (rows(b, i), 0)),
                  pl.BlockSpec((tl, HEADS_QK), lambda b, i: (rows(b, i), 1)),
                  pl.BlockSpec((tl, HEADS_V), lambda b, i: (rows(b, i), 1)),
                  pl.BlockSpec((tl, HEADS_V), lambda b, i: (rows(b, i), 2)),
                  pl.BlockSpec((tl, LANES), lambda b, i: (rows(b, i), _ML_GATE_BLK)),
                  pl.BlockSpec((ML_CONV, 2 * HEADS_QK), lambda b, i: (0, 0)),
                  pl.BlockSpec((1, LANES), lambda b, i: (0, 0)),
                  pl.BlockSpec((1, HEADS_V), lambda b, i: (0, 0))],
        out_specs=pl.BlockSpec((tl, HEADS_V), lambda b, i: (rows(b, i), 0)),
        scratch_shapes=[pltpu.VMEM((tl + SUBLANES, 2 * HEADS_QK), F32),
                        pltpu.VMEM((N_HEADS, DQK_PAD, DV_PAD), F32),
                        pltpu.VMEM((SUBLANES, DQK_PAD), F32),
                        pltpu.VMEM((SUBLANES, LANES), F32)],
        compiler_params=_cparams(("parallel", "arbitrary")),
        name="mlstm",
    )(proj, proj, proj, proj, proj, cw, gbias, norm_g)


def _ret_log_gamma(h):
    return float(np.log(np.float32(1.0) - np.power(np.float32(2.0), np.float32(-5.0 - h))))


def _ret_kernel(q_ref, k_ref, v_ref, g_ref, pos_ref, inv_ref, sgn_ref, ng_ref,
                y_ref, s_st, *, tl):
    i = pl.program_id(1)

    @pl.when(i == 0)
    def _():
        s_st[...] = jnp.zeros_like(s_st)

    ang = pos_ref[...] * inv_ref[...]
    cos_t = jnp.cos(ang)
    sin_t = jnp.sin(ang) * sgn_ref[...]
    cl = RET_CHUNK
    row = lax.broadcasted_iota(jnp.int32, (cl, cl), 0)
    col = lax.broadcasted_iota(jnp.int32, (cl, cl), 1)
    rel = (row - col).astype(F32)
    jcol = lax.broadcasted_iota(jnp.int32, (cl, 1), 0).astype(F32)
    kscale = DQK ** -0.5
    for h in range(N_HEADS):
        lg = _ret_log_gamma(h)
        decay = jnp.where(rel >= 0, jnp.exp(jnp.maximum(rel, 0.0) * lg), 0.0)
        zeta = jnp.exp((cl - 1 - jcol) * lg)
        xi = jnp.exp((jcol + 1.0) * lg)
        chunk_decay = float(np.exp(np.float32(cl) * np.float32(lg)))
        qh = q_ref[:, h * DQK_PAD:(h + 1) * DQK_PAD]
        kh = k_ref[:, h * DQK_PAD:(h + 1) * DQK_PAD]
        qh = qh * cos_t + pltpu.roll(qh, DQK_PAD // 2, 1) * sin_t
        kh = (kh * cos_t + pltpu.roll(kh, DQK_PAD // 2, 1) * sin_t) * kscale
        for cc in range(tl // cl):
            r0 = cc * cl
            qb = qh[r0:r0 + cl].astype(BF16)
            k = kh[r0:r0 + cl]
            kb = k.astype(BF16)
            v = v_ref[r0:r0 + cl, h * DV_PAD:(h + 1) * DV_PAD]
            vb = v.astype(BF16)
            s_prev = s_st[h]
            s = lax.dot_general(qb, kb, _NT, preferred_element_type=F32) * decay
            intra = _dot(s.astype(BF16), vb)
            cross = _dot(qb, s_prev.astype(BF16)) * xi
            r = lax.dot_general((k * zeta).astype(BF16), vb, _TN, preferred_element_type=F32)
            s_st[h] = chunk_decay * s_prev + r
            hn = _head_norm_padded(intra + cross, ng_ref[0:1, h * DV_PAD:(h + 1) * DV_PAD])
            gate = g_ref[r0:r0 + cl, h * DV_PAD:(h + 1) * DV_PAD]
            y_ref[r0:r0 + cl, h * DV_PAD:(h + 1) * DV_PAD] = gate * jax.nn.sigmoid(gate) * hn


def _ret_mixer(proj, bsz, seqlen, pos_f, inv_pad, sgn_pad, norm_g, tl=ROW_TILE):
    t = proj.shape[0]
    nl = seqlen // tl
    rows = lambda b, i: b * nl + i
    return pl.pallas_call(
        functools.partial(_ret_kernel, tl=tl),
        out_shape=jax.ShapeDtypeStruct((t, HEADS_V), F32),
        grid=(bsz, nl),
        in_specs=[pl.BlockSpec((tl, HEADS_QK), lambda b, i: (rows(b, i), 0)),
                  pl.BlockSpec((tl, HEADS_QK), lambda b, i: (rows(b, i), 1)),
                  pl.BlockSpec((tl, HEADS_V), lambda b, i: (rows(b, i), 1)),
                  pl.BlockSpec((tl, HEADS_V), lambda b, i: (rows(b, i), 2)),
                  pl.BlockSpec((tl, 1), lambda b, i: (rows(b, i), 0)),
                  pl.BlockSpec((1, DQK_PAD), lambda b, i: (0, 0)),
                  pl.BlockSpec((1, DQK_PAD), lambda b, i: (0, 0)),
                  pl.BlockSpec((1, HEADS_V), lambda b, i: (0, 0))],
        out_specs=pl.BlockSpec((tl, HEADS_V), lambda b, i: (rows(b, i), 0)),
        scratch_shapes=[pltpu.VMEM((N_HEADS, DQK_PAD, DV_PAD), F32)],
        compiler_params=_cparams(("parallel", "arbitrary")),
        name="retention",
    )(proj, proj, proj, proj, pos_f, inv_pad, sgn_pad, norm_g)


def _pack_rows(x):
    half = x.shape[1] // 2
    lo = pltpu.bitcast(x[:, :half].astype(BF16).astype(F32), jnp.uint32)
    hi = pltpu.bitcast(x[:, half:].astype(BF16).astype(F32), jnp.uint32)
    return hi | (lo >> 16)


def _unpack_rows(u):
    lo = pltpu.bitcast(u << 16, F32)
    hi = pltpu.bitcast(u & jnp.uint32(0xFFFF0000), F32)
    return jnp.concatenate([lo, hi], axis=1)


def _post_kernel(ym_ref, xq_ref, h_ref, mk_ref, mv_ref, wom_ref, wox_ref, g_ref, b_ref,
                 rwh_ref, rwl_ref, rb_ref, h1_ref, h1p_ref, idx_ref, gate_ref):
    tl = xq_ref.shape[0]
    xq = xq_ref[...] * (XATTN_HEAD_DIM ** -0.5)
    lane = lax.broadcasted_iota(jnp.int32, (tl, XATTN_WIDTH), 1)
    head = lane // XATTN_HEAD_DIM
    mk = mk_ref[0]
    mv = mv_ref[0]
    ymem = jnp.zeros((tl, XATTN_WIDTH), F32)
    for hh in range(XATTN_HEADS):
        sel = head == hh
        qh = jnp.where(sel, xq, 0.0).astype(BF16)
        s = lax.dot_general(qh, mk, _NT, preferred_element_type=F32)
        s = s - jnp.max(s, axis=-1, keepdims=True)
        p = jnp.exp(s)
        p = p / jnp.sum(p, axis=-1, keepdims=True)
        ymem = jnp.where(sel, _dot(p.astype(BF16), mv), ymem)
    y = _dot(ym_ref[...].astype(BF16), wom_ref[...]) + _dot(ymem.astype(BF16), wox_ref[...])
    h1 = _layer_norm_rows(DEEPNORM_ALPHA * h_ref[...] + y, g_ref[...], b_ref[...])
    h1_ref[...] = h1
    h1p_ref[...] = _pack_rows(h1)
    h_hi = h1.astype(BF16)
    h_lo = (h1 - h_hi.astype(F32)).astype(BF16)
    logits = (_dot(h_hi, rwh_ref[...]) + _dot(h_lo, rwh_ref[...]) + _dot(h_hi, rwl_ref[...])
              + rb_ref[...])
    ln = lax.broadcasted_iota(jnp.int32, logits.shape, 1)
    vals = logits
    tv, ti = [], []
    for _ in range(TOP_K):
        m = jnp.max(vals, axis=-1, keepdims=True)
        ix = jnp.min(jnp.where(vals == m, ln, LANES), axis=-1, keepdims=True)
        tv.append(m)
        ti.append(ix)
        vals = jnp.where(ln == ix, -jnp.inf, vals)
    ex = [jnp.exp(v - tv[0]) for v in tv]
    tot = ex[0] + ex[1] + ex[2] + ex[3]
    idx_out = jnp.zeros(logits.shape, jnp.int32)
    gate_out = jnp.zeros(logits.shape, F32)
    for k in range(TOP_K):
        idx_out = jnp.where(ln == k, ti[k], idx_out)
        gate_out = jnp.where(ln == k, ex[k] / tot, gate_out)
    idx_ref[...] = idx_out
    gate_ref[...] = gate_out


def _post_mixer(y_mix, proj, xq_blk, h, mem_k, mem_v, wo_mix, wo_mem, ln_g, ln_b, rw, rb,
                seqlen, tl=4 * ROW_TILE):
    t, cm = y_mix.shape
    nl = seqlen // tl
    d = D_MODEL
    rw_hi = rw.astype(BF16)
    rw_lo = (rw - rw_hi.astype(F32)).astype(BF16)
    full = lambda a, b: pl.BlockSpec((a, b), lambda i: (0, 0))
    return pl.pallas_call(
        _post_kernel,
        out_shape=(jax.ShapeDtypeStruct((t, d), F32),
                   jax.ShapeDtypeStruct((t, d // 2), jnp.uint32),
                   jax.ShapeDtypeStruct((t, LANES), jnp.int32),
                   jax.ShapeDtypeStruct((t, LANES), F32)),
        grid=(t // tl,),
        in_specs=[pl.BlockSpec((tl, cm), lambda i: (i, 0)),
                  pl.BlockSpec((tl, XATTN_WIDTH), lambda i: (i, xq_blk)),
                  pl.BlockSpec((tl, d), lambda i: (i, 0)),
                  pl.BlockSpec((1, N_MEM, XATTN_WIDTH), lambda i: (i // nl, 0, 0)),
                  pl.BlockSpec((1, N_MEM, XATTN_WIDTH), lambda i: (i // nl, 0, 0)),
                  full(cm, d), full(XATTN_WIDTH, d), full(1, d), full(1, d),
                  full(d, LANES), full(d, LANES), full(1, LANES)],
        out_specs=(pl.BlockSpec((tl, d), lambda i: (i, 0)),
                   pl.BlockSpec((tl, d // 2), lambda i: (i, 0)),
                   pl.BlockSpec((tl, LANES), lambda i: (i, 0)),
                   pl.BlockSpec((tl, LANES), lambda i: (i, 0))),
        compiler_params=_cparams(("parallel",)),
        name="post_mixer",
    )(y_mix, proj, h, mem_k, mem_v, wo_mix, wo_mem, ln_g, ln_b, rw_hi, rw_lo, rb)


SC_CORES = 2
SC_SUBCORES = 16
SC_WORKERS = SC_CORES * SC_SUBCORES
SC_CHUNK = 64


def _sc_gather(table, idx):
    v, d = table.shape
    b = idx.shape[0]
    per_w = b // SC_WORKERS
    n_chunks = per_w // SC_CHUNK
    assert per_w * SC_WORKERS == b and n_chunks * SC_CHUNK == per_w and n_chunks % 2 == 0
    mesh = plsc.VectorSubcoreMesh(core_axis_name="c", subcore_axis_name="s")

    @functools.partial(
        pl.kernel, mesh=mesh,
        out_type=jax.ShapeDtypeStruct((b, d), table.dtype),
        scratch_types=[pltpu.VMEM((per_w,), jnp.int32),
                       pltpu.VMEM((SC_CHUNK, d), table.dtype),
                       pltpu.VMEM((SC_CHUNK, d), table.dtype),
                       pltpu.SemaphoreType.DMA,
                       pltpu.SemaphoreType.DMA],
    )
    def gather_kernel(table_hbm, idx_hbm, out_hbm, idx_v, rows0, rows1, sem0, sem1):
        wid = lax.axis_index("s") * SC_CORES + lax.axis_index("c")
        base = wid * per_w
        pltpu.sync_copy(idx_hbm.at[pl.ds(pl.multiple_of(base, 8), per_w)], idx_v)
        ring = ((rows0, sem0), (rows1, sem1))

        def gather(c, buf, sem):
            rows = idx_v.at[pl.ds(pl.multiple_of(c * SC_CHUNK, 8), SC_CHUNK)]
            return pltpu.make_async_copy(table_hbm.at[rows], buf, sem)

        for c0, (buf, sem) in enumerate(ring):
            gather(c0, buf, sem).start()

        @pl.loop(0, n_chunks, step=2)
        def _(c):
            for k, (buf, sem) in enumerate(ring):
                cc = c + k
                gather(cc, buf, sem).wait()
                off = pl.multiple_of(base + cc * SC_CHUNK, 8)
                pltpu.sync_copy(buf, out_hbm.at[pl.ds(off, SC_CHUNK)])

                @pl.when(cc + 2 < n_chunks)
                def _():
                    gather(cc + 2, buf, sem).start()

    return gather_kernel(table, idx)


def _sc_dispatch(x, pos_kmajor, n_rows):
    t, d = x.shape
    per_w = t // SC_WORKERS
    n_chunks = per_w // SC_CHUNK
    assert per_w * SC_WORKERS == t and n_chunks * SC_CHUNK == per_w
    mesh = plsc.VectorSubcoreMesh(core_axis_name="c", subcore_axis_name="s")

    @functools.partial(
        pl.kernel, mesh=mesh,
        out_type=jax.ShapeDtypeStruct((n_rows, d), x.dtype),
        scratch_types=[pltpu.VMEM((SC_CHUNK,), jnp.int32),
                       pltpu.VMEM((SC_CHUNK, d), x.dtype)],
    )
    def dispatch_kernel(x_hbm, pos_hbm, out_hbm, idx_v, rows_v):
        wid = lax.axis_index("s") * SC_CORES + lax.axis_index("c")
        base = wid * per_w

        @pl.loop(0, n_chunks)
        def _(j):
            off = pl.multiple_of(base + j * SC_CHUNK, 8)
            pltpu.sync_copy(x_hbm.at[pl.ds(off, SC_CHUNK)], rows_v)
            for k in range(TOP_K):
                pltpu.sync_copy(pos_hbm.at[pl.ds(pl.multiple_of(k * t + off, 8), SC_CHUNK)], idx_v)
                pltpu.sync_copy(rows_v, out_hbm.at[idx_v])

    return dispatch_kernel(x, pos_kmajor)


def _expert_kernel(bexp_ref, nvalid_ref, nused_ref, x_ref, wgu_ref, bgu_ref, wd_ref, bd_ref, y_ref,
                   wgu_bf, wd_bf):
    i = pl.program_id(0)
    de = wd_ref.shape[2]

    @pl.when(i < nused_ref[0])
    def _():
        prev = bexp_ref[jnp.maximum(i - 1, 0)]

        @pl.when((i == 0) | (prev != bexp_ref[i]))
        def _():
            wgu_bf[...] = wgu_ref[0, 0].astype(BF16)
            wd_bf[...] = wd_ref[0, 0].astype(BF16)

        rows = lax.broadcasted_iota(jnp.int32, x_ref.shape, 0)
        xb = _unpack_rows(jnp.where(rows < nvalid_ref[i], x_ref[...], jnp.uint32(0))).astype(BF16)
        gu = _dot(xb, wgu_bf[...]) + bgu_ref[0, 0]
        x_glu = jnp.minimum(gu[:, :de], SWIGLU_LIMIT)
        x_lin = jnp.clip(gu[:, de:], -SWIGLU_LIMIT, SWIGLU_LIMIT)
        act = x_glu * jax.nn.sigmoid(SWIGLU_ALPHA * x_glu) * (x_lin + 1.0)
        y_ref[...] = _pack_rows(_dot(act.astype(BF16), wd_bf[...]) + bd_ref[0, 0])

    @pl.when(i >= nused_ref[0])
    def _():
        y_ref[...] = jnp.zeros_like(y_ref)


def _experts(xr, block_exp, n_valid, n_used, layer, w_gu, b_gu, w_down, b_down, bm=MOE_BM):
    n_rows, dp = xr.shape
    d = 2 * dp
    n_blocks = n_rows // bm
    nl, ne, _, de2 = w_gu.shape
    de = de2 // 2
    row_blk = lambda i, nu: jnp.minimum(i, nu[0] - 1)
    grid_spec = pltpu.PrefetchScalarGridSpec(
        num_scalar_prefetch=3,
        grid=(n_blocks,),
        in_specs=[pl.BlockSpec((bm, dp), lambda i, be, nv, nu: (row_blk(i, nu), 0)),
                  pl.BlockSpec((1, 1, d, de2), lambda i, be, nv, nu: (layer, be[i], 0, 0)),
                  pl.BlockSpec((1, 1, 1, de2), lambda i, be, nv, nu: (layer, be[i], 0, 0)),
                  pl.BlockSpec((1, 1, de, d), lambda i, be, nv, nu: (layer, be[i], 0, 0)),
                  pl.BlockSpec((1, 1, 1, d), lambda i, be, nv, nu: (layer, be[i], 0, 0))],
        out_specs=pl.BlockSpec((bm, dp), lambda i, be, nv, nu: (i, 0)),
        scratch_shapes=[pltpu.VMEM((d, de2), BF16),
                        pltpu.VMEM((de, d), BF16)],
    )
    return pl.pallas_call(
        _expert_kernel,
        out_shape=jax.ShapeDtypeStruct((n_rows, dp), jnp.uint32),
        grid_spec=grid_spec,
        compiler_params=_cparams(("arbitrary",)),
        name="experts",
    )(block_exp, n_valid, n_used, xr, w_gu, b_gu.reshape(nl, ne, 1, de2), w_down,
      b_down.reshape(nl, ne, 1, d))


def _combine_kernel(y0_ref, y1_ref, y2_ref, y3_ref, h1_ref, gate_ref, g_ref, b_ref, o_ref):
    gate = gate_ref[...]
    acc = DEEPNORM_ALPHA * h1_ref[...]
    for k, y_ref in enumerate((y0_ref, y1_ref, y2_ref, y3_ref)):
        acc = acc + gate[:, k:k + 1] * _unpack_rows(y_ref[0])
    o_ref[...] = _layer_norm_rows(acc, g_ref[...], b_ref[...])


def _combine(yg, h1, gates, ln_g, ln_b, tl=ROW_TILE):
    t, d = h1.shape
    ysel = lambda k: pl.BlockSpec((1, tl, d // 2), lambda i: (k, i, 0))
    return pl.pallas_call(
        _combine_kernel,
        out_shape=jax.ShapeDtypeStruct((t, d), F32),
        grid=(t // tl,),
        in_specs=[ysel(0), ysel(1), ysel(2), ysel(3),
                  pl.BlockSpec((tl, d), lambda i: (i, 0)),
                  pl.BlockSpec((tl, LANES), lambda i: (i, 0)),
                  pl.BlockSpec((1, d), lambda i: (0, 0)),
                  pl.BlockSpec((1, d), lambda i: (0, 0))],
        out_specs=pl.BlockSpec((tl, d), lambda i: (i, 0)),
        compiler_params=_cparams(("parallel",)),
        name="combine",
    )(yg, yg, yg, yg, h1, gates, ln_g, ln_b)


def _route(idx, bm=MOE_BM):
    t = idx.shape[0]
    n_assign = t * TOP_K
    n_rows = (-(-n_assign // bm) + N_EXPERTS) * bm
    n_blocks = n_rows // bm
    onehot = (idx[:, :, None] == jnp.arange(N_EXPERTS, dtype=jnp.int32)[None, None, :])
    sel = jnp.sum(onehot.astype(jnp.int32), axis=1)
    csum = jnp.cumsum(sel, axis=0)
    counts = csum[-1]
    rank = csum - sel
    padded = ((counts + bm - 1) // bm) * bm
    pad_end = jnp.cumsum(padded)
    pad_start = pad_end - padded
    pos = pad_start[idx] + jnp.take_along_axis(rank, idx, axis=1)
    block_start = jnp.arange(n_blocks, dtype=jnp.int32) * bm
    block_exp = jnp.minimum(
        jnp.sum((block_start[:, None] >= pad_end[None, :]).astype(jnp.int32), axis=1),
        N_EXPERTS - 1).astype(jnp.int32)
    n_valid = jnp.clip(pad_start[block_exp] + counts[block_exp] - block_start, 0, bm).astype(jnp.int32)
    n_used = (pad_end[-1] // bm).astype(jnp.int32).reshape(1)
    return pos.T.reshape(-1).astype(jnp.int32), block_exp, n_valid, n_used, n_rows


def _take_cols(w, cols):
    cols = np.asarray(cols, np.int32)
    out = jnp.take(w, jnp.asarray(np.maximum(cols, 0)), axis=-1)
    return jnp.where(jnp.asarray(cols >= 0), out, 0.0)


def _head_cols(offset, width, pad):
    cols = []
    for h in range(N_HEADS):
        cols += list(range(offset + h * width, offset + (h + 1) * width)) + [-1] * (pad - width)
    return cols


def _rope_head_cols(offset):
    half = DQK // 2
    slot = DQK_PAD // 2
    cols = []
    for h in range(N_HEADS):
        b = offset + h * DQK
        cols += list(range(b, b + half)) + [-1] * (slot - half)
        cols += list(range(b + half, b + DQK)) + [-1] * (slot - half)
    return cols


_ML_GATE_OFF = 2 * ML_QK + 2 * MIX_WIDTH
_ML_COLS = (_head_cols(0, DQK, DQK_PAD) + _head_cols(ML_QK, DQK, DQK_PAD)
            + _head_cols(2 * ML_QK, DV, DV_PAD) + _head_cols(2 * ML_QK + MIX_WIDTH, DV, DV_PAD)
            + list(range(_ML_GATE_OFF + 2 * N_HEADS, _ML_GATE_OFF + 2 * N_HEADS + XATTN_WIDTH))
            + list(range(_ML_GATE_OFF, _ML_GATE_OFF + 2 * N_HEADS)) + [-1] * (LANES - 2 * N_HEADS))
_RET_COLS = (_rope_head_cols(0) + _rope_head_cols(ML_QK)
             + _head_cols(2 * ML_QK, DV, DV_PAD) + _head_cols(2 * ML_QK + MIX_WIDTH, DV, DV_PAD)
             + list(range(2 * ML_QK + 2 * MIX_WIDTH, 2 * ML_QK + 2 * MIX_WIDTH + XATTN_WIDTH)))
_MIX_PAD_COLS = _head_cols(0, DV, DV_PAD)
_XQ_BLK_PADDED = (2 * HEADS_QK + 2 * HEADS_V) // XATTN_WIDTH
_XQ_BLK_S5 = MIX_WIDTH // XATTN_WIDTH


def kernel(x, mem, positions, mem_w_k, mem_w_v, l0_w_in, l0_s5_a_re, l0_s5_a_im, l0_s5_log_dt, l0_s5_b_re, l0_s5_b_im, l0_s5_c_re, l0_s5_c_im, l0_s5_d, l0_s5_w_glu, l0_s5_b_glu, l1_w_in, l1_ml_conv_q, l1_ml_conv_k, l1_ml_b_i, l1_ml_b_f, l1_ml_norm_g, l2_w_in, l2_ret_norm_g, l3_w_in, l3_s5_a_re, l3_s5_a_im, l3_s5_log_dt, l3_s5_b_re, l3_s5_b_im, l3_s5_c_re, l3_s5_c_im, l3_s5_d, l3_s5_w_glu, l3_s5_b_glu, w_out, ln1_g, ln1_b, ln2_g, ln2_b, router_w, router_b, exp_w_gu, exp_b_gu, exp_w_down, exp_b_down):
    bsz, seqlen, d = x.shape
    t = bsz * seqlen
    h = x.reshape(t, d)

    w_kv = jnp.concatenate([mem_w_k, mem_w_v], axis=1).astype(BF16)
    kv = _inproj(mem.reshape(bsz * N_MEM, d), w_kv).astype(BF16)
    mem_k = kv[:, :XATTN_WIDTH].reshape(bsz, N_MEM, XATTN_WIDTH)
    mem_v = kv[:, XATTN_WIDTH:].reshape(bsz, N_MEM, XATTN_WIDTH)

    half = DQK // 2
    inv = ROPE_BASE ** (-jnp.arange(0, DQK, 2, dtype=F32) / DQK)
    zpad = jnp.zeros((DQK_PAD // 2 - half,), F32)
    inv_pad = jnp.concatenate([inv, zpad, inv, zpad]).reshape(1, DQK_PAD)
    sgn_pad = jnp.concatenate([-jnp.ones((half,), F32), zpad, jnp.ones((half,), F32), zpad]
                              ).reshape(1, DQK_PAD)
    pos_f = positions.astype(F32).reshape(t, 1)

    s5_params = {
        0: (l0_s5_a_re, l0_s5_a_im, l0_s5_log_dt, l0_s5_b_re, l0_s5_b_im, l0_s5_c_re, l0_s5_c_im,
            l0_s5_d, l0_s5_w_glu, l0_s5_b_glu),
        3: (l3_s5_a_re, l3_s5_a_im, l3_s5_log_dt, l3_s5_b_re, l3_s5_b_im, l3_s5_c_re, l3_s5_c_im,
            l3_s5_d, l3_s5_w_glu, l3_s5_b_glu),
    }
    w_ins = (l0_w_in, l1_w_in, l2_w_in, l3_w_in)

    for i in range(DEPTH):
        kind = i % 3
        wo = w_out[i]
        if kind == 0:
            proj = _inproj(h, w_ins[i].astype(BF16))
            y_mix = _s5_mixer(proj, bsz, seqlen, *s5_params[i])
            wo_mix = wo[:MIX_WIDTH].astype(BF16)
            xq_blk = _XQ_BLK_S5
        elif kind == 1:
            proj = _inproj(h, _take_cols(w_ins[i], _ML_COLS).astype(BF16))
            cw = jnp.concatenate([_take_cols(l1_ml_conv_q, _head_cols(0, DQK, DQK_PAD)),
                                  _take_cols(l1_ml_conv_k, _head_cols(0, DQK, DQK_PAD))], axis=1)
            gbias = jnp.concatenate([l1_ml_b_i, l1_ml_b_f,
                                     jnp.zeros((LANES - 2 * N_HEADS,), F32)]).reshape(1, LANES)
            norm_g = _take_cols(l1_ml_norm_g, _MIX_PAD_COLS).reshape(1, HEADS_V)
            y_mix = _mlstm_mixer(proj, bsz, seqlen, cw, gbias, norm_g)
            wo_mix = _take_cols(wo[:MIX_WIDTH].T, _MIX_PAD_COLS).T.astype(BF16)
            xq_blk = _XQ_BLK_PADDED
        else:
            proj = _inproj(h, _take_cols(w_ins[i], _RET_COLS).astype(BF16))
            norm_g = _take_cols(l2_ret_norm_g, _MIX_PAD_COLS).reshape(1, HEADS_V)
            y_mix = _ret_mixer(proj, bsz, seqlen, pos_f, inv_pad, sgn_pad, norm_g)
            wo_mix = _take_cols(wo[:MIX_WIDTH].T, _MIX_PAD_COLS).T.astype(BF16)
            xq_blk = _XQ_BLK_PADDED
        wo_mem = wo[MIX_WIDTH:].astype(BF16)
        rw = jnp.pad(router_w[i], ((0, 0), (0, LANES - N_EXPERTS)))
        rb = jnp.concatenate([router_b[i], jnp.full((LANES - N_EXPERTS,), -1e30, F32)]
                             ).reshape(1, LANES)
        h1, h1p, idx, gates = _post_mixer(y_mix, proj, xq_blk, h, mem_k, mem_v, wo_mix, wo_mem,
                                     ln1_g[i].reshape(1, d), ln1_b[i].reshape(1, d), rw, rb, seqlen)
        pos_kmajor, block_exp, n_valid, n_used, n_rows = _route(idx[:, :TOP_K])
        xr = _sc_dispatch(h1p, pos_kmajor, n_rows)
        yr = _experts(xr, block_exp, n_valid, n_used, i, exp_w_gu, exp_b_gu, exp_w_down, exp_b_down)
        yg = _sc_gather(yr, pos_kmajor).reshape(TOP_K, t, d // 2)
        h = _combine(yg, h1, gates, ln2_g[i].reshape(1, d), ln2_b[i].reshape(1, d))
    return h.reshape(bsz, seqlen, d)
```

```python
import functools

import numpy as np
import jax
import jax.numpy as jnp
from jax import lax
from jax.experimental import pallas as pl
from jax.experimental.pallas import tpu as pltpu
from jax.experimental.pallas import tpu_sc as plsc

F32 = jnp.float32
BF16 = jnp.bfloat16

D_MODEL = 1024
DEPTH = 4
N_MEM = 256
MIX_WIDTH = 768
XATTN_HEADS = 4
XATTN_WIDTH = 256
XATTN_HEAD_DIM = 64
S5_GROUP = 16
S5_GROUPS = 48
S5_STATE = 64
N_HEADS = 4
DQK = 96
DV = 192
ML_QK = 384
ML_CONV = 4
ML_CHUNK = 64
RET_CHUNK = 128
ROPE_BASE = 10000.0
N_EXPERTS = 32
TOP_K = 4
SWIGLU_LIMIT = 7.0
SWIGLU_ALPHA = 1.702
DEEPNORM_ALPHA = (2.0 * DEPTH) ** 0.25
LN_EPS = 1e-5

LANES = 128
SUBLANES = 8
DQK_PAD = 128
DV_PAD = 256
HEADS_QK = N_HEADS * DQK_PAD
HEADS_V = N_HEADS * DV_PAD
S5_LC = 16
S5_K = S5_LC * S5_GROUP
MOE_BM = 512
ROW_TILE = 256
VMEM_LIMIT = 56 * 1024 * 1024

_NT = (((1,), (1,)), ((), ()))
_TN = (((0,), (0,)), ((), ()))


def _cparams(sem):
    return pltpu.CompilerParams(dimension_semantics=sem, vmem_limit_bytes=VMEM_LIMIT)


def _dot(a, b):
    return jnp.dot(a, b, preferred_element_type=F32)


def _layer_norm_rows(z, g, b):
    mu = jnp.mean(z, axis=-1, keepdims=True)
    d = z - mu
    var = jnp.mean(d * d, axis=-1, keepdims=True)
    return d * lax.rsqrt(var + LN_EPS) * g + b


def _inproj_kernel(x_ref, w_ref, o_ref):
    xb = x_ref[...].astype(BF16)
    n = o_ref.shape[1]
    step = 512
    for c0 in range(0, n, step):
        c1 = min(c0 + step, n)
        o_ref[:, c0:c1] = _dot(xb, w_ref[:, c0:c1])


def _inproj(x, w_bf16, tm=ROW_TILE):
    t, d = x.shape
    n = w_bf16.shape[1]
    return pl.pallas_call(
        _inproj_kernel,
        out_shape=jax.ShapeDtypeStruct((t, n), F32),
        grid=(t // tm,),
        in_specs=[pl.BlockSpec((tm, d), lambda i: (i, 0)),
                  pl.BlockSpec((d, n), lambda i: (0, 0))],
        out_specs=pl.BlockSpec((tm, n), lambda i: (i, 0)),
        compiler_params=_cparams(("parallel",)),
        name="inproj",
    )(x, w_bf16)


S5_LANE_GROUPS = S5_K // S5_GROUP
S5_BLOCKS = MIX_WIDTH // S5_K
S5_SW = S5_LANE_GROUPS * S5_STATE
S5_LB = 512


def _block_diag(m):
    nb, g, a, b = m.shape
    eye = jnp.eye(g, dtype=m.dtype)
    return (m[:, :, :, None, :] * eye[None, :, None, :, None]).reshape(nb, g * a, g * b)


def _s5_prep(a_re, a_im, log_dt, b_re, b_im, c_re, c_im):
    hp = lax.Precision.HIGHEST
    lam_re = jnp.minimum(a_re.astype(F32), -1e-4)
    lam_im = a_im.astype(F32)
    dt = jnp.exp(log_dt.astype(F32))[:, None]
    mag = jnp.exp(dt * lam_re)
    ab_re = mag * jnp.cos(dt * lam_im)
    ab_im = mag * jnp.sin(dt * lam_im)
    den = lam_re * lam_re + lam_im * lam_im
    num_re = ab_re - 1.0
    coef_re = (num_re * lam_re + ab_im * lam_im) / den
    coef_im = (ab_im * lam_re - num_re * lam_im) / den
    bre = b_re.astype(F32)
    bim = b_im.astype(F32)
    bb_re = coef_re[..., None] * bre - coef_im[..., None] * bim
    bb_im = coef_re[..., None] * bim + coef_im[..., None] * bre
    pr = [jnp.ones_like(ab_re)]
    pi = [jnp.zeros_like(ab_im)]
    for _ in range(S5_LC):
        r, i = pr[-1], pi[-1]
        pr.append(r * ab_re - i * ab_im)
        pi.append(r * ab_im + i * ab_re)
    pw_re = jnp.stack(pr)
    pw_im = jnp.stack(pi)
    p_re = pw_re[:S5_LC, :, :, None] * bb_re[None] - pw_im[:S5_LC, :, :, None] * bb_im[None]
    p_im = pw_re[:S5_LC, :, :, None] * bb_im[None] + pw_im[:S5_LC, :, :, None] * bb_re[None]
    cre = c_re.astype(F32)
    cim = c_im.astype(F32)
    kmat = (jnp.einsum('ghp,tgpk->tgkh', cre, p_re, precision=hp)
            - jnp.einsum('ghp,tgpk->tgkh', cim, p_im, precision=hp))
    nb, lg = S5_BLOCKS, S5_LANE_GROUPS
    toep = _block_diag(kmat.reshape(S5_LC * nb, lg, S5_GROUP, S5_GROUP)
                       ).reshape(S5_LC, nb, S5_K, S5_K).transpose(1, 0, 2, 3)
    bmat = jnp.concatenate(
        [_block_diag(bb_re.transpose(0, 2, 1).reshape(nb, lg, S5_GROUP, S5_STATE)),
         _block_diag(bb_im.transpose(0, 2, 1).reshape(nb, lg, S5_GROUP, S5_STATE))], axis=2)
    cmat = jnp.concatenate(
        [_block_diag(cre.transpose(0, 2, 1).reshape(nb, lg, S5_STATE, S5_GROUP)),
         _block_diag(-cim.transpose(0, 2, 1).reshape(nb, lg, S5_STATE, S5_GROUP))], axis=1)
    lane = lambda v: v.reshape(nb, 1, S5_SW)
    avec = jnp.concatenate([lane(ab_re), lane(ab_im), lane(pw_re[S5_LC]), lane(pw_im[S5_LC])],
                           axis=1)
    return toep.astype(BF16), bmat.astype(BF16), cmat.astype(BF16), avec


def _s5_kernel(u_ref, t_ref, b_ref, c_ref, a_ref, y_ref, s_acc, x_prev, x_carry, u_half, y_half,
               *, lb):
    li = pl.program_id(1)
    bsz = u_ref.shape[0]
    ncb = lb // S5_LC
    m = bsz * ncb

    @pl.when(li == 0)
    def _():
        x_carry[...] = jnp.zeros_like(x_carry)

    ar = a_ref[0, 0:1, :]
    ai = a_ref[0, 1:2, :]
    alr = a_ref[0, 2:3, :]
    ali = a_ref[0, 3:4, :]

    def cmul(zr, zi, wr, wi):
        return wr * zr - wi * zi, wr * zi + wi * zr

    n_half = S5_K // LANES
    for hf in range(n_half):
        u_half[hf] = u_ref[:, :, hf * LANES:(hf + 1) * LANES].reshape(bsz * lb, LANES)
    xs = []
    for s in range(S5_LC):
        halves = [jnp.concatenate([u_half[hf, pl.ds(c * S5_LC + s, bsz, stride=lb), :]
                                   for c in range(ncb)], axis=0) for hf in range(n_half)]
        xs.append(jnp.concatenate(halves, axis=1).astype(BF16))
    bmat = b_ref[0]
    sr = jnp.zeros((m, S5_SW), F32)
    si = jnp.zeros((m, S5_SW), F32)
    for s in range(S5_LC):
        bu = _dot(xs[s], bmat)
        sr, si = cmul(sr, si, ar, ai)
        sr = sr + bu[:, :S5_SW]
        si = si + bu[:, S5_SW:]
    s_acc[:, :S5_SW] = sr
    s_acc[:, S5_SW:] = si

    xr = x_carry[:, :S5_SW]
    xi = x_carry[:, S5_SW:]
    for c in range(ncb):
        x_prev[c * bsz:(c + 1) * bsz, :S5_SW] = xr
        x_prev[c * bsz:(c + 1) * bsz, S5_SW:] = xi
        loc = s_acc[c * bsz:(c + 1) * bsz, :]
        xr, xi = cmul(xr, xi, alr, ali)
        xr = xr + loc[:, :S5_SW]
        xi = xi + loc[:, S5_SW:]
    x_carry[:, :S5_SW] = xr
    x_carry[:, S5_SW:] = xi

    cmat = c_ref[0]
    zr = x_prev[:, :S5_SW]
    zi = x_prev[:, S5_SW:]
    for j in range(S5_LC):
        zr, zi = cmul(zr, zi, ar, ai)
        yj = _dot(jnp.concatenate([zr, zi], axis=1).astype(BF16), cmat)
        for s in range(j + 1):
            yj = yj + _dot(xs[s], t_ref[0, j - s])
        for hf in range(n_half):
            for c in range(ncb):
                y_half[hf, pl.ds(c * S5_LC + j, bsz, stride=lb), :] = (
                    yj[c * bsz:(c + 1) * bsz, hf * LANES:(hf + 1) * LANES])
    for hf in range(n_half):
        y_ref[:, :, hf * LANES:(hf + 1) * LANES] = y_half[hf].reshape(bsz, lb, LANES)


def _s5_scan(proj3, mats, lb=S5_LB):
    toep, bmat, cmat, avec = mats
    bsz, seqlen, _ = proj3.shape
    return pl.pallas_call(
        functools.partial(_s5_kernel, lb=lb),
        out_shape=jax.ShapeDtypeStruct((bsz, seqlen, MIX_WIDTH), F32),
        grid=(S5_BLOCKS, seqlen // lb),
        in_specs=[pl.BlockSpec((bsz, lb, S5_K), lambda v, l: (0, l, v)),
                  pl.BlockSpec((1, S5_LC, S5_K, S5_K), lambda v, l: (v, 0, 0, 0)),
                  pl.BlockSpec((1, S5_K, 2 * S5_SW), lambda v, l: (v, 0, 0)),
                  pl.BlockSpec((1, 2 * S5_SW, S5_K), lambda v, l: (v, 0, 0)),
                  pl.BlockSpec((1, 4, S5_SW), lambda v, l: (v, 0, 0))],
        out_specs=pl.BlockSpec((bsz, lb, S5_K), lambda v, l: (0, l, v)),
        scratch_shapes=[pltpu.VMEM((bsz * (lb // S5_LC), 2 * S5_SW), F32),
                        pltpu.VMEM((bsz * (lb // S5_LC), 2 * S5_SW), F32),
                        pltpu.VMEM((bsz, 2 * S5_SW), F32),
                        pltpu.VMEM((S5_K // LANES, bsz * lb, LANES), F32),
                        pltpu.VMEM((S5_K // LANES, bsz * lb, LANES), F32)],
        compiler_params=_cparams(("parallel", "arbitrary")),
        name="s5_scan",
    )(proj3, toep, bmat, cmat, avec)


def _s5_post_kernel(y_ref, u_ref, d_ref, w_ref, b_ref, o_ref):
    y = y_ref[...] + d_ref[...] * u_ref[...]
    y = jax.nn.gelu(y)
    o_ref[...] = y * jax.nn.sigmoid(_dot(y.astype(BF16), w_ref[...]) + b_ref[...])


def _s5_post(y_ssm, proj, d_skip, w_glu_bf16, b_glu, tm=ROW_TILE):
    t = y_ssm.shape[0]
    w = MIX_WIDTH
    return pl.pallas_call(
        _s5_post_kernel,
        out_shape=jax.ShapeDtypeStruct((t, w), F32),
        grid=(t // tm,),
        in_specs=[pl.BlockSpec((tm, w), lambda i: (i, 0)),
                  pl.BlockSpec((tm, w), lambda i: (i, 0)),
                  pl.BlockSpec((1, w), lambda i: (0, 0)),
                  pl.BlockSpec((w, w), lambda i: (0, 0)),
                  pl.BlockSpec((1, w), lambda i: (0, 0))],
        out_specs=pl.BlockSpec((tm, w), lambda i: (i, 0)),
        compiler_params=_cparams(("parallel",)),
        name="s5_post",
    )(y_ssm, proj, d_skip.reshape(1, w), w_glu_bf16, b_glu.reshape(1, w))


def _s5_mixer(proj, bsz, seqlen, a_re, a_im, log_dt, b_re, b_im, c_re, c_im, d_skip, w_glu, b_glu):
    mats = _s5_prep(a_re, a_im, log_dt, b_re, b_im, c_re, c_im)
    y = _s5_scan(proj.reshape(bsz, seqlen, proj.shape[1]), mats)
    return _s5_post(y.reshape(bsz * seqlen, MIX_WIDTH), proj, d_skip, w_glu.astype(BF16), b_glu)


def _head_norm_padded(hv, g):
    lane = lax.broadcasted_iota(jnp.int32, hv.shape, 1)
    real = lane < DV
    mu = jnp.sum(hv, axis=-1, keepdims=True) * (1.0 / DV)
    d = jnp.where(real, hv - mu, 0.0)
    var = jnp.sum(d * d, axis=-1, keepdims=True) * (1.0 / DV)
    return d * lax.rsqrt(var + LN_EPS) * g


def _log_sigmoid(x):
    return jnp.minimum(x, 0.0) - jnp.log(1.0 + jnp.exp(-jnp.abs(x)))


def _mlstm_kernel(q_ref, k_ref, v_ref, o_ref, gt_ref, cw_ref, gb_ref, ng_ref,
                  y_ref, cbuf, c_st, n_st, m_st, *, tl):
    i = pl.program_id(1)

    @pl.when(i == 0)
    def _():
        cbuf[0:SUBLANES, :] = jnp.zeros((SUBLANES, 2 * HEADS_QK), F32)
        c_st[...] = jnp.zeros_like(c_st)
        n_st[...] = jnp.zeros_like(n_st)
        m_st[...] = jnp.zeros_like(m_st)

    cbuf[SUBLANES:SUBLANES + tl, 0:HEADS_QK] = q_ref[...]
    cbuf[SUBLANES:SUBLANES + tl, HEADS_QK:2 * HEADS_QK] = k_ref[...]
    acc = jnp.zeros((tl, 2 * HEADS_QK), F32)
    for w in range(ML_CONV):
        acc = acc + cbuf[pl.ds(SUBLANES - (ML_CONV - 1) + w, tl), :] * cw_ref[w:w + 1, :]
    qk = acc * jax.nn.sigmoid(acc)
    cbuf[0:SUBLANES, :] = cbuf[tl:tl + SUBLANES, :]

    gt = gt_ref[...] + gb_ref[...]
    lf = _log_sigmoid(gt)
    gt_t = gt.T
    lf_t = _log_sigmoid(gt_t)

    cl = ML_CHUNK
    row = lax.broadcasted_iota(jnp.int32, (cl, cl), 0)
    col = lax.broadcasted_iota(jnp.int32, (cl, cl), 1)
    tri = row >= col
    scale = DQK ** -0.5
    for cc in range(tl // cl):
        r0 = cc * cl
        for h in range(N_HEADS):
            ig_col = gt[r0:r0 + cl, h:h + 1]
            lf_col = lf[r0:r0 + cl, N_HEADS + h:N_HEADS + h + 1]
            ig_row = gt_t[h:h + 1, r0:r0 + cl]
            lf_row = lf_t[N_HEADS + h:N_HEADS + h + 1, r0:r0 + cl]
            bcum_col = jnp.sum(jnp.where(tri, lf_row, 0.0), axis=1, keepdims=True)
            bcum_row = jnp.sum(jnp.where(col >= row, lf_col, 0.0), axis=0, keepdims=True)
            btot = jnp.sum(lf_row, axis=1, keepdims=True)
            w_row = btot - bcum_row + ig_row
            m_loc = jnp.max(w_row, axis=1, keepdims=True)
            e_col = jnp.exp(btot - bcum_col + ig_col - m_loc)
            m_prev = m_st[h:h + 1, 0:1]
            c_prev = c_st[h]
            n_prev = n_st[h:h + 1, :]
            q = qk[r0:r0 + cl, h * DQK_PAD:(h + 1) * DQK_PAD] * scale
            k = qk[r0:r0 + cl, HEADS_QK + h * DQK_PAD:HEADS_QK + (h + 1) * DQK_PAD]
            v = v_ref[r0:r0 + cl, h * DV_PAD:(h + 1) * DV_PAD]
            qb = q.astype(BF16)
            kb = k.astype(BF16)
            vb = v.astype(BF16)
            dmat = jnp.where(tri, bcum_col - bcum_row + ig_row, -jnp.inf)
            g_col = bcum_col + m_prev
            m_row = jnp.maximum(g_col, jnp.max(dmat, axis=1, keepdims=True))
            inter = jnp.exp(g_col - m_row)
            s_qk = lax.dot_general(qb, kb, _NT, preferred_element_type=F32) * jnp.exp(dmat - m_row)
            num = inter * _dot(qb, c_prev.astype(BF16)) + _dot(s_qk.astype(BF16), vb)
            den = (inter * jnp.sum(q * n_prev, axis=1, keepdims=True)
                   + jnp.sum(s_qk, axis=1, keepdims=True))
            hv = num / jnp.maximum(jnp.abs(den), jnp.exp(-m_row))
            ke = k * e_col
            kv = lax.dot_general(ke.astype(BF16), vb, _TN, preferred_element_type=F32)
            nk = jnp.sum(ke, axis=0, keepdims=True)
            m_new = jnp.maximum(btot + m_prev, m_loc)
            sa = jnp.exp(btot + m_prev - m_new)
            sb = jnp.exp(m_loc - m_new)
            c_st[h] = sa * c_prev + sb * kv
            n_st[h:h + 1, :] = sa * n_prev + sb * nk
            m_st[h:h + 1, :] = jnp.broadcast_to(m_new, (1, LANES))
            hn = _head_norm_padded(hv, ng_ref[0:1, h * DV_PAD:(h + 1) * DV_PAD])
            og = o_ref[r0:r0 + cl, h * DV_PAD:(h + 1) * DV_PAD]
            y_ref[r0:r0 + cl, h * DV_PAD:(h + 1) * DV_PAD] = jax.nn.sigmoid(og) * hn


_ML_GATE_BLK = (2 * HEADS_QK + 2 * HEADS_V + XATTN_WIDTH) // LANES


def _mlstm_mixer(proj, bsz, seqlen, cw, gbias, norm_g, tl=ROW_TILE):
    t = proj.shape[0]
    nl = seqlen // tl
    rows = lambda b, i: b * nl + i
    return pl.pallas_call(
        functools.partial(_mlstm_kernel, tl=tl),
        out_shape=jax.ShapeDtypeStruct((t, HEADS_V), F32),
        grid=(bsz, nl),
        in_specs=[pl.BlockSpec((tl, HEADS_QK), lambda b, i: (rows(b, i), 0)),
                  pl.BlockSpec((tl, HEADS_QK), lambda b, i: (rows(b, i), 1)),
                  pl.BlockSpec((tl, HEADS_V), lambda b, i: (rows(b, i), 1)),
                  pl.BlockSpec((tl, HEADS_V), lambda b, i: (rows(b, i), 2)),
                  pl.BlockSpec((tl, LANES), lambda b, i: (rows(b, i), _ML_GATE_BLK)),
                  pl.BlockSpec((ML_CONV, 2 * HEADS_QK), lambda b, i: (0, 0)),
                  pl.BlockSpec((1, LANES), lambda b, i: (0, 0)),
                  pl.BlockSpec((1, HEADS_V), lambda b, i: (0, 0))],
        out_specs=pl.BlockSpec((tl, HEADS_V), lambda b, i: (rows(b, i), 0)),
        scratch_shapes=[pltpu.VMEM((tl + SUBLANES, 2 * HEADS_QK), F32),
                        pltpu.VMEM((N_HEADS, DQK_PAD, DV_PAD), F32),
                        pltpu.VMEM((SUBLANES, DQK_PAD), F32),
                        pltpu.VMEM((SUBLANES, LANES), F32)],
        compiler_params=_cparams(("parallel", "arbitrary")),
        name="mlstm",
    )(proj, proj, proj, proj, proj, cw, gbias, norm_g)


def _ret_log_gamma(h):
    return float(np.log(np.float32(1.0) - np.power(np.float32(2.0), np.float32(-5.0 - h))))


def _ret_kernel(q_ref, k_ref, v_ref, g_ref, pos_ref, inv_ref, sgn_ref, ng_ref,
                y_ref, s_st, *, tl):
    i = pl.program_id(1)

    @pl.when(i == 0)
    def _():
        s_st[...] = jnp.zeros_like(s_st)

    ang = pos_ref[...] * inv_ref[...]
    cos_t = jnp.cos(ang)
    sin_t = jnp.sin(ang) * sgn_ref[...]
    cl = RET_CHUNK
    row = lax.broadcasted_iota(jnp.int32, (cl, cl), 0)
    col = lax.broadcasted_iota(jnp.int32, (cl, cl), 1)
    rel = (row - col).astype(F32)
    jcol = lax.broadcasted_iota(jnp.int32, (cl, 1), 0).astype(F32)
    kscale = DQK ** -0.5
    for h in range(N_HEADS):
        lg = _ret_log_gamma(h)
        decay = jnp.where(rel >= 0, jnp.exp(jnp.maximum(rel, 0.0) * lg), 0.0)
        zeta = jnp.exp((cl - 1 - jcol) * lg)
        xi = jnp.exp((jcol + 1.0) * lg)
        chunk_decay = float(np.exp(np.float32(cl) * np.float32(lg)))
        qh = q_ref[:, h * DQK_PAD:(h + 1) * DQK_PAD]
        kh = k_ref[:, h * DQK_PAD:(h + 1) * DQK_PAD]
        qh = qh * cos_t + pltpu.roll(qh, DQK_PAD // 2, 1) * sin_t
        kh = (kh * cos_t + pltpu.roll(kh, DQK_PAD // 2, 1) * sin_t) * kscale
        for cc in range(tl // cl):
            r0 = cc * cl
            qb = qh[r0:r0 + cl].astype(BF16)
            k = kh[r0:r0 + cl]
            kb = k.astype(BF16)
            v = v_ref[r0:r0 + cl, h * DV_PAD:(h + 1) * DV_PAD]
            vb = v.astype(BF16)
            s_prev = s_st[h]
            s = lax.dot_general(qb, kb, _NT, preferred_element_type=F32) * decay
            intra = _dot(s.astype(BF16), vb)
            cross = _dot(qb, s_prev.astype(BF16)) * xi
            r = lax.dot_general((k * zeta).astype(BF16), vb, _TN, preferred_element_type=F32)
            s_st[h] = chunk_decay * s_prev + r
            hn = _head_norm_padded(intra + cross, ng_ref[0:1, h * DV_PAD:(h + 1) * DV_PAD])
            gate = g_ref[r0:r0 + cl, h * DV_PAD:(h + 1) * DV_PAD]
            y_ref[r0:r0 + cl, h * DV_PAD:(h + 1) * DV_PAD] = gate * jax.nn.sigmoid(gate) * hn


def _ret_mixer(proj, bsz, seqlen, pos_f, inv_pad, sgn_pad, norm_g, tl=ROW_TILE):
    t = proj.shape[0]
    nl = seqlen // tl
    rows = lambda b, i: b * nl + i
    return pl.pallas_call(
        functools.partial(_ret_kernel, tl=tl),
        out_shape=jax.ShapeDtypeStruct((t, HEADS_V), F32),
        grid=(bsz, nl),
        in_specs=[pl.BlockSpec((tl, HEADS_QK), lambda b, i: (rows(b, i), 0)),
                  pl.BlockSpec((tl, HEADS_QK), lambda b, i: (rows(b, i), 1)),
                  pl.BlockSpec((tl, HEADS_V), lambda b, i: (rows(b, i), 1)),
                  pl.BlockSpec((tl, HEADS_V), lambda b, i: (rows(b, i), 2)),
                  pl.BlockSpec((tl, 1), lambda b, i: (rows(b, i), 0)),
                  pl.BlockSpec((1, DQK_PAD), lambda b, i: (0, 0)),
                  pl.BlockSpec((1, DQK_PAD), lambda b, i: (0, 0)),
                  pl.BlockSpec((1, HEADS_V), lambda b, i: (0, 0))],
        out_specs=pl.BlockSpec((tl, HEADS_V), lambda b, i: (rows(b, i), 0)),
        scratch_shapes=[pltpu.VMEM((N_HEADS, DQK_PAD, DV_PAD), F32)],
        compiler_params=_cparams(("parallel", "arbitrary")),
        name="retention",
    )(proj, proj, proj, proj, pos_f, inv_pad, sgn_pad, norm_g)


def _pack_rows(x):
    half = x.shape[1] // 2
    lo = pltpu.bitcast(x[:, :half].astype(BF16).astype(F32), jnp.uint32)
    hi = pltpu.bitcast(x[:, half:].astype(BF16).astype(F32), jnp.uint32)
    return hi | (lo >> 16)


def _unpack_rows(u):
    lo = pltpu.bitcast(u << 16, F32)
    hi = pltpu.bitcast(u & jnp.uint32(0xFFFF0000), F32)
    return jnp.concatenate([lo, hi], axis=1)


def _post_kernel(ym_ref, xq_ref, h_ref, mk_ref, mv_ref, wom_ref, wox_ref, g_ref, b_ref,
                 rwh_ref, rwl_ref, rb_ref, h1_ref, h1p_ref, idx_ref, gate_ref):
    tl = xq_ref.shape[0]
    xq = xq_ref[...] * (XATTN_HEAD_DIM ** -0.5)
    lane = lax.broadcasted_iota(jnp.int32, (tl, XATTN_WIDTH), 1)
    head = lane // XATTN_HEAD_DIM
    mk = mk_ref[0]
    mv = mv_ref[0]
    ymem = jnp.zeros((tl, XATTN_WIDTH), F32)
    for hh in range(XATTN_HEADS):
        sel = head == hh
        qh = jnp.where(sel, xq, 0.0).astype(BF16)
        s = lax.dot_general(qh, mk, _NT, preferred_element_type=F32)
        s = s - jnp.max(s, axis=-1, keepdims=True)
        p = jnp.exp(s)
        p = p / jnp.sum(p, axis=-1, keepdims=True)
        ymem = jnp.where(sel, _dot(p.astype(BF16), mv), ymem)
    y = _dot(ym_ref[...].astype(BF16), wom_ref[...]) + _dot(ymem.astype(BF16), wox_ref[...])
    h1 = _layer_norm_rows(DEEPNORM_ALPHA * h_ref[...] + y, g_ref[...], b_ref[...])
    h1_ref[...] = h1
    h1p_ref[...] = _pack_rows(h1)
    h_hi = h1.astype(BF16)
    h_lo = (h1 - h_hi.astype(F32)).astype(BF16)
    logits = (_dot(h_hi, rwh_ref[...]) + _dot(h_lo, rwh_ref[...]) + _dot(h_hi, rwl_ref[...])
              + rb_ref[...])
    ln = lax.broadcasted_iota(jnp.int32, logits.shape, 1)
    vals = logits
    tv, ti = [], []
    for _ in range(TOP_K):
        m = jnp.max(vals, axis=-1, keepdims=True)
        ix = jnp.min(jnp.where(vals == m, ln, LANES), axis=-1, keepdims=True)
        tv.append(m)
        ti.append(ix)
        vals = jnp.where(ln == ix, -jnp.inf, vals)
    ex = [jnp.exp(v - tv[0]) for v in tv]
    tot = ex[0] + ex[1] + ex[2] + ex[3]
    idx_out = jnp.zeros(logits.shape, jnp.int32)
    gate_out = jnp.zeros(logits.shape, F32)
    for k in range(TOP_K):
        idx_out = jnp.where(ln == k, ti[k], idx_out)
        gate_out = jnp.where(ln == k, ex[k] / tot, gate_out)
    idx_ref[...] = idx_out
    gate_ref[...] = gate_out


def _post_mixer(y_mix, proj, xq_blk, h, mem_k, mem_v, wo_mix, wo_mem, ln_g, ln_b, rw, rb,
                seqlen, tl=4 * ROW_TILE):
    t, cm = y_mix.shape
    nl = seqlen // tl
    d = D_MODEL
    rw_hi = rw.astype(BF16)
    rw_lo = (rw - rw_hi.astype(F32)).astype(BF16)
    full = lambda a, b: pl.BlockSpec((a, b), lambda i: (0, 0))
    return pl.pallas_call(
        _post_kernel,
        out_shape=(jax.ShapeDtypeStruct((t, d), F32),
                   jax.ShapeDtypeStruct((t, d // 2), jnp.uint32),
                   jax.ShapeDtypeStruct((t, LANES), jnp.int32),
                   jax.ShapeDtypeStruct((t, LANES), F32)),
        grid=(t // tl,),
        in_specs=[pl.BlockSpec((tl, cm), lambda i: (i, 0)),
                  pl.BlockSpec((tl, XATTN_WIDTH), lambda i: (i, xq_blk)),
                  pl.BlockSpec((tl, d), lambda i: (i, 0)),
                  pl.BlockSpec((1, N_MEM, XATTN_WIDTH), lambda i: (i // nl, 0, 0)),
                  pl.BlockSpec((1, N_MEM, XATTN_WIDTH), lambda i: (i // nl, 0, 0)),
                  full(cm, d), full(XATTN_WIDTH, d), full(1, d), full(1, d),
                  full(d, LANES), full(d, LANES), full(1, LANES)],
        out_specs=(pl.BlockSpec((tl, d), lambda i: (i, 0)),
                   pl.BlockSpec((tl, d // 2), lambda i: (i, 0)),
                   pl.BlockSpec((tl, LANES), lambda i: (i, 0)),
                   pl.BlockSpec((tl, LANES), lambda i: (i, 0))),
        compiler_params=_cparams(("parallel",)),
        name="post_mixer",
    )(y_mix, proj, h, mem_k, mem_v, wo_mix, wo_mem, ln_g, ln_b, rw_hi, rw_lo, rb)


SC_CORES = 2
SC_SUBCORES = 16
SC_WORKERS = SC_CORES * SC_SUBCORES
SC_CHUNK = 64


def _sc_gather(table, idx):
    v, d = table.shape
    b = idx.shape[0]
    per_w = b // SC_WORKERS
    n_chunks = per_w // SC_CHUNK
    assert per_w * SC_WORKERS == b and n_chunks * SC_CHUNK == per_w and n_chunks % 2 == 0
    mesh = plsc.VectorSubcoreMesh(core_axis_name="c", subcore_axis_name="s")

    @functools.partial(
        pl.kernel, mesh=mesh,
        out_type=jax.ShapeDtypeStruct((b, d), table.dtype),
        scratch_types=[pltpu.VMEM((per_w,), jnp.int32),
                       pltpu.VMEM((SC_CHUNK, d), table.dtype),
                       pltpu.VMEM((SC_CHUNK, d), table.dtype),
                       pltpu.SemaphoreType.DMA,
                       pltpu.SemaphoreType.DMA],
    )
    def gather_kernel(table_hbm, idx_hbm, out_hbm, idx_v, rows0, rows1, sem0, sem1):
        wid = lax.axis_index("s") * SC_CORES + lax.axis_index("c")
        base = wid * per_w
        pltpu.sync_copy(idx_hbm.at[pl.ds(pl.multiple_of(base, 8), per_w)], idx_v)
        ring = ((rows0, sem0), (rows1, sem1))

        def gather(c, buf, sem):
            rows = idx_v.at[pl.ds(pl.multiple_of(c * SC_CHUNK, 8), SC_CHUNK)]
            return pltpu.make_async_copy(table_hbm.at[rows], buf, sem)

        for c0, (buf, sem) in enumerate(ring):
            gather(c0, buf, sem).start()

        @pl.loop(0, n_chunks, step=2)
        def _(c):
            for k, (buf, sem) in enumerate(ring):
                cc = c + k
                gather(cc, buf, sem).wait()
                off = pl.multiple_of(base + cc * SC_CHUNK, 8)
                pltpu.sync_copy(buf, out_hbm.at[pl.ds(off, SC_CHUNK)])

                @pl.when(cc + 2 < n_chunks)
                def _():
                    gather(cc + 2, buf, sem).start()

    return gather_kernel(table, idx)


def _sc_dispatch(x, pos_kmajor, n_rows):
    t, d = x.shape
    per_w = t // SC_WORKERS
    n_chunks = per_w // SC_CHUNK
    assert per_w * SC_WORKERS == t and n_chunks * SC_CHUNK == per_w
    mesh = plsc.VectorSubcoreMesh(core_axis_name="c", subcore_axis_name="s")

    @functools.partial(
        pl.kernel, mesh=mesh,
        out_type=jax.ShapeDtypeStruct((n_rows, d), x.dtype),
        scratch_types=[pltpu.VMEM((SC_CHUNK,), jnp.int32),
                       pltpu.VMEM((SC_CHUNK, d), x.dtype)],
    )
    def dispatch_kernel(x_hbm, pos_hbm, out_hbm, idx_v, rows_v):
        wid = lax.axis_index("s") * SC_CORES + lax.axis_index("c")
        base = wid * per_w

        @pl.loop(0, n_chunks)
        def _(j):
            off = pl.multiple_of(base + j * SC_CHUNK, 8)
            pltpu.sync_copy(x_hbm.at[pl.ds(off, SC_CHUNK)], rows_v)
            for k in range(TOP_K):
                pltpu.sync_copy(pos_hbm.at[pl.ds(pl.multiple_of(k * t + off, 8), SC_CHUNK)], idx_v)
                pltpu.sync_copy(rows_v, out_hbm.at[idx_v])

    return dispatch_kernel(x, pos_kmajor)


def _expert_kernel(bexp_ref, nvalid_ref, nused_ref, x_ref, wgu_ref, bgu_ref, wd_ref, bd_ref, y_ref,
                   wgu_bf, wd_bf):
    i = pl.program_id(0)
    de = wd_ref.shape[2]

    @pl.when(i < nused_ref[0])
    def _():
        prev = bexp_ref[jnp.maximum(i - 1, 0)]

        @pl.when((i == 0) | (prev != bexp_ref[i]))
        def _():
            wgu_bf[...] = wgu_ref[0, 0].astype(BF16)
            wd_bf[...] = wd_ref[0, 0].astype(BF16)

        rows = lax.broadcasted_iota(jnp.int32, x_ref.shape, 0)
        xb = _unpack_rows(jnp.where(rows < nvalid_ref[i], x_ref[...], jnp.uint32(0))).astype(BF16)
        gu = _dot(xb, wgu_bf[...]) + bgu_ref[0, 0]
        x_glu = jnp.minimum(gu[:, :de], SWIGLU_LIMIT)
        x_lin = jnp.clip(gu[:, de:], -SWIGLU_LIMIT, SWIGLU_LIMIT)
        act = x_glu * jax.nn.sigmoid(SWIGLU_ALPHA * x_glu) * (x_lin + 1.0)
        y_ref[...] = _pack_rows(_dot(act.astype(BF16), wd_bf[...]) + bd_ref[0, 0])

    @pl.when(i >= nused_ref[0])
    def _():
        y_ref[...] = jnp.zeros_like(y_ref)


def _experts(xr, block_exp, n_valid, n_used, layer, w_gu, b_gu, w_down, b_down, bm=MOE_BM):
    n_rows, dp = xr.shape
    d = 2 * dp
    n_blocks = n_rows // bm
    nl, ne, _, de2 = w_gu.shape
    de = de2 // 2
    row_blk = lambda i, nu: jnp.minimum(i, nu[0] - 1)
    grid_spec = pltpu.PrefetchScalarGridSpec(
        num_scalar_prefetch=3,
        grid=(n_blocks,),
        in_specs=[pl.BlockSpec((bm, dp), lambda i, be, nv, nu: (row_blk(i, nu), 0)),
                  pl.BlockSpec((1, 1, d, de2), lambda i, be, nv, nu: (layer, be[i], 0, 0)),
                  pl.BlockSpec((1, 1, 1, de2), lambda i, be, nv, nu: (layer, be[i], 0, 0)),
                  pl.BlockSpec((1, 1, de, d), lambda i, be, nv, nu: (layer, be[i], 0, 0)),
                  pl.BlockSpec((1, 1, 1, d), lambda i, be, nv, nu: (layer, be[i], 0, 0))],
        out_specs=pl.BlockSpec((bm, dp), lambda i, be, nv, nu: (i, 0)),
        scratch_shapes=[pltpu.VMEM((d, de2), BF16),
                        pltpu.VMEM((de, d), BF16)],
    )
    return pl.pallas_call(
        _expert_kernel,
        out_shape=jax.ShapeDtypeStruct((n_rows, dp), jnp.uint32),
        grid_spec=grid_spec,
        compiler_params=_cparams(("arbitrary",)),
        name="experts",
    )(block_exp, n_valid, n_used, xr, w_gu, b_gu.reshape(nl, ne, 1, de2), w_down,
      b_down.reshape(nl, ne, 1, d))


def _combine_kernel(y0_ref, y1_ref, y2_ref, y3_ref, h1_ref, gate_ref, g_ref, b_ref, o_ref):
    gate = gate_ref[...]
    acc = DEEPNORM_ALPHA * h1_ref[...]
    for k, y_ref in enumerate((y0_ref, y1_ref, y2_ref, y3_ref)):
        acc = acc + gate[:, k:k + 1] * _unpack_rows(y_ref[0])
    o_ref[...] = _layer_norm_rows(acc, g_ref[...], b_ref[...])


def _combine(yg, h1, gates, ln_g, ln_b, tl=ROW_TILE):
    t, d = h1.shape
    ysel = lambda k: pl.BlockSpec((1, tl, d // 2), lambda i: (k, i, 0))
    return pl.pallas_call(
        _combine_kernel,
        out_shape=jax.ShapeDtypeStruct((t, d), F32),
        grid=(t // tl,),
        in_specs=[ysel(0), ysel(1), ysel(2), ysel(3),
                  pl.BlockSpec((tl, d), lambda i: (i, 0)),
                  pl.BlockSpec((tl, LANES), lambda i: (i, 0)),
                  pl.BlockSpec((1, d), lambda i: (0, 0)),
                  pl.BlockSpec((1, d), lambda i: (0, 0))],
        out_specs=pl.BlockSpec((tl, d), lambda i: (i, 0)),
        compiler_params=_cparams(("parallel",)),
        name="combine",
    )(yg, yg, yg, yg, h1, gates, ln_g, ln_b)


def _route(idx, bm=MOE_BM):
    t = idx.shape[0]
    n_assign = t * TOP_K
    n_rows = (-(-n_assign // bm) + N_EXPERTS) * bm
    n_blocks = n_rows // bm
    onehot = (idx[:, :, None] == jnp.arange(N_EXPERTS, dtype=jnp.int32)[None, None, :])
    sel = jnp.sum(onehot.astype(jnp.int32), axis=1)
    csum = jnp.cumsum(sel, axis=0)
    counts = csum[-1]
    rank = csum - sel
    padded = ((counts + bm - 1) // bm) * bm
    pad_end = jnp.cumsum(padded)
    pad_start = pad_end - padded
    base = pad_start[None, :] + rank
    pos = jnp.sum(jnp.where(onehot, base[:, None, :], 0), axis=2)
    block_start = jnp.arange(n_blocks, dtype=jnp.int32) * bm
    block_exp = jnp.minimum(
        jnp.sum((block_start[:, None] >= pad_end[None, :]).astype(jnp.int32), axis=1),
        N_EXPERTS - 1).astype(jnp.int32)
    own = block_exp[:, None] == jnp.arange(N_EXPERTS, dtype=jnp.int32)[None, :]
    filled_end = jnp.sum(jnp.where(own, (pad_start + counts)[None, :], 0), axis=1)
    n_valid = jnp.clip(filled_end - block_start, 0, bm).astype(jnp.int32)
    n_used = (pad_end[-1] // bm).astype(jnp.int32).reshape(1)
    return pos.T.reshape(-1).astype(jnp.int32), block_exp, n_valid, n_used, n_rows


def _take_cols(w, cols):
    cols = np.asarray(cols, np.int32)
    out = jnp.take(w, jnp.asarray(np.maximum(cols, 0)), axis=-1)
    return jnp.where(jnp.asarray(cols >= 0), out, 0.0)


def _head_cols(offset, width, pad):
    cols = []
    for h in range(N_HEADS):
        cols += list(range(offset + h * width, offset + (h + 1) * width)) + [-1] * (pad - width)
    return cols


def _rope_head_cols(offset):
    half = DQK // 2
    slot = DQK_PAD // 2
    cols = []
    for h in range(N_HEADS):
        b = offset + h * DQK
        cols += list(range(b, b + half)) + [-1] * (slot - half)
        cols += list(range(b + half, b + DQK)) + [-1] * (slot - half)
    return cols


_ML_GATE_OFF = 2 * ML_QK + 2 * MIX_WIDTH
_ML_COLS = (_head_cols(0, DQK, DQK_PAD) + _head_cols(ML_QK, DQK, DQK_PAD)
            + _head_cols(2 * ML_QK, DV, DV_PAD) + _head_cols(2 * ML_QK + MIX_WIDTH, DV, DV_PAD)
            + list(range(_ML_GATE_OFF + 2 * N_HEADS, _ML_GATE_OFF + 2 * N_HEADS + XATTN_WIDTH))
            + list(range(_ML_GATE_OFF, _ML_GATE_OFF + 2 * N_HEADS)) + [-1] * (LANES - 2 * N_HEADS))
_RET_COLS = (_rope_head_cols(0) + _rope_head_cols(ML_QK)
             + _head_cols(2 * ML_QK, DV, DV_PAD) + _head_cols(2 * ML_QK + MIX_WIDTH, DV, DV_PAD)
             + list(range(2 * ML_QK + 2 * MIX_WIDTH, 2 * ML_QK + 2 * MIX_WIDTH + XATTN_WIDTH)))
_MIX_PAD_COLS = _head_cols(0, DV, DV_PAD)
_XQ_BLK_PADDED = (2 * HEADS_QK + 2 * HEADS_V) // XATTN_WIDTH
_XQ_BLK_S5 = MIX_WIDTH // XATTN_WIDTH


def kernel(x, mem, positions, mem_w_k, mem_w_v, l0_w_in, l0_s5_a_re, l0_s5_a_im, l0_s5_log_dt, l0_s5_b_re, l0_s5_b_im, l0_s5_c_re, l0_s5_c_im, l0_s5_d, l0_s5_w_glu, l0_s5_b_glu, l1_w_in, l1_ml_conv_q, l1_ml_conv_k, l1_ml_b_i, l1_ml_b_f, l1_ml_norm_g, l2_w_in, l2_ret_norm_g, l3_w_in, l3_s5_a_re, l3_s5_a_im, l3_s5_log_dt, l3_s5_b_re, l3_s5_b_im, l3_s5_c_re, l3_s5_c_im, l3_s5_d, l3_s5_w_glu, l3_s5_b_glu, w_out, ln1_g, ln1_b, ln2_g, ln2_b, router_w, router_b, exp_w_gu, exp_b_gu, exp_w_down, exp_b_down):
    bsz, seqlen, d = x.shape
    t = bsz * seqlen
    h = x.reshape(t, d)

    w_kv = jnp.concatenate([mem_w_k, mem_w_v], axis=1).astype(BF16)
    kv = _inproj(mem.reshape(bsz * N_MEM, d), w_kv).astype(BF16)
    mem_k = kv[:, :XATTN_WIDTH].reshape(bsz, N_MEM, XATTN_WIDTH)
    mem_v = kv[:, XATTN_WIDTH:].reshape(bsz, N_MEM, XATTN_WIDTH)

    half = DQK // 2
    inv = ROPE_BASE ** (-jnp.arange(0, DQK, 2, dtype=F32) / DQK)
    zpad = jnp.zeros((DQK_PAD // 2 - half,), F32)
    inv_pad = jnp.concatenate([inv, zpad, inv, zpad]).reshape(1, DQK_PAD)
    sgn_pad = jnp.concatenate([-jnp.ones((half,), F32), zpad, jnp.ones((half,), F32), zpad]
                              ).reshape(1, DQK_PAD)
    pos_f = positions.astype(F32).reshape(t, 1)

    s5_params = {
        0: (l0_s5_a_re, l0_s5_a_im, l0_s5_log_dt, l0_s5_b_re, l0_s5_b_im, l0_s5_c_re, l0_s5_c_im,
            l0_s5_d, l0_s5_w_glu, l0_s5_b_glu),
        3: (l3_s5_a_re, l3_s5_a_im, l3_s5_log_dt, l3_s5_b_re, l3_s5_b_im, l3_s5_c_re, l3_s5_c_im,
            l3_s5_d, l3_s5_w_glu, l3_s5_b_glu),
    }
    w_ins = (l0_w_in, l1_w_in, l2_w_in, l3_w_in)

    for i in range(DEPTH):
        kind = i % 3
        wo = w_out[i]
        if kind == 0:
            proj = _inproj(h, w_ins[i].astype(BF16))
            y_mix = _s5_mixer(proj, bsz, seqlen, *s5_params[i])
            wo_mix = wo[:MIX_WIDTH].astype(BF16)
            xq_blk = _XQ_BLK_S5
        elif kind == 1:
            proj = _inproj(h, _take_cols(w_ins[i], _ML_COLS).astype(BF16))
            cw = jnp.concatenate([_take_cols(l1_ml_conv_q, _head_cols(0, DQK, DQK_PAD)),
                                  _take_cols(l1_ml_conv_k, _head_cols(0, DQK, DQK_PAD))], axis=1)
            gbias = jnp.concatenate([l1_ml_b_i, l1_ml_b_f,
                                     jnp.zeros((LANES - 2 * N_HEADS,), F32)]).reshape(1, LANES)
            norm_g = _take_cols(l1_ml_norm_g, _MIX_PAD_COLS).reshape(1, HEADS_V)
            y_mix = _mlstm_mixer(proj, bsz, seqlen, cw, gbias, norm_g)
            wo_mix = _take_cols(wo[:MIX_WIDTH].T, _MIX_PAD_COLS).T.astype(BF16)
            xq_blk = _XQ_BLK_PADDED
        else:
            proj = _inproj(h, _take_cols(w_ins[i], _RET_COLS).astype(BF16))
            norm_g = _take_cols(l2_ret_norm_g, _MIX_PAD_COLS).reshape(1, HEADS_V)
            y_mix = _ret_mixer(proj, bsz, seqlen, pos_f, inv_pad, sgn_pad, norm_g)
            wo_mix = _take_cols(wo[:MIX_WIDTH].T, _MIX_PAD_COLS).T.astype(BF16)
            xq_blk = _XQ_BLK_PADDED
        wo_mem = wo[MIX_WIDTH:].astype(BF16)
        rw = jnp.pad(router_w[i], ((0, 0), (0, LANES - N_EXPERTS)))
        rb = jnp.concatenate([router_b[i], jnp.full((LANES - N_EXPERTS,), -1e30, F32)]
                             ).reshape(1, LANES)
        h1, h1p, idx, gates = _post_mixer(y_mix, proj, xq_blk, h, mem_k, mem_v, wo_mix, wo_mem,
                                     ln1_g[i].reshape(1, d), ln1_b[i].reshape(1, d), rw, rb, seqlen)
        pos_kmajor, block_exp, n_valid, n_used, n_rows = _route(idx[:, :TOP_K])
        xr = _sc_dispatch(h1p, pos_kmajor, n_rows)
        yr = _experts(xr, block_exp, n_valid, n_used, i, exp_w_gu, exp_b_gu, exp_w_down, exp_b_down)
        yg = _sc_gather(yr, pos_kmajor).reshape(TOP_K, t, d // 2)
        h = _combine(yg, h1, gates, ln2_g[i].reshape(1, d), ln2_b[i].reshape(1, d))
    return h.reshape(bsz, seqlen, d)
```

```python
import functools

import numpy as np
import jax
import jax.numpy as jnp
from jax import lax
from jax.experimental import pallas as pl
from jax.experimental.pallas import tpu as pltpu
from jax.experimental.pallas import tpu_sc as plsc

F32 = jnp.float32
BF16 = jnp.bfloat16

D_MODEL = 1024
DEPTH = 4
N_MEM = 256
MIX_WIDTH = 768
XATTN_HEADS = 4
XATTN_WIDTH = 256
XATTN_HEAD_DIM = 64
S5_GROUP = 16
S5_GROUPS = 48
S5_STATE = 64
N_HEADS = 4
DQK = 96
DV = 192
ML_QK = 384
ML_CONV = 4
ML_CHUNK = 64
RET_CHUNK = 128
ROPE_BASE = 10000.0
N_EXPERTS = 32
TOP_K = 4
SWIGLU_LIMIT = 7.0
SWIGLU_ALPHA = 1.702
DEEPNORM_ALPHA = (2.0 * DEPTH) ** 0.25
LN_EPS = 1e-5

LANES = 128
SUBLANES = 8
DQK_PAD = 128
DV_PAD = 256
HEADS_QK = N_HEADS * DQK_PAD
HEADS_V = N_HEADS * DV_PAD
S5_LC = 16
S5_K = S5_LC * S5_GROUP
MOE_BM = 512
ROW_TILE = 256
VMEM_LIMIT = 56 * 1024 * 1024

_NT = (((1,), (1,)), ((), ()))
_TN = (((0,), (0,)), ((), ()))


def _cparams(sem):
    return pltpu.CompilerParams(dimension_semantics=sem, vmem_limit_bytes=VMEM_LIMIT)


def _dot(a, b):
    return jnp.dot(a, b, preferred_element_type=F32)


def _layer_norm_rows(z, g, b):
    mu = jnp.mean(z, axis=-1, keepdims=True)
    d = z - mu
    var = jnp.mean(d * d, axis=-1, keepdims=True)
    return d * lax.rsqrt(var + LN_EPS) * g + b


def _inproj_kernel(x_ref, w_ref, o_ref):
    xb = x_ref[...].astype(BF16)
    n = o_ref.shape[1]
    step = 512
    for c0 in range(0, n, step):
        c1 = min(c0 + step, n)
        o_ref[:, c0:c1] = _dot(xb, w_ref[:, c0:c1])


def _inproj(x, w_bf16, tm=ROW_TILE):
    t, d = x.shape
    n = w_bf16.shape[1]
    return pl.pallas_call(
        _inproj_kernel,
        out_shape=jax.ShapeDtypeStruct((t, n), F32),
        grid=(t // tm,),
        in_specs=[pl.BlockSpec((tm, d), lambda i: (i, 0)),
                  pl.BlockSpec((d, n), lambda i: (0, 0))],
        out_specs=pl.BlockSpec((tm, n), lambda i: (i, 0)),
        compiler_params=_cparams(("parallel",)),
        name="inproj",
    )(x, w_bf16)


S5_LANE_GROUPS = S5_K // S5_GROUP
S5_BLOCKS = MIX_WIDTH // S5_K
S5_SW = S5_LANE_GROUPS * S5_STATE
S5_LB = 512


def _block_diag(m):
    nb, g, a, b = m.shape
    eye = jnp.eye(g, dtype=m.dtype)
    return (m[:, :, :, None, :] * eye[None, :, None, :, None]).reshape(nb, g * a, g * b)


def _s5_prep(a_re, a_im, log_dt, b_re, b_im, c_re, c_im):
    hp = lax.Precision.HIGHEST
    lam_re = jnp.minimum(a_re.astype(F32), -1e-4)
    lam_im = a_im.astype(F32)
    dt = jnp.exp(log_dt.astype(F32))[:, None]
    mag = jnp.exp(dt * lam_re)
    ab_re = mag * jnp.cos(dt * lam_im)
    ab_im = mag * jnp.sin(dt * lam_im)
    den = lam_re * lam_re + lam_im * lam_im
    num_re = ab_re - 1.0
    coef_re = (num_re * lam_re + ab_im * lam_im) / den
    coef_im = (ab_im * lam_re - num_re * lam_im) / den
    bre = b_re.astype(F32)
    bim = b_im.astype(F32)
    bb_re = coef_re[..., None] * bre - coef_im[..., None] * bim
    bb_im = coef_re[..., None] * bim + coef_im[..., None] * bre
    pr = [jnp.ones_like(ab_re)]
    pi = [jnp.zeros_like(ab_im)]
    for _ in range(S5_LC):
        r, i = pr[-1], pi[-1]
        pr.append(r * ab_re - i * ab_im)
        pi.append(r * ab_im + i * ab_re)
    pw_re = jnp.stack(pr)
    pw_im = jnp.stack(pi)
    p_re = pw_re[:S5_LC, :, :, None] * bb_re[None] - pw_im[:S5_LC, :, :, None] * bb_im[None]
    p_im = pw_re[:S5_LC, :, :, None] * bb_im[None] + pw_im[:S5_LC, :, :, None] * bb_re[None]
    cre = c_re.astype(F32)
    cim = c_im.astype(F32)
    kmat = (jnp.einsum('ghp,tgpk->tgkh', cre, p_re, precision=hp)
            - jnp.einsum('ghp,tgpk->tgkh', cim, p_im, precision=hp))
    nb, lg = S5_BLOCKS, S5_LANE_GROUPS
    toep = _block_diag(kmat.astype(BF16).reshape(S5_LC * nb, lg, S5_GROUP, S5_GROUP)
                       ).reshape(S5_LC, nb, S5_K, S5_K).transpose(1, 0, 2, 3)
    bmat = jnp.concatenate(
        [_block_diag(bb_re.transpose(0, 2, 1).reshape(nb, lg, S5_GROUP, S5_STATE)),
         _block_diag(bb_im.transpose(0, 2, 1).reshape(nb, lg, S5_GROUP, S5_STATE))], axis=2)
    cmat = jnp.concatenate(
        [_block_diag(cre.transpose(0, 2, 1).reshape(nb, lg, S5_STATE, S5_GROUP)),
         _block_diag(-cim.transpose(0, 2, 1).reshape(nb, lg, S5_STATE, S5_GROUP))], axis=1)
    lane = lambda v: v.reshape(nb, 1, S5_SW)
    avec = jnp.concatenate([lane(ab_re), lane(ab_im), lane(pw_re[S5_LC]), lane(pw_im[S5_LC])],
                           axis=1)
    return toep.astype(BF16), bmat.astype(BF16), cmat.astype(BF16), avec


def _s5_kernel(u_ref, t_ref, b_ref, c_ref, a_ref, y_ref, s_acc, x_prev, x_carry, u_half, y_half,
               *, lb):
    li = pl.program_id(1)
    bsz = u_ref.shape[0]
    ncb = lb // S5_LC
    m = bsz * ncb

    @pl.when(li == 0)
    def _():
        x_carry[...] = jnp.zeros_like(x_carry)

    ar = a_ref[0, 0:1, :]
    ai = a_ref[0, 1:2, :]
    alr = a_ref[0, 2:3, :]
    ali = a_ref[0, 3:4, :]

    def cmul(zr, zi, wr, wi):
        return wr * zr - wi * zi, wr * zi + wi * zr

    n_half = S5_K // LANES
    for hf in range(n_half):
        u_half[hf] = u_ref[:, :, hf * LANES:(hf + 1) * LANES].reshape(bsz * lb, LANES)
    xs = []
    for s in range(S5_LC):
        halves = [jnp.concatenate([u_half[hf, pl.ds(c * S5_LC + s, bsz, stride=lb), :]
                                   for c in range(ncb)], axis=0) for hf in range(n_half)]
        xs.append(jnp.concatenate(halves, axis=1).astype(BF16))
    bmat = b_ref[0]
    sr = jnp.zeros((m, S5_SW), F32)
    si = jnp.zeros((m, S5_SW), F32)
    for s in range(S5_LC):
        bu = _dot(xs[s], bmat)
        sr, si = cmul(sr, si, ar, ai)
        sr = sr + bu[:, :S5_SW]
        si = si + bu[:, S5_SW:]
    s_acc[:, :S5_SW] = sr
    s_acc[:, S5_SW:] = si

    xr = x_carry[:, :S5_SW]
    xi = x_carry[:, S5_SW:]
    for c in range(ncb):
        x_prev[c * bsz:(c + 1) * bsz, :S5_SW] = xr
        x_prev[c * bsz:(c + 1) * bsz, S5_SW:] = xi
        loc = s_acc[c * bsz:(c + 1) * bsz, :]
        xr, xi = cmul(xr, xi, alr, ali)
        xr = xr + loc[:, :S5_SW]
        xi = xi + loc[:, S5_SW:]
    x_carry[:, :S5_SW] = xr
    x_carry[:, S5_SW:] = xi

    cmat = c_ref[0]
    zr = x_prev[:, :S5_SW]
    zi = x_prev[:, S5_SW:]
    for j in range(S5_LC):
        zr, zi = cmul(zr, zi, ar, ai)
        yj = _dot(jnp.concatenate([zr, zi], axis=1).astype(BF16), cmat)
        for s in range(j + 1):
            yj = yj + _dot(xs[s], t_ref[0, j - s])
        for hf in range(n_half):
            for c in range(ncb):
                y_half[hf, pl.ds(c * S5_LC + j, bsz, stride=lb), :] = (
                    yj[c * bsz:(c + 1) * bsz, hf * LANES:(hf + 1) * LANES])
    for hf in range(n_half):
        y_ref[:, :, hf * LANES:(hf + 1) * LANES] = y_half[hf].reshape(bsz, lb, LANES)


def _s5_scan(proj3, mats, lb=S5_LB):
    toep, bmat, cmat, avec = mats
    bsz, seqlen, _ = proj3.shape
    return pl.pallas_call(
        functools.partial(_s5_kernel, lb=lb),
        out_shape=jax.ShapeDtypeStruct((bsz, seqlen, MIX_WIDTH), F32),
        grid=(S5_BLOCKS, seqlen // lb),
        in_specs=[pl.BlockSpec((bsz, lb, S5_K), lambda v, l: (0, l, v)),
                  pl.BlockSpec((1, S5_LC, S5_K, S5_K), lambda v, l: (v, 0, 0, 0)),
                  pl.BlockSpec((1, S5_K, 2 * S5_SW), lambda v, l: (v, 0, 0)),
                  pl.BlockSpec((1, 2 * S5_SW, S5_K), lambda v, l: (v, 0, 0)),
                  pl.BlockSpec((1, 4, S5_SW), lambda v, l: (v, 0, 0))],
        out_specs=pl.BlockSpec((bsz, lb, S5_K), lambda v, l: (0, l, v)),
        scratch_shapes=[pltpu.VMEM((bsz * (lb // S5_LC), 2 * S5_SW), F32),
                        pltpu.VMEM((bsz * (lb // S5_LC), 2 * S5_SW), F32),
                        pltpu.VMEM((bsz, 2 * S5_SW), F32),
                        pltpu.VMEM((S5_K // LANES, bsz * lb, LANES), F32),
                        pltpu.VMEM((S5_K // LANES, bsz * lb, LANES), F32)],
        compiler_params=_cparams(("parallel", "arbitrary")),
        name="s5_scan",
    )(proj3, toep, bmat, cmat, avec)


def _s5_post_kernel(y_ref, u_ref, d_ref, w_ref, b_ref, o_ref):
    y = y_ref[...] + d_ref[...] * u_ref[...]
    y = jax.nn.gelu(y)
    o_ref[...] = y * jax.nn.sigmoid(_dot(y.astype(BF16), w_ref[...]) + b_ref[...])


def _s5_post(y_ssm, proj, d_skip, w_glu_bf16, b_glu, tm=ROW_TILE):
    t = y_ssm.shape[0]
    w = MIX_WIDTH
    return pl.pallas_call(
        _s5_post_kernel,
        out_shape=jax.ShapeDtypeStruct((t, w), F32),
        grid=(t // tm,),
        in_specs=[pl.BlockSpec((tm, w), lambda i: (i, 0)),
                  pl.BlockSpec((tm, w), lambda i: (i, 0)),
                  pl.BlockSpec((1, w), lambda i: (0, 0)),
                  pl.BlockSpec((w, w), lambda i: (0, 0)),
                  pl.BlockSpec((1, w), lambda i: (0, 0))],
        out_specs=pl.BlockSpec((tm, w), lambda i: (i, 0)),
        compiler_params=_cparams(("parallel",)),
        name="s5_post",
    )(y_ssm, proj, d_skip.reshape(1, w), w_glu_bf16, b_glu.reshape(1, w))


def _s5_mixer(proj, bsz, seqlen, a_re, a_im, log_dt, b_re, b_im, c_re, c_im, d_skip, w_glu, b_glu):
    mats = _s5_prep(a_re, a_im, log_dt, b_re, b_im, c_re, c_im)
    y = _s5_scan(proj.reshape(bsz, seqlen, proj.shape[1]), mats)
    return _s5_post(y.reshape(bsz * seqlen, MIX_WIDTH), proj, d_skip, w_glu.astype(BF16), b_glu)


def _head_norm_padded(hv, g):
    lane = lax.broadcasted_iota(jnp.int32, hv.shape, 1)
    real = lane < DV
    mu = jnp.sum(hv, axis=-1, keepdims=True) * (1.0 / DV)
    d = jnp.where(real, hv - mu, 0.0)
    var = jnp.sum(d * d, axis=-1, keepdims=True) * (1.0 / DV)
    return d * lax.rsqrt(var + LN_EPS) * g


def _log_sigmoid(x):
    return jnp.minimum(x, 0.0) - jnp.log(1.0 + jnp.exp(-jnp.abs(x)))


def _mlstm_kernel(q_ref, k_ref, v_ref, o_ref, gt_ref, cw_ref, gb_ref, ng_ref,
                  y_ref, cbuf, c_st, n_st, m_st, *, tl):
    i = pl.program_id(1)

    @pl.when(i == 0)
    def _():
        cbuf[0:SUBLANES, :] = jnp.zeros((SUBLANES, 2 * HEADS_QK), F32)
        c_st[...] = jnp.zeros_like(c_st)
        n_st[...] = jnp.zeros_like(n_st)
        m_st[...] = jnp.zeros_like(m_st)

    cbuf[SUBLANES:SUBLANES + tl, 0:HEADS_QK] = q_ref[...]
    cbuf[SUBLANES:SUBLANES + tl, HEADS_QK:2 * HEADS_QK] = k_ref[...]
    acc = jnp.zeros((tl, 2 * HEADS_QK), F32)
    for w in range(ML_CONV):
        acc = acc + cbuf[pl.ds(SUBLANES - (ML_CONV - 1) + w, tl), :] * cw_ref[w:w + 1, :]
    qk = acc * jax.nn.sigmoid(acc)
    cbuf[0:SUBLANES, :] = cbuf[tl:tl + SUBLANES, :]

    gt = gt_ref[...] + gb_ref[...]
    lf = _log_sigmoid(gt)
    gt_t = gt.T
    lf_t = _log_sigmoid(gt_t)

    cl = ML_CHUNK
    row = lax.broadcasted_iota(jnp.int32, (cl, cl), 0)
    col = lax.broadcasted_iota(jnp.int32, (cl, cl), 1)
    tri = row >= col
    scale = DQK ** -0.5
    for cc in range(tl // cl):
        r0 = cc * cl
        for h in range(N_HEADS):
            ig_col = gt[r0:r0 + cl, h:h + 1]
            lf_col = lf[r0:r0 + cl, N_HEADS + h:N_HEADS + h + 1]
            ig_row = gt_t[h:h + 1, r0:r0 + cl]
            lf_row = lf_t[N_HEADS + h:N_HEADS + h + 1, r0:r0 + cl]
            bcum_col = jnp.sum(jnp.where(tri, lf_row, 0.0), axis=1, keepdims=True)
            bcum_row = jnp.sum(jnp.where(col >= row, lf_col, 0.0), axis=0, keepdims=True)
            btot = jnp.sum(lf_row, axis=1, keepdims=True)
            w_row = btot - bcum_row + ig_row
            m_loc = jnp.max(w_row, axis=1, keepdims=True)
            e_col = jnp.exp(btot - bcum_col + ig_col - m_loc)
            m_prev = m_st[h:h + 1, 0:1]
            c_prev = c_st[h]
            n_prev = n_st[h:h + 1, :]
            q = qk[r0:r0 + cl, h * DQK_PAD:(h + 1) * DQK_PAD] * scale
            k = qk[r0:r0 + cl, HEADS_QK + h * DQK_PAD:HEADS_QK + (h + 1) * DQK_PAD]
            v = v_ref[r0:r0 + cl, h * DV_PAD:(h + 1) * DV_PAD]
            qb = q.astype(BF16)
            kb = k.astype(BF16)
            vb = v.astype(BF16)
            dmat = jnp.where(tri, bcum_col - bcum_row + ig_row, -jnp.inf)
            g_col = bcum_col + m_prev
            m_row = jnp.maximum(g_col, jnp.max(dmat, axis=1, keepdims=True))
            inter = jnp.exp(g_col - m_row)
            s_qk = lax.dot_general(qb, kb, _NT, preferred_element_type=F32) * jnp.exp(dmat - m_row)
            num = inter * _dot(qb, c_prev.astype(BF16)) + _dot(s_qk.astype(BF16), vb)
            den = (inter * jnp.sum(q * n_prev, axis=1, keepdims=True)
                   + jnp.sum(s_qk, axis=1, keepdims=True))
            hv = num / jnp.maximum(jnp.abs(den), jnp.exp(-m_row))
            ke = k * e_col
            kv = lax.dot_general(ke.astype(BF16), vb, _TN, preferred_element_type=F32)
            nk = jnp.sum(ke, axis=0, keepdims=True)
            m_new = jnp.maximum(btot + m_prev, m_loc)
            sa = jnp.exp(btot + m_prev - m_new)
            sb = jnp.exp(m_loc - m_new)
            c_st[h] = sa * c_prev + sb * kv
            n_st[h:h + 1, :] = sa * n_prev + sb * nk
            m_st[h:h + 1, :] = jnp.broadcast_to(m_new, (1, LANES))
            hn = _head_norm_padded(hv, ng_ref[0:1, h * DV_PAD:(h + 1) * DV_PAD])
            og = o_ref[r0:r0 + cl, h * DV_PAD:(h + 1) * DV_PAD]
            y_ref[r0:r0 + cl, h * DV_PAD:(h + 1) * DV_PAD] = jax.nn.sigmoid(og) * hn


_ML_GATE_BLK = (2 * HEADS_QK + 2 * HEADS_V + XATTN_WIDTH) // LANES


def _mlstm_mixer(proj, bsz, seqlen, cw, gbias, norm_g, tl=ROW_TILE):
    t = proj.shape[0]
    nl = seqlen // tl
    rows = lambda b, i: b * nl + i
    return pl.pallas_call(
        functools.partial(_mlstm_kernel, tl=tl),
        out_shape=jax.ShapeDtypeStruct((t, HEADS_V), F32),
        grid=(bsz, nl),
        in_specs=[pl.BlockSpec((tl, HEADS_QK), lambda b, i: (rows(b, i), 0)),
                  pl.BlockSpec((tl, HEADS_QK), lambda b, i: (rows(b, i), 1)),
                  pl.BlockSpec((tl, HEADS_V), lambda b, i: (rows(b, i), 1)),
                  pl.BlockSpec((tl, HEADS_V), lambda b, i: (rows(b, i), 2)),
                  pl.BlockSpec((tl, LANES), lambda b, i: (rows(b, i), _ML_GATE_BLK)),
                  pl.BlockSpec((ML_CONV, 2 * HEADS_QK), lambda b, i: (0, 0)),
                  pl.BlockSpec((1, LANES), lambda b, i: (0, 0)),
                  pl.BlockSpec((1, HEADS_V), lambda b, i: (0, 0))],
        out_specs=pl.BlockSpec((tl, HEADS_V), lambda b, i: (rows(b, i), 0)),
        scratch_shapes=[pltpu.VMEM((tl + SUBLANES, 2 * HEADS_QK), F32),
                        pltpu.VMEM((N_HEADS, DQK_PAD, DV_PAD), F32),
                        pltpu.VMEM((SUBLANES, DQK_PAD), F32),
                        pltpu.VMEM((SUBLANES, LANES), F32)],
        compiler_params=_cparams(("parallel", "arbitrary")),
        name="mlstm",
    )(proj, proj, proj, proj, proj, cw, gbias, norm_g)


def _ret_log_gamma(h):
    return float(np.log(np.float32(1.0) - np.power(np.float32(2.0), np.float32(-5.0 - h))))


def _ret_kernel(q_ref, k_ref, v_ref, g_ref, pos_ref, inv_ref, sgn_ref, ng_ref,
                y_ref, s_st, *, tl):
    i = pl.program_id(1)

    @pl.when(i == 0)
    def _():
        s_st[...] = jnp.zeros_like(s_st)

    ang = pos_ref[...] * inv_ref[...]
    cos_t = jnp.cos(ang)
    sin_t = jnp.sin(ang) * sgn_ref[...]
    cl = RET_CHUNK
    row = lax.broadcasted_iota(jnp.int32, (cl, cl), 0)
    col = lax.broadcasted_iota(jnp.int32, (cl, cl), 1)
    rel = (row - col).astype(F32)
    jcol = lax.broadcasted_iota(jnp.int32, (cl, 1), 0).astype(F32)
    kscale = DQK ** -0.5
    for h in range(N_HEADS):
        lg = _ret_log_gamma(h)
        decay = jnp.where(rel >= 0, jnp.exp(jnp.maximum(rel, 0.0) * lg), 0.0)
        zeta = jnp.exp((cl - 1 - jcol) * lg)
        xi = jnp.exp((jcol + 1.0) * lg)
        chunk_decay = float(np.exp(np.float32(cl) * np.float32(lg)))
        qh = q_ref[:, h * DQK_PAD:(h + 1) * DQK_PAD]
        kh = k_ref[:, h * DQK_PAD:(h + 1) * DQK_PAD]
        qh = qh * cos_t + pltpu.roll(qh, DQK_PAD // 2, 1) * sin_t
        kh = (kh * cos_t + pltpu.roll(kh, DQK_PAD // 2, 1) * sin_t) * kscale
        for cc in range(tl // cl):
            r0 = cc * cl
            qb = qh[r0:r0 + cl].astype(BF16)
            k = kh[r0:r0 + cl]
            kb = k.astype(BF16)
            v = v_ref[r0:r0 + cl, h * DV_PAD:(h + 1) * DV_PAD]
            vb = v.astype(BF16)
            s_prev = s_st[h]
            s = lax.dot_general(qb, kb, _NT, preferred_element_type=F32) * decay
            intra = _dot(s.astype(BF16), vb)
            cross = _dot(qb, s_prev.astype(BF16)) * xi
            r = lax.dot_general((k * zeta).astype(BF16), vb, _TN, preferred_element_type=F32)
            s_st[h] = chunk_decay * s_prev + r
            hn = _head_norm_padded(intra + cross, ng_ref[0:1, h * DV_PAD:(h + 1) * DV_PAD])
            gate = g_ref[r0:r0 + cl, h * DV_PAD:(h + 1) * DV_PAD]
            y_ref[r0:r0 + cl, h * DV_PAD:(h + 1) * DV_PAD] = gate * jax.nn.sigmoid(gate) * hn


def _ret_mixer(proj, bsz, seqlen, pos_f, inv_pad, sgn_pad, norm_g, tl=ROW_TILE):
    t = proj.shape[0]
    nl = seqlen // tl
    rows = lambda b, i: b * nl + i
    return pl.pallas_call(
        functools.partial(_ret_kernel, tl=tl),
        out_shape=jax.ShapeDtypeStruct((t, HEADS_V), F32),
        grid=(bsz, nl),
        in_specs=[pl.BlockSpec((tl, HEADS_QK), lambda b, i: (rows(b, i), 0)),
                  pl.BlockSpec((tl, HEADS_QK), lambda b, i: (rows(b, i), 1)),
                  pl.BlockSpec((tl, HEADS_V), lambda b, i: (rows(b, i), 1)),
                  pl.BlockSpec((tl, HEADS_V), lambda b, i: (rows(b, i), 2)),
                  pl.BlockSpec((tl, 1), lambda b, i: (rows(b, i), 0)),
                  pl.BlockSpec((1, DQK_PAD), lambda b, i: (0, 0)),
                  pl.BlockSpec((1, DQK_PAD), lambda b, i: (0, 0)),
                  pl.BlockSpec((1, HEADS_V), lambda b, i: (0, 0))],
        out_specs=pl.BlockSpec((tl, HEADS_V), lambda b, i: (rows(b, i), 0)),
        scratch_shapes=[pltpu.VMEM((N_HEADS, DQK_PAD, DV_PAD), F32)],
        compiler_params=_cparams(("parallel", "arbitrary")),
        name="retention",
    )(proj, proj, proj, proj, pos_f, inv_pad, sgn_pad, norm_g)


def _pack_rows(x):
    half = x.shape[1] // 2
    lo = pltpu.bitcast(x[:, :half].astype(BF16).astype(F32), jnp.uint32)
    hi = pltpu.bitcast(x[:, half:].astype(BF16).astype(F32), jnp.uint32)
    return hi | (lo >> 16)


def _unpack_rows(u):
    lo = pltpu.bitcast(u << 16, F32)
    hi = pltpu.bitcast(u & jnp.uint32(0xFFFF0000), F32)
    return jnp.concatenate([lo, hi], axis=1)


def _post_kernel(ym_ref, xq_ref, h_ref, mk_ref, mv_ref, wom_ref, wox_ref, g_ref, b_ref,
                 rwh_ref, rwl_ref, rb_ref, h1_ref, h1p_ref, idx_ref, gate_ref):
    tl = xq_ref.shape[0]
    xq = xq_ref[...] * (XATTN_HEAD_DIM ** -0.5)
    lane = lax.broadcasted_iota(jnp.int32, (tl, XATTN_WIDTH), 1)
    head = lane // XATTN_HEAD_DIM
    mk = mk_ref[0]
    mv = mv_ref[0]
    ymem = jnp.zeros((tl, XATTN_WIDTH), F32)
    for hh in range(XATTN_HEADS):
        sel = head == hh
        qh = jnp.where(sel, xq, 0.0).astype(BF16)
        s = lax.dot_general(qh, mk, _NT, preferred_element_type=F32)
        s = s - jnp.max(s, axis=-1, keepdims=True)
        p = jnp.exp(s)
        p = p / jnp.sum(p, axis=-1, keepdims=True)
        ymem = jnp.where(sel, _dot(p.astype(BF16), mv), ymem)
    y = _dot(ym_ref[...].astype(BF16), wom_ref[...]) + _dot(ymem.astype(BF16), wox_ref[...])
    h1 = _layer_norm_rows(DEEPNORM_ALPHA * h_ref[...] + y, g_ref[...], b_ref[...])
    h1_ref[...] = h1
    h1p_ref[...] = _pack_rows(h1)
    h_hi = h1.astype(BF16)
    h_lo = (h1 - h_hi.astype(F32)).astype(BF16)
    logits = (_dot(h_hi, rwh_ref[...]) + _dot(h_lo, rwh_ref[...]) + _dot(h_hi, rwl_ref[...])
              + rb_ref[...])
    ln = lax.broadcasted_iota(jnp.int32, logits.shape, 1)
    vals = logits
    tv, ti = [], []
    for _ in range(TOP_K):
        m = jnp.max(vals, axis=-1, keepdims=True)
        ix = jnp.min(jnp.where(vals == m, ln, LANES), axis=-1, keepdims=True)
        tv.append(m)
        ti.append(ix)
        vals = jnp.where(ln == ix, -jnp.inf, vals)
    ex = [jnp.exp(v - tv[0]) for v in tv]
    tot = ex[0] + ex[1] + ex[2] + ex[3]
    idx_out = jnp.zeros(logits.shape, jnp.int32)
    gate_out = jnp.zeros(logits.shape, F32)
    for k in range(TOP_K):
        idx_out = jnp.where(ln == k, ti[k], idx_out)
        gate_out = jnp.where(ln == k, ex[k] / tot, gate_out)
    idx_ref[...] = idx_out
    gate_ref[...] = gate_out


def _post_mixer(y_mix, proj, xq_blk, h, mem_k, mem_v, wo_mix, wo_mem, ln_g, ln_b, rw, rb,
                seqlen, tl=4 * ROW_TILE):
    t, cm = y_mix.shape
    nl = seqlen // tl
    d = D_MODEL
    rw_hi = rw.astype(BF16)
    rw_lo = (rw - rw_hi.astype(F32)).astype(BF16)
    full = lambda a, b: pl.BlockSpec((a, b), lambda i: (0, 0))
    return pl.pallas_call(
        _post_kernel,
        out_shape=(jax.ShapeDtypeStruct((t, d), F32),
                   jax.ShapeDtypeStruct((t, d // 2), jnp.uint32),
                   jax.ShapeDtypeStruct((t, LANES), jnp.int32),
                   jax.ShapeDtypeStruct((t, LANES), F32)),
        grid=(t // tl,),
        in_specs=[pl.BlockSpec((tl, cm), lambda i: (i, 0)),
                  pl.BlockSpec((tl, XATTN_WIDTH), lambda i: (i, xq_blk)),
                  pl.BlockSpec((tl, d), lambda i: (i, 0)),
                  pl.BlockSpec((1, N_MEM, XATTN_WIDTH), lambda i: (i // nl, 0, 0)),
                  pl.BlockSpec((1, N_MEM, XATTN_WIDTH), lambda i: (i // nl, 0, 0)),
                  full(cm, d), full(XATTN_WIDTH, d), full(1, d), full(1, d),
                  full(d, LANES), full(d, LANES), full(1, LANES)],
        out_specs=(pl.BlockSpec((tl, d), lambda i: (i, 0)),
                   pl.BlockSpec((tl, d // 2), lambda i: (i, 0)),
                   pl.BlockSpec((tl, LANES), lambda i: (i, 0)),
                   pl.BlockSpec((tl, LANES), lambda i: (i, 0))),
        compiler_params=_cparams(("parallel",)),
        name="post_mixer",
    )(y_mix, proj, h, mem_k, mem_v, wo_mix, wo_mem, ln_g, ln_b, rw_hi, rw_lo, rb)


SC_CORES = 2
SC_SUBCORES = 16
SC_WORKERS = SC_CORES * SC_SUBCORES
SC_CHUNK = 64


def _sc_gather(table, idx):
    v, d = table.shape
    b = idx.shape[0]
    per_w = b // SC_WORKERS
    n_chunks = per_w // SC_CHUNK
    assert per_w * SC_WORKERS == b and n_chunks * SC_CHUNK == per_w and n_chunks % 2 == 0
    mesh = plsc.VectorSubcoreMesh(core_axis_name="c", subcore_axis_name="s")

    @functools.partial(
        pl.kernel, mesh=mesh,
        out_type=jax.ShapeDtypeStruct((b, d), table.dtype),
        scratch_types=[pltpu.VMEM((per_w,), jnp.int32),
                       pltpu.VMEM((SC_CHUNK, d), table.dtype),
                       pltpu.VMEM((SC_CHUNK, d), table.dtype),
                       pltpu.SemaphoreType.DMA,
                       pltpu.SemaphoreType.DMA],
    )
    def gather_kernel(table_hbm, idx_hbm, out_hbm, idx_v, rows0, rows1, sem0, sem1):
        wid = lax.axis_index("s") * SC_CORES + lax.axis_index("c")
        base = wid * per_w
        pltpu.sync_copy(idx_hbm.at[pl.ds(pl.multiple_of(base, 8), per_w)], idx_v)
        ring = ((rows0, sem0), (rows1, sem1))

        def gather(c, buf, sem):
            rows = idx_v.at[pl.ds(pl.multiple_of(c * SC_CHUNK, 8), SC_CHUNK)]
            return pltpu.make_async_copy(table_hbm.at[rows], buf, sem)

        for c0, (buf, sem) in enumerate(ring):
            gather(c0, buf, sem).start()

        @pl.loop(0, n_chunks, step=2)
        def _(c):
            for k, (buf, sem) in enumerate(ring):
                cc = c + k
                gather(cc, buf, sem).wait()
                off = pl.multiple_of(base + cc * SC_CHUNK, 8)
                pltpu.sync_copy(buf, out_hbm.at[pl.ds(off, SC_CHUNK)])

                @pl.when(cc + 2 < n_chunks)
                def _():
                    gather(cc + 2, buf, sem).start()

    return gather_kernel(table, idx)


def _sc_dispatch(x, pos_kmajor, n_rows):
    t, d = x.shape
    per_w = t // SC_WORKERS
    n_chunks = per_w // SC_CHUNK
    assert per_w * SC_WORKERS == t and n_chunks * SC_CHUNK == per_w
    mesh = plsc.VectorSubcoreMesh(core_axis_name="c", subcore_axis_name="s")

    @functools.partial(
        pl.kernel, mesh=mesh,
        out_type=jax.ShapeDtypeStruct((n_rows, d), x.dtype),
        scratch_types=[pltpu.VMEM((SC_CHUNK,), jnp.int32),
                       pltpu.VMEM((SC_CHUNK, d), x.dtype)],
    )
    def dispatch_kernel(x_hbm, pos_hbm, out_hbm, idx_v, rows_v):
        wid = lax.axis_index("s") * SC_CORES + lax.axis_index("c")
        base = wid * per_w

        @pl.loop(0, n_chunks)
        def _(j):
            off = pl.multiple_of(base + j * SC_CHUNK, 8)
            pltpu.sync_copy(x_hbm.at[pl.ds(off, SC_CHUNK)], rows_v)
            for k in range(TOP_K):
                pltpu.sync_copy(pos_hbm.at[pl.ds(pl.multiple_of(k * t + off, 8), SC_CHUNK)], idx_v)
                pltpu.sync_copy(rows_v, out_hbm.at[idx_v])

    return dispatch_kernel(x, pos_kmajor)


def _expert_kernel(bexp_ref, nvalid_ref, nused_ref, x_ref, wgu_ref, bgu_ref, wd_ref, bd_ref, y_ref,
                   wgu_bf, wd_bf):
    i = pl.program_id(0)
    de = wd_ref.shape[2]

    @pl.when(i < nused_ref[0])
    def _():
        prev = bexp_ref[jnp.maximum(i - 1, 0)]

        @pl.when((i == 0) | (prev != bexp_ref[i]))
        def _():
            wgu_bf[...] = wgu_ref[0, 0].astype(BF16)
            wd_bf[...] = wd_ref[0, 0].astype(BF16)

        rows = lax.broadcasted_iota(jnp.int32, x_ref.shape, 0)
        xb = _unpack_rows(jnp.where(rows < nvalid_ref[i], x_ref[...], jnp.uint32(0))).astype(BF16)
        gu = _dot(xb, wgu_bf[...]) + bgu_ref[0, 0]
        x_glu = jnp.minimum(gu[:, :de], SWIGLU_LIMIT)
        x_lin = jnp.clip(gu[:, de:], -SWIGLU_LIMIT, SWIGLU_LIMIT)
        act = x_glu * jax.nn.sigmoid(SWIGLU_ALPHA * x_glu) * (x_lin + 1.0)
        y_ref[...] = _pack_rows(_dot(act.astype(BF16), wd_bf[...]) + bd_ref[0, 0])

    @pl.when(i >= nused_ref[0])
    def _():
        y_ref[...] = jnp.zeros_like(y_ref)


def _experts(xr, block_exp, n_valid, n_used, layer, w_gu, b_gu, w_down, b_down, bm=MOE_BM):
    n_rows, dp = xr.shape
    d = 2 * dp
    n_blocks = n_rows // bm
    nl, ne, _, de2 = w_gu.shape
    de = de2 // 2
    row_blk = lambda i, nu: jnp.minimum(i, nu[0] - 1)
    grid_spec = pltpu.PrefetchScalarGridSpec(
        num_scalar_prefetch=3,
        grid=(n_blocks,),
        in_specs=[pl.BlockSpec((bm, dp), lambda i, be, nv, nu: (row_blk(i, nu), 0)),
                  pl.BlockSpec((1, 1, d, de2), lambda i, be, nv, nu: (layer, be[i], 0, 0)),
                  pl.BlockSpec((1, 1, 1, de2), lambda i, be, nv, nu: (layer, be[i], 0, 0)),
                  pl.BlockSpec((1, 1, de, d), lambda i, be, nv, nu: (layer, be[i], 0, 0)),
                  pl.BlockSpec((1, 1, 1, d), lambda i, be, nv, nu: (layer, be[i], 0, 0))],
        out_specs=pl.BlockSpec((bm, dp), lambda i, be, nv, nu: (i, 0)),
        scratch_shapes=[pltpu.VMEM((d, de2), BF16),
                        pltpu.VMEM((de, d), BF16)],
    )
    return pl.pallas_call(
        _expert_kernel,
        out_shape=jax.ShapeDtypeStruct((n_rows, dp), jnp.uint32),
        grid_spec=grid_spec,
        compiler_params=_cparams(("arbitrary",)),
        name="experts",
    )(block_exp, n_valid, n_used, xr, w_gu, b_gu.reshape(nl, ne, 1, de2), w_down,
      b_down.reshape(nl, ne, 1, d))


def _combine_kernel(y0_ref, y1_ref, y2_ref, y3_ref, h1_ref, gate_ref, g_ref, b_ref, o_ref):
    gate = gate_ref[...]
    acc = DEEPNORM_ALPHA * h1_ref[...]
    for k, y_ref in enumerate((y0_ref, y1_ref, y2_ref, y3_ref)):
        acc = acc + gate[:, k:k + 1] * _unpack_rows(y_ref[0])
    o_ref[...] = _layer_norm_rows(acc, g_ref[...], b_ref[...])


def _combine(yg, h1, gates, ln_g, ln_b, tl=ROW_TILE):
    t, d = h1.shape
    ysel = lambda k: pl.BlockSpec((1, tl, d // 2), lambda i: (k, i, 0))
    return pl.pallas_call(
        _combine_kernel,
        out_shape=jax.ShapeDtypeStruct((t, d), F32),
        grid=(t // tl,),
        in_specs=[ysel(0), ysel(1), ysel(2), ysel(3),
                  pl.BlockSpec((tl, d), lambda i: (i, 0)),
                  pl.BlockSpec((tl, LANES), lambda i: (i, 0)),
                  pl.BlockSpec((1, d), lambda i: (0, 0)),
                  pl.BlockSpec((1, d), lambda i: (0, 0))],
        out_specs=pl.BlockSpec((tl, d), lambda i: (i, 0)),
        compiler_params=_cparams(("parallel",)),
        name="combine",
    )(yg, yg, yg, yg, h1, gates, ln_g, ln_b)


def _route(idx, bm=MOE_BM):
    t = idx.shape[0]
    n_assign = t * TOP_K
    n_rows = (-(-n_assign // bm) + N_EXPERTS) * bm
    n_blocks = n_rows // bm
    onehot = (idx[:, :, None] == jnp.arange(N_EXPERTS, dtype=jnp.int32)[None, None, :])
    sel = jnp.sum(onehot.astype(jnp.int32), axis=1)
    csum = jnp.cumsum(sel, axis=0)
    counts = csum[-1]
    rank = csum - sel
    padded = ((counts + bm - 1) // bm) * bm
    pad_end = jnp.cumsum(padded)
    pad_start = pad_end - padded
    base = pad_start[None, :] + rank
    pos = jnp.sum(jnp.where(onehot, base[:, None, :], 0), axis=2)
    block_start = jnp.arange(n_blocks, dtype=jnp.int32) * bm
    block_exp = jnp.minimum(
        jnp.sum((block_start[:, None] >= pad_end[None, :]).astype(jnp.int32), axis=1),
        N_EXPERTS - 1).astype(jnp.int32)
    own = block_exp[:, None] == jnp.arange(N_EXPERTS, dtype=jnp.int32)[None, :]
    filled_end = jnp.sum(jnp.where(own, (pad_start + counts)[None, :], 0), axis=1)
    n_valid = jnp.clip(filled_end - block_start, 0, bm).astype(jnp.int32)
    n_used = (pad_end[-1] // bm).astype(jnp.int32).reshape(1)
    return pos.T.reshape(-1).astype(jnp.int32), block_exp, n_valid, n_used, n_rows


def _take_cols(w, cols):
    cols = list(cols)
    parts, i = [], 0
    while i < len(cols):
        j = i
        if cols[i] < 0:
            while j < len(cols) and cols[j] < 0:
                j += 1
            parts.append(jnp.zeros(w.shape[:-1] + (j - i,), w.dtype))
        else:
            while j + 1 < len(cols) and cols[j + 1] == cols[j] + 1:
                j += 1
            j += 1
            parts.append(w[..., cols[i]:cols[i] + (j - i)])
        i = j
    return jnp.concatenate(parts, axis=-1)


def _head_cols(offset, width, pad):
    cols = []
    for h in range(N_HEADS):
        cols += list(range(offset + h * width, offset + (h + 1) * width)) + [-1] * (pad - width)
    return cols


def _rope_head_cols(offset):
    half = DQK // 2
    slot = DQK_PAD // 2
    cols = []
    for h in range(N_HEADS):
        b = offset + h * DQK
        cols += list(range(b, b + half)) + [-1] * (slot - half)
        cols += list(range(b + half, b + DQK)) + [-1] * (slot - half)
    return cols


_ML_GATE_OFF = 2 * ML_QK + 2 * MIX_WIDTH
_ML_COLS = (_head_cols(0, DQK, DQK_PAD) + _head_cols(ML_QK, DQK, DQK_PAD)
            + _head_cols(2 * ML_QK, DV, DV_PAD) + _head_cols(2 * ML_QK + MIX_WIDTH, DV, DV_PAD)
            + list(range(_ML_GATE_OFF + 2 * N_HEADS, _ML_GATE_OFF + 2 * N_HEADS + XATTN_WIDTH))
            + list(range(_ML_GATE_OFF, _ML_GATE_OFF + 2 * N_HEADS)) + [-1] * (LANES - 2 * N_HEADS))
_RET_COLS = (_rope_head_cols(0) + _rope_head_cols(ML_QK)
             + _head_cols(2 * ML_QK, DV, DV_PAD) + _head_cols(2 * ML_QK + MIX_WIDTH, DV, DV_PAD)
             + list(range(2 * ML_QK + 2 * MIX_WIDTH, 2 * ML_QK + 2 * MIX_WIDTH + XATTN_WIDTH)))
_MIX_PAD_COLS = _head_cols(0, DV, DV_PAD)
_XQ_BLK_PADDED = (2 * HEADS_QK + 2 * HEADS_V) // XATTN_WIDTH
_XQ_BLK_S5 = MIX_WIDTH // XATTN_WIDTH


def kernel(x, mem, positions, mem_w_k, mem_w_v, l0_w_in, l0_s5_a_re, l0_s5_a_im, l0_s5_log_dt, l0_s5_b_re, l0_s5_b_im, l0_s5_c_re, l0_s5_c_im, l0_s5_d, l0_s5_w_glu, l0_s5_b_glu, l1_w_in, l1_ml_conv_q, l1_ml_conv_k, l1_ml_b_i, l1_ml_b_f, l1_ml_norm_g, l2_w_in, l2_ret_norm_g, l3_w_in, l3_s5_a_re, l3_s5_a_im, l3_s5_log_dt, l3_s5_b_re, l3_s5_b_im, l3_s5_c_re, l3_s5_c_im, l3_s5_d, l3_s5_w_glu, l3_s5_b_glu, w_out, ln1_g, ln1_b, ln2_g, ln2_b, router_w, router_b, exp_w_gu, exp_b_gu, exp_w_down, exp_b_down):
    bsz, seqlen, d = x.shape
    t = bsz * seqlen
    h = x.reshape(t, d)

    w_kv = jnp.concatenate([mem_w_k, mem_w_v], axis=1).astype(BF16)
    kv = _inproj(mem.reshape(bsz * N_MEM, d), w_kv).astype(BF16)
    mem_k = kv[:, :XATTN_WIDTH].reshape(bsz, N_MEM, XATTN_WIDTH)
    mem_v = kv[:, XATTN_WIDTH:].reshape(bsz, N_MEM, XATTN_WIDTH)

    half = DQK // 2
    inv = ROPE_BASE ** (-jnp.arange(0, DQK, 2, dtype=F32) / DQK)
    zpad = jnp.zeros((DQK_PAD // 2 - half,), F32)
    inv_pad = jnp.concatenate([inv, zpad, inv, zpad]).reshape(1, DQK_PAD)
    sgn_pad = jnp.concatenate([-jnp.ones((half,), F32), zpad, jnp.ones((half,), F32), zpad]
                              ).reshape(1, DQK_PAD)
    pos_f = positions.astype(F32).reshape(t, 1)

    s5_params = {
        0: (l0_s5_a_re, l0_s5_a_im, l0_s5_log_dt, l0_s5_b_re, l0_s5_b_im, l0_s5_c_re, l0_s5_c_im,
            l0_s5_d, l0_s5_w_glu, l0_s5_b_glu),
        3: (l3_s5_a_re, l3_s5_a_im, l3_s5_log_dt, l3_s5_b_re, l3_s5_b_im, l3_s5_c_re, l3_s5_c_im,
            l3_s5_d, l3_s5_w_glu, l3_s5_b_glu),
    }
    w_ins = (l0_w_in, l1_w_in, l2_w_in, l3_w_in)

    for i in range(DEPTH):
        kind = i % 3
        wo = w_out[i]
        if kind == 0:
            proj = _inproj(h, w_ins[i].astype(BF16))
            y_mix = _s5_mixer(proj, bsz, seqlen, *s5_params[i])
            wo_mix = wo[:MIX_WIDTH].astype(BF16)
            xq_blk = _XQ_BLK_S5
        elif kind == 1:
            proj = _inproj(h, _take_cols(w_ins[i], _ML_COLS).astype(BF16))
            cw = jnp.concatenate([_take_cols(l1_ml_conv_q, _head_cols(0, DQK, DQK_PAD)),
                                  _take_cols(l1_ml_conv_k, _head_cols(0, DQK, DQK_PAD))], axis=1)
            gbias = jnp.concatenate([l1_ml_b_i, l1_ml_b_f,
                                     jnp.zeros((LANES - 2 * N_HEADS,), F32)]).reshape(1, LANES)
            norm_g = _take_cols(l1_ml_norm_g, _MIX_PAD_COLS).reshape(1, HEADS_V)
            y_mix = _mlstm_mixer(proj, bsz, seqlen, cw, gbias, norm_g)
            wo_mix = _take_cols(wo[:MIX_WIDTH].T, _MIX_PAD_COLS).T.astype(BF16)
            xq_blk = _XQ_BLK_PADDED
        else:
            proj = _inproj(h, _take_cols(w_ins[i], _RET_COLS).astype(BF16))
            norm_g = _take_cols(l2_ret_norm_g, _MIX_PAD_COLS).reshape(1, HEADS_V)
            y_mix = _ret_mixer(proj, bsz, seqlen, pos_f, inv_pad, sgn_pad, norm_g)
            wo_mix = _take_cols(wo[:MIX_WIDTH].T, _MIX_PAD_COLS).T.astype(BF16)
            xq_blk = _XQ_BLK_PADDED
        wo_mem = wo[MIX_WIDTH:].astype(BF16)
        rw = jnp.pad(router_w[i], ((0, 0), (0, LANES - N_EXPERTS)))
        rb = jnp.concatenate([router_b[i], jnp.full((LANES - N_EXPERTS,), -1e30, F32)]
                             ).reshape(1, LANES)
        h1, h1p, idx, gates = _post_mixer(y_mix, proj, xq_blk, h, mem_k, mem_v, wo_mix, wo_mem,
                                     ln1_g[i].reshape(1, d), ln1_b[i].reshape(1, d), rw, rb, seqlen)
        pos_kmajor, block_exp, n_valid, n_used, n_rows = _route(idx[:, :TOP_K])
        xr = _sc_dispatch(h1p, pos_kmajor, n_rows)
        yr = _experts(xr, block_exp, n_valid, n_used, i, exp_w_gu, exp_b_gu, exp_w_down, exp_b_down)
        yg = _sc_gather(yr, pos_kmajor).reshape(TOP_K, t, d // 2)
        h = _combine(yg, h1, gates, ln2_g[i].reshape(1, d), ln2_b[i].reshape(1, d))
    return h.reshape(bsz, seqlen, d)
```

```python
import functools

import numpy as np
import jax
import jax.numpy as jnp
from jax import lax
from jax.experimental import pallas as pl
from jax.experimental.pallas import tpu as pltpu
from jax.experimental.pallas import tpu_sc as plsc

F32 = jnp.float32
BF16 = jnp.bfloat16

D_MODEL = 1024
DEPTH = 4
N_MEM = 256
MIX_WIDTH = 768
XATTN_HEADS = 4
XATTN_WIDTH = 256
XATTN_HEAD_DIM = 64
S5_GROUP = 16
S5_GROUPS = 48
S5_STATE = 64
N_HEADS = 4
DQK = 96
DV = 192
ML_QK = 384
ML_CONV = 4
ML_CHUNK = 64
RET_CHUNK = 128
ROPE_BASE = 10000.0
N_EXPERTS = 32
TOP_K = 4
SWIGLU_LIMIT = 7.0
SWIGLU_ALPHA = 1.702
DEEPNORM_ALPHA = (2.0 * DEPTH) ** 0.25
LN_EPS = 1e-5

LANES = 128
SUBLANES = 8
DQK_PAD = 128
DV_PAD = 256
HEADS_QK = N_HEADS * DQK_PAD
HEADS_V = N_HEADS * DV_PAD
S5_LC = 16
S5_K = S5_LC * S5_GROUP
MOE_BM = 512
ROW_TILE = 256
VMEM_LIMIT = 56 * 1024 * 1024

_NT = (((1,), (1,)), ((), ()))
_TN = (((0,), (0,)), ((), ()))


def _cparams(sem):
    return pltpu.CompilerParams(dimension_semantics=sem, vmem_limit_bytes=VMEM_LIMIT)


def _dot(a, b):
    return jnp.dot(a, b, preferred_element_type=F32)


def _layer_norm_rows(z, g, b):
    mu = jnp.mean(z, axis=-1, keepdims=True)
    d = z - mu
    var = jnp.mean(d * d, axis=-1, keepdims=True)
    return d * lax.rsqrt(var + LN_EPS) * g + b


def _inproj_kernel(x_ref, w_ref, o_ref):
    xb = x_ref[...].astype(BF16)
    n = o_ref.shape[1]
    step = 512
    for c0 in range(0, n, step):
        c1 = min(c0 + step, n)
        o_ref[:, c0:c1] = _dot(xb, w_ref[:, c0:c1])


def _inproj(x, w_bf16, tm=2 * ROW_TILE):
    t, d = x.shape
    n = w_bf16.shape[1]
    return pl.pallas_call(
        _inproj_kernel,
        out_shape=jax.ShapeDtypeStruct((t, n), F32),
        grid=(t // tm,),
        in_specs=[pl.BlockSpec((tm, d), lambda i: (i, 0)),
                  pl.BlockSpec((d, n), lambda i: (0, 0))],
        out_specs=pl.BlockSpec((tm, n), lambda i: (i, 0)),
        compiler_params=_cparams(("parallel",)),
        name="inproj",
    )(x, w_bf16)


S5_LANE_GROUPS = S5_K // S5_GROUP
S5_BLOCKS = MIX_WIDTH // S5_K
S5_SW = S5_LANE_GROUPS * S5_STATE
S5_LB = 512


def _block_diag(m):
    nb, g, a, b = m.shape
    eye = jnp.eye(g, dtype=m.dtype)
    return (m[:, :, :, None, :] * eye[None, :, None, :, None]).reshape(nb, g * a, g * b)


def _s5_prep(a_re, a_im, log_dt, b_re, b_im, c_re, c_im):
    hp = lax.Precision.HIGHEST
    lam_re = jnp.minimum(a_re.astype(F32), -1e-4)
    lam_im = a_im.astype(F32)
    dt = jnp.exp(log_dt.astype(F32))[:, None]
    mag = jnp.exp(dt * lam_re)
    ab_re = mag * jnp.cos(dt * lam_im)
    ab_im = mag * jnp.sin(dt * lam_im)
    den = lam_re * lam_re + lam_im * lam_im
    num_re = ab_re - 1.0
    coef_re = (num_re * lam_re + ab_im * lam_im) / den
    coef_im = (ab_im * lam_re - num_re * lam_im) / den
    bre = b_re.astype(F32)
    bim = b_im.astype(F32)
    bb_re = coef_re[..., None] * bre - coef_im[..., None] * bim
    bb_im = coef_re[..., None] * bim + coef_im[..., None] * bre
    pr = [jnp.ones_like(ab_re)]
    pi = [jnp.zeros_like(ab_im)]
    for _ in range(S5_LC):
        r, i = pr[-1], pi[-1]
        pr.append(r * ab_re - i * ab_im)
        pi.append(r * ab_im + i * ab_re)
    pw_re = jnp.stack(pr)
    pw_im = jnp.stack(pi)
    p_re = pw_re[:S5_LC, :, :, None] * bb_re[None] - pw_im[:S5_LC, :, :, None] * bb_im[None]
    p_im = pw_re[:S5_LC, :, :, None] * bb_im[None] + pw_im[:S5_LC, :, :, None] * bb_re[None]
    cre = c_re.astype(F32)
    cim = c_im.astype(F32)
    kmat = (jnp.einsum('ghp,tgpk->tgkh', cre, p_re, precision=hp)
            - jnp.einsum('ghp,tgpk->tgkh', cim, p_im, precision=hp))
    nb, lg = S5_BLOCKS, S5_LANE_GROUPS
    toep = _block_diag(kmat.astype(BF16).reshape(S5_LC * nb, lg, S5_GROUP, S5_GROUP)
                       ).reshape(S5_LC, nb, S5_K, S5_K).transpose(1, 0, 2, 3)
    bmat = jnp.concatenate(
        [_block_diag(bb_re.transpose(0, 2, 1).reshape(nb, lg, S5_GROUP, S5_STATE)),
         _block_diag(bb_im.transpose(0, 2, 1).reshape(nb, lg, S5_GROUP, S5_STATE))], axis=2)
    cmat = jnp.concatenate(
        [_block_diag(cre.transpose(0, 2, 1).reshape(nb, lg, S5_STATE, S5_GROUP)),
         _block_diag(-cim.transpose(0, 2, 1).reshape(nb, lg, S5_STATE, S5_GROUP))], axis=1)
    lane = lambda v: v.reshape(nb, 1, S5_SW)
    avec = jnp.concatenate([lane(ab_re), lane(ab_im), lane(pw_re[S5_LC]), lane(pw_im[S5_LC])],
                           axis=1)
    return toep.astype(BF16), bmat.astype(BF16), cmat.astype(BF16), avec


def _s5_kernel(u_ref, t_ref, b_ref, c_ref, a_ref, y_ref, s_acc, x_prev, x_carry, u_half, y_half,
               *, lb):
    li = pl.program_id(1)
    bsz = u_ref.shape[0]
    ncb = lb // S5_LC
    m = bsz * ncb

    @pl.when(li == 0)
    def _():
        x_carry[...] = jnp.zeros_like(x_carry)

    ar = a_ref[0, 0:1, :]
    ai = a_ref[0, 1:2, :]
    alr = a_ref[0, 2:3, :]
    ali = a_ref[0, 3:4, :]

    def cmul(zr, zi, wr, wi):
        return wr * zr - wi * zi, wr * zi + wi * zr

    n_half = S5_K // LANES
    for hf in range(n_half):
        u_half[hf] = u_ref[:, :, hf * LANES:(hf + 1) * LANES].reshape(bsz * lb, LANES)
    xs = []
    for s in range(S5_LC):
        halves = [jnp.concatenate([u_half[hf, pl.ds(c * S5_LC + s, bsz, stride=lb), :]
                                   for c in range(ncb)], axis=0) for hf in range(n_half)]
        xs.append(jnp.concatenate(halves, axis=1).astype(BF16))
    bmat = b_ref[0]
    sr = jnp.zeros((m, S5_SW), F32)
    si = jnp.zeros((m, S5_SW), F32)
    for s in range(S5_LC):
        bu = _dot(xs[s], bmat)
        sr, si = cmul(sr, si, ar, ai)
        sr = sr + bu[:, :S5_SW]
        si = si + bu[:, S5_SW:]
    s_acc[:, :S5_SW] = sr
    s_acc[:, S5_SW:] = si

    xr = x_carry[:, :S5_SW]
    xi = x_carry[:, S5_SW:]
    for c in range(ncb):
        x_prev[c * bsz:(c + 1) * bsz, :S5_SW] = xr
        x_prev[c * bsz:(c + 1) * bsz, S5_SW:] = xi
        loc = s_acc[c * bsz:(c + 1) * bsz, :]
        xr, xi = cmul(xr, xi, alr, ali)
        xr = xr + loc[:, :S5_SW]
        xi = xi + loc[:, S5_SW:]
    x_carry[:, :S5_SW] = xr
    x_carry[:, S5_SW:] = xi

    cmat = c_ref[0]
    zr = x_prev[:, :S5_SW]
    zi = x_prev[:, S5_SW:]
    for j in range(S5_LC):
        zr, zi = cmul(zr, zi, ar, ai)
        yj = _dot(jnp.concatenate([zr, zi], axis=1).astype(BF16), cmat)
        for s in range(j + 1):
            yj = yj + _dot(xs[s], t_ref[0, j - s])
        for hf in range(n_half):
            for c in range(ncb):
                y_half[hf, pl.ds(c * S5_LC + j, bsz, stride=lb), :] = (
                    yj[c * bsz:(c + 1) * bsz, hf * LANES:(hf + 1) * LANES])
    for hf in range(n_half):
        y_ref[:, :, hf * LANES:(hf + 1) * LANES] = y_half[hf].reshape(bsz, lb, LANES)


def _s5_scan(proj3, mats, lb=S5_LB):
    toep, bmat, cmat, avec = mats
    bsz, seqlen, _ = proj3.shape
    return pl.pallas_call(
        functools.partial(_s5_kernel, lb=lb),
        out_shape=jax.ShapeDtypeStruct((bsz, seqlen, MIX_WIDTH), F32),
        grid=(S5_BLOCKS, seqlen // lb),
        in_specs=[pl.BlockSpec((bsz, lb, S5_K), lambda v, l: (0, l, v)),
                  pl.BlockSpec((1, S5_LC, S5_K, S5_K), lambda v, l: (v, 0, 0, 0)),
                  pl.BlockSpec((1, S5_K, 2 * S5_SW), lambda v, l: (v, 0, 0)),
                  pl.BlockSpec((1, 2 * S5_SW, S5_K), lambda v, l: (v, 0, 0)),
                  pl.BlockSpec((1, 4, S5_SW), lambda v, l: (v, 0, 0))],
        out_specs=pl.BlockSpec((bsz, lb, S5_K), lambda v, l: (0, l, v)),
        scratch_shapes=[pltpu.VMEM((bsz * (lb // S5_LC), 2 * S5_SW), F32),
                        pltpu.VMEM((bsz * (lb // S5_LC), 2 * S5_SW), F32),
                        pltpu.VMEM((bsz, 2 * S5_SW), F32),
                        pltpu.VMEM((S5_K // LANES, bsz * lb, LANES), F32),
                        pltpu.VMEM((S5_K // LANES, bsz * lb, LANES), F32)],
        compiler_params=_cparams(("parallel", "arbitrary")),
        name="s5_scan",
    )(proj3, toep, bmat, cmat, avec)


def _s5_post_kernel(y_ref, u_ref, d_ref, w_ref, b_ref, o_ref):
    y = y_ref[...] + d_ref[...] * u_ref[...]
    y = jax.nn.gelu(y)
    o_ref[...] = y * jax.nn.sigmoid(_dot(y.astype(BF16), w_ref[...]) + b_ref[...])


def _s5_post(y_ssm, proj, d_skip, w_glu_bf16, b_glu, tm=ROW_TILE):
    t = y_ssm.shape[0]
    w = MIX_WIDTH
    return pl.pallas_call(
        _s5_post_kernel,
        out_shape=jax.ShapeDtypeStruct((t, w), F32),
        grid=(t // tm,),
        in_specs=[pl.BlockSpec((tm, w), lambda i: (i, 0)),
                  pl.BlockSpec((tm, w), lambda i: (i, 0)),
                  pl.BlockSpec((1, w), lambda i: (0, 0)),
                  pl.BlockSpec((w, w), lambda i: (0, 0)),
                  pl.BlockSpec((1, w), lambda i: (0, 0))],
        out_specs=pl.BlockSpec((tm, w), lambda i: (i, 0)),
        compiler_params=_cparams(("parallel",)),
        name="s5_post",
    )(y_ssm, proj, d_skip.reshape(1, w), w_glu_bf16, b_glu.reshape(1, w))


def _s5_mixer(proj, bsz, seqlen, a_re, a_im, log_dt, b_re, b_im, c_re, c_im, d_skip, w_glu, b_glu):
    mats = _s5_prep(a_re, a_im, log_dt, b_re, b_im, c_re, c_im)
    y = _s5_scan(proj.reshape(bsz, seqlen, proj.shape[1]), mats)
    return _s5_post(y.reshape(bsz * seqlen, MIX_WIDTH), proj, d_skip, w_glu.astype(BF16), b_glu)


def _head_norm_padded(hv, g):
    lane = lax.broadcasted_iota(jnp.int32, hv.shape, 1)
    real = lane < DV
    mu = jnp.sum(hv, axis=-1, keepdims=True) * (1.0 / DV)
    d = jnp.where(real, hv - mu, 0.0)
    var = jnp.sum(d * d, axis=-1, keepdims=True) * (1.0 / DV)
    return d * lax.rsqrt(var + LN_EPS) * g


def _log_sigmoid(x):
    return jnp.minimum(x, 0.0) - jnp.log(1.0 + jnp.exp(-jnp.abs(x)))


def _mlstm_kernel(q_ref, k_ref, v_ref, o_ref, gt_ref, cw_ref, gb_ref, ng_ref,
                  y_ref, cbuf, c_st, n_st, m_st, *, tl):
    i = pl.program_id(1)

    @pl.when(i == 0)
    def _():
        cbuf[0:SUBLANES, :] = jnp.zeros((SUBLANES, 2 * HEADS_QK), F32)
        c_st[...] = jnp.zeros_like(c_st)
        n_st[...] = jnp.zeros_like(n_st)
        m_st[...] = jnp.zeros_like(m_st)

    cbuf[SUBLANES:SUBLANES + tl, 0:HEADS_QK] = q_ref[...]
    cbuf[SUBLANES:SUBLANES + tl, HEADS_QK:2 * HEADS_QK] = k_ref[...]
    acc = jnp.zeros((tl, 2 * HEADS_QK), F32)
    for w in range(ML_CONV):
        acc = acc + cbuf[pl.ds(SUBLANES - (ML_CONV - 1) + w, tl), :] * cw_ref[w:w + 1, :]
    qk = acc * jax.nn.sigmoid(acc)
    cbuf[0:SUBLANES, :] = cbuf[tl:tl + SUBLANES, :]

    gt = gt_ref[...] + gb_ref[...]
    lf = _log_sigmoid(gt)
    gt_t = gt.T
    lf_t = _log_sigmoid(gt_t)

    cl = ML_CHUNK
    row = lax.broadcasted_iota(jnp.int32, (cl, cl), 0)
    col = lax.broadcasted_iota(jnp.int32, (cl, cl), 1)
    tri = row >= col
    scale = DQK ** -0.5
    for cc in range(tl // cl):
        r0 = cc * cl
        for h in range(N_HEADS):
            ig_col = gt[r0:r0 + cl, h:h + 1]
            lf_col = lf[r0:r0 + cl, N_HEADS + h:N_HEADS + h + 1]
            ig_row = gt_t[h:h + 1, r0:r0 + cl]
            lf_row = lf_t[N_HEADS + h:N_HEADS + h + 1, r0:r0 + cl]
            bcum_col = jnp.sum(jnp.where(tri, lf_row, 0.0), axis=1, keepdims=True)
            bcum_row = jnp.sum(jnp.where(col >= row, lf_col, 0.0), axis=0, keepdims=True)
            btot = jnp.sum(lf_row, axis=1, keepdims=True)
            w_row = btot - bcum_row + ig_row
            m_loc = jnp.max(w_row, axis=1, keepdims=True)
            e_col = jnp.exp(btot - bcum_col + ig_col - m_loc)
            m_prev = m_st[h:h + 1, 0:1]
            c_prev = c_st[h]
            n_prev = n_st[h:h + 1, :]
            q = qk[r0:r0 + cl, h * DQK_PAD:(h + 1) * DQK_PAD] * scale
            k = qk[r0:r0 + cl, HEADS_QK + h * DQK_PAD:HEADS_QK + (h + 1) * DQK_PAD]
            v = v_ref[r0:r0 + cl, h * DV_PAD:(h + 1) * DV_PAD]
            qb = q.astype(BF16)
            kb = k.astype(BF16)
            vb = v.astype(BF16)
            dmat = jnp.where(tri, bcum_col - bcum_row + ig_row, -jnp.inf)
            g_col = bcum_col + m_prev
            m_row = jnp.maximum(g_col, jnp.max(dmat, axis=1, keepdims=True))
            inter = jnp.exp(g_col - m_row)
            s_qk = lax.dot_general(qb, kb, _NT, preferred_element_type=F32) * jnp.exp(dmat - m_row)
            num = inter * _dot(qb, c_prev.astype(BF16)) + _dot(s_qk.astype(BF16), vb)
            den = (inter * jnp.sum(q * n_prev, axis=1, keepdims=True)
                   + jnp.sum(s_qk, axis=1, keepdims=True))
            hv = num / jnp.maximum(jnp.abs(den), jnp.exp(-m_row))
            ke = k * e_col
            kv = lax.dot_general(ke.astype(BF16), vb, _TN, preferred_element_type=F32)
            nk = jnp.sum(ke, axis=0, keepdims=True)
            m_new = jnp.maximum(btot + m_prev, m_loc)
            sa = jnp.exp(btot + m_prev - m_new)
            sb = jnp.exp(m_loc - m_new)
            c_st[h] = sa * c_prev + sb * kv
            n_st[h:h + 1, :] = sa * n_prev + sb * nk
            m_st[h:h + 1, :] = jnp.broadcast_to(m_new, (1, LANES))
            hn = _head_norm_padded(hv, ng_ref[0:1, h * DV_PAD:(h + 1) * DV_PAD])
            og = o_ref[r0:r0 + cl, h * DV_PAD:(h + 1) * DV_PAD]
            y_ref[r0:r0 + cl, h * DV_PAD:(h + 1) * DV_PAD] = jax.nn.sigmoid(og) * hn


_ML_GATE_BLK = (2 * HEADS_QK + 2 * HEADS_V + XATTN_WIDTH) // LANES


def _mlstm_mixer(proj, bsz, seqlen, cw, gbias, norm_g, tl=ROW_TILE):
    t = proj.shape[0]
    nl = seqlen // tl
    rows = lambda b, i: b * nl + i
    return pl.pallas_call(
        functools.partial(_mlstm_kernel, tl=tl),
        out_shape=jax.ShapeDtypeStruct((t, HEADS_V), F32),
        grid=(bsz, nl),
        in_specs=[pl.BlockSpec((tl, HEADS_QK), lambda b, i: (rows(b, i), 0)),
                  pl.BlockSpec((tl, HEADS_QK), lambda b, i: (rows(b, i), 1)),
                  pl.BlockSpec((tl, HEADS_V), lambda b, i: (rows(b, i), 1)),
                  pl.BlockSpec((tl, HEADS_V), lambda b, i: (rows(b, i), 2)),
                  pl.BlockSpec((tl, LANES), lambda b, i: (rows(b, i), _ML_GATE_BLK)),
                  pl.BlockSpec((ML_CONV, 2 * HEADS_QK), lambda b, i: (0, 0)),
                  pl.BlockSpec((1, LANES), lambda b, i: (0, 0)),
                  pl.BlockSpec((1, HEADS_V), lambda b, i: (0, 0))],
        out_specs=pl.BlockSpec((tl, HEADS_V), lambda b, i: (rows(b, i), 0)),
        scratch_shapes=[pltpu.VMEM((tl + SUBLANES, 2 * HEADS_QK), F32),
                        pltpu.VMEM((N_HEADS, DQK_PAD, DV_PAD), F32),
                        pltpu.VMEM((SUBLANES, DQK_PAD), F32),
                        pltpu.VMEM((SUBLANES, LANES), F32)],
        compiler_params=_cparams(("parallel", "arbitrary")),
        name="mlstm",
    )(proj, proj, proj, proj, proj, cw, gbias, norm_g)


def _ret_log_gamma(h):
    return float(np.log(np.float32(1.0) - np.power(np.float32(2.0), np.float32(-5.0 - h))))


def _ret_kernel(q_ref, k_ref, v_ref, g_ref, pos_ref, inv_ref, sgn_ref, ng_ref,
                y_ref, s_st, *, tl):
    i = pl.program_id(1)

    @pl.when(i == 0)
    def _():
        s_st[...] = jnp.zeros_like(s_st)

    ang = pos_ref[...] * inv_ref[...]
    cos_t = jnp.cos(ang)
    sin_t = jnp.sin(ang) * sgn_ref[...]
    cl = RET_CHUNK
    row = lax.broadcasted_iota(jnp.int32, (cl, cl), 0)
    col = lax.broadcasted_iota(jnp.int32, (cl, cl), 1)
    rel = (row - col).astype(F32)
    jcol = lax.broadcasted_iota(jnp.int32, (cl, 1), 0).astype(F32)
    kscale = DQK ** -0.5
    for h in range(N_HEADS):
        lg = _ret_log_gamma(h)
        decay = jnp.where(rel >= 0, jnp.exp(jnp.maximum(rel, 0.0) * lg), 0.0)
        zeta = jnp.exp((cl - 1 - jcol) * lg)
        xi = jnp.exp((jcol + 1.0) * lg)
        chunk_decay = float(np.exp(np.float32(cl) * np.float32(lg)))
        qh = q_ref[:, h * DQK_PAD:(h + 1) * DQK_PAD]
        kh = k_ref[:, h * DQK_PAD:(h + 1) * DQK_PAD]
        qh = qh * cos_t + pltpu.roll(qh, DQK_PAD // 2, 1) * sin_t
        kh = (kh * cos_t + pltpu.roll(kh, DQK_PAD // 2, 1) * sin_t) * kscale
        for cc in range(tl // cl):
            r0 = cc * cl
            qb = qh[r0:r0 + cl].astype(BF16)
            k = kh[r0:r0 + cl]
            kb = k.astype(BF16)
            v = v_ref[r0:r0 + cl, h * DV_PAD:(h + 1) * DV_PAD]
            vb = v.astype(BF16)
            s_prev = s_st[h]
            s = lax.dot_general(qb, kb, _NT, preferred_element_type=F32) * decay
            intra = _dot(s.astype(BF16), vb)
            cross = _dot(qb, s_prev.astype(BF16)) * xi
            r = lax.dot_general((k * zeta).astype(BF16), vb, _TN, preferred_element_type=F32)
            s_st[h] = chunk_decay * s_prev + r
            hn = _head_norm_padded(intra + cross, ng_ref[0:1, h * DV_PAD:(h + 1) * DV_PAD])
            gate = g_ref[r0:r0 + cl, h * DV_PAD:(h + 1) * DV_PAD]
            y_ref[r0:r0 + cl, h * DV_PAD:(h + 1) * DV_PAD] = gate * jax.nn.sigmoid(gate) * hn


def _ret_mixer(proj, bsz, seqlen, pos_f, inv_pad, sgn_pad, norm_g, tl=ROW_TILE):
    t = proj.shape[0]
    nl = seqlen // tl
    rows = lambda b, i: b * nl + i
    return pl.pallas_call(
        functools.partial(_ret_kernel, tl=tl),
        out_shape=jax.ShapeDtypeStruct((t, HEADS_V), F32),
        grid=(bsz, nl),
        in_specs=[pl.BlockSpec((tl, HEADS_QK), lambda b, i: (rows(b, i), 0)),
                  pl.BlockSpec((tl, HEADS_QK), lambda b, i: (rows(b, i), 1)),
                  pl.BlockSpec((tl, HEADS_V), lambda b, i: (rows(b, i), 1)),
                  pl.BlockSpec((tl, HEADS_V), lambda b, i: (rows(b, i), 2)),
                  pl.BlockSpec((tl, 1), lambda b, i: (rows(b, i), 0)),
                  pl.BlockSpec((1, DQK_PAD), lambda b, i: (0, 0)),
                  pl.BlockSpec((1, DQK_PAD), lambda b, i: (0, 0)),
                  pl.BlockSpec((1, HEADS_V), lambda b, i: (0, 0))],
        out_specs=pl.BlockSpec((tl, HEADS_V), lambda b, i: (rows(b, i), 0)),
        scratch_shapes=[pltpu.VMEM((N_HEADS, DQK_PAD, DV_PAD), F32)],
        compiler_params=_cparams(("parallel", "arbitrary")),
        name="retention",
    )(proj, proj, proj, proj, pos_f, inv_pad, sgn_pad, norm_g)


def _pack_rows(x):
    half = x.shape[1] // 2
    lo = pltpu.bitcast(x[:, :half].astype(BF16).astype(F32), jnp.uint32)
    hi = pltpu.bitcast(x[:, half:].astype(BF16).astype(F32), jnp.uint32)
    return hi | (lo >> 16)


def _unpack_rows(u):
    lo = pltpu.bitcast(u << 16, F32)
    hi = pltpu.bitcast(u & jnp.uint32(0xFFFF0000), F32)
    return jnp.concatenate([lo, hi], axis=1)


def _post_kernel(ym_ref, xq_ref, h_ref, mk_ref, mv_ref, wom_ref, wox_ref, g_ref, b_ref,
                 rwh_ref, rwl_ref, rb_ref, h1_ref, h1p_ref, idx_ref, gate_ref):
    tl = xq_ref.shape[0]
    xq = xq_ref[...] * (XATTN_HEAD_DIM ** -0.5)
    lane = lax.broadcasted_iota(jnp.int32, (tl, XATTN_WIDTH), 1)
    head = lane // XATTN_HEAD_DIM
    mk = mk_ref[0]
    mv = mv_ref[0]
    ymem = jnp.zeros((tl, XATTN_WIDTH), F32)
    for hh in range(XATTN_HEADS):
        sel = head == hh
        qh = jnp.where(sel, xq, 0.0).astype(BF16)
        s = lax.dot_general(qh, mk, _NT, preferred_element_type=F32)
        s = s - jnp.max(s, axis=-1, keepdims=True)
        p = jnp.exp(s)
        p = p / jnp.sum(p, axis=-1, keepdims=True)
        ymem = jnp.where(sel, _dot(p.astype(BF16), mv), ymem)
    y = _dot(ym_ref[...].astype(BF16), wom_ref[...]) + _dot(ymem.astype(BF16), wox_ref[...])
    h1 = _layer_norm_rows(DEEPNORM_ALPHA * h_ref[...] + y, g_ref[...], b_ref[...])
    h1_ref[...] = h1
    h1p_ref[...] = _pack_rows(h1)
    h_hi = h1.astype(BF16)
    h_lo = (h1 - h_hi.astype(F32)).astype(BF16)
    logits = (_dot(h_hi, rwh_ref[...]) + _dot(h_lo, rwh_ref[...]) + _dot(h_hi, rwl_ref[...])
              + rb_ref[...])
    ln = lax.broadcasted_iota(jnp.int32, logits.shape, 1)
    vals = logits
    tv, ti = [], []
    for _ in range(TOP_K):
        m = jnp.max(vals, axis=-1, keepdims=True)
        ix = jnp.min(jnp.where(vals == m, ln, LANES), axis=-1, keepdims=True)
        tv.append(m)
        ti.append(ix)
        vals = jnp.where(ln == ix, -jnp.inf, vals)
    ex = [jnp.exp(v - tv[0]) for v in tv]
    tot = ex[0] + ex[1] + ex[2] + ex[3]
    idx_out = jnp.zeros(logits.shape, jnp.int32)
    gate_out = jnp.zeros(logits.shape, F32)
    for k in range(TOP_K):
        idx_out = jnp.where(ln == k, ti[k], idx_out)
        gate_out = jnp.where(ln == k, ex[k] / tot, gate_out)
    idx_ref[...] = idx_out
    gate_ref[...] = gate_out


def _post_mixer(y_mix, proj, xq_blk, h, mem_k, mem_v, wo_mix, wo_mem, ln_g, ln_b, rw, rb,
                seqlen, tl=4 * ROW_TILE):
    t, cm = y_mix.shape
    nl = seqlen // tl
    d = D_MODEL
    rw_hi = rw.astype(BF16)
    rw_lo = (rw - rw_hi.astype(F32)).astype(BF16)
    full = lambda a, b: pl.BlockSpec((a, b), lambda i: (0, 0))
    return pl.pallas_call(
        _post_kernel,
        out_shape=(jax.ShapeDtypeStruct((t, d), F32),
                   jax.ShapeDtypeStruct((t, d // 2), jnp.uint32),
                   jax.ShapeDtypeStruct((t, LANES), jnp.int32),
                   jax.ShapeDtypeStruct((t, LANES), F32)),
        grid=(t // tl,),
        in_specs=[pl.BlockSpec((tl, cm), lambda i: (i, 0)),
                  pl.BlockSpec((tl, XATTN_WIDTH), lambda i: (i, xq_blk)),
                  pl.BlockSpec((tl, d), lambda i: (i, 0)),
                  pl.BlockSpec((1, N_MEM, XATTN_WIDTH), lambda i: (i // nl, 0, 0)),
                  pl.BlockSpec((1, N_MEM, XATTN_WIDTH), lambda i: (i // nl, 0, 0)),
                  full(cm, d), full(XATTN_WIDTH, d), full(1, d), full(1, d),
                  full(d, LANES), full(d, LANES), full(1, LANES)],
        out_specs=(pl.BlockSpec((tl, d), lambda i: (i, 0)),
                   pl.BlockSpec((tl, d // 2), lambda i: (i, 0)),
                   pl.BlockSpec((tl, LANES), lambda i: (i, 0)),
                   pl.BlockSpec((tl, LANES), lambda i: (i, 0))),
        compiler_params=_cparams(("parallel",)),
        name="post_mixer",
    )(y_mix, proj, h, mem_k, mem_v, wo_mix, wo_mem, ln_g, ln_b, rw_hi, rw_lo, rb)


SC_CORES = 2
SC_SUBCORES = 16
SC_WORKERS = SC_CORES * SC_SUBCORES
SC_CHUNK = 64


def _sc_gather(table, idx):
    v, d = table.shape
    b = idx.shape[0]
    per_w = b // SC_WORKERS
    n_chunks = per_w // SC_CHUNK
    assert per_w * SC_WORKERS == b and n_chunks * SC_CHUNK == per_w and n_chunks % 2 == 0
    mesh = plsc.VectorSubcoreMesh(core_axis_name="c", subcore_axis_name="s")

    @functools.partial(
        pl.kernel, mesh=mesh,
        out_type=jax.ShapeDtypeStruct((b, d), table.dtype),
        scratch_types=[pltpu.VMEM((per_w,), jnp.int32),
                       pltpu.VMEM((SC_CHUNK, d), table.dtype),
                       pltpu.VMEM((SC_CHUNK, d), table.dtype),
                       pltpu.SemaphoreType.DMA,
                       pltpu.SemaphoreType.DMA],
    )
    def gather_kernel(table_hbm, idx_hbm, out_hbm, idx_v, rows0, rows1, sem0, sem1):
        wid = lax.axis_index("s") * SC_CORES + lax.axis_index("c")
        base = wid * per_w
        pltpu.sync_copy(idx_hbm.at[pl.ds(pl.multiple_of(base, 8), per_w)], idx_v)
        ring = ((rows0, sem0), (rows1, sem1))

        def gather(c, buf, sem):
            rows = idx_v.at[pl.ds(pl.multiple_of(c * SC_CHUNK, 8), SC_CHUNK)]
            return pltpu.make_async_copy(table_hbm.at[rows], buf, sem)

        for c0, (buf, sem) in enumerate(ring):
            gather(c0, buf, sem).start()

        @pl.loop(0, n_chunks, step=2)
        def _(c):
            for k, (buf, sem) in enumerate(ring):
                cc = c + k
                gather(cc, buf, sem).wait()
                off = pl.multiple_of(base + cc * SC_CHUNK, 8)
                pltpu.sync_copy(buf, out_hbm.at[pl.ds(off, SC_CHUNK)])

                @pl.when(cc + 2 < n_chunks)
                def _():
                    gather(cc + 2, buf, sem).start()

    return gather_kernel(table, idx)


def _sc_dispatch(x, pos_kmajor, n_rows):
    t, d = x.shape
    per_w = t // SC_WORKERS
    n_chunks = per_w // SC_CHUNK
    assert per_w * SC_WORKERS == t and n_chunks * SC_CHUNK == per_w
    mesh = plsc.VectorSubcoreMesh(core_axis_name="c", subcore_axis_name="s")

    @functools.partial(
        pl.kernel, mesh=mesh,
        out_type=jax.ShapeDtypeStruct((n_rows, d), x.dtype),
        scratch_types=[pltpu.VMEM((SC_CHUNK,), jnp.int32),
                       pltpu.VMEM((SC_CHUNK, d), x.dtype)],
    )
    def dispatch_kernel(x_hbm, pos_hbm, out_hbm, idx_v, rows_v):
        wid = lax.axis_index("s") * SC_CORES + lax.axis_index("c")
        base = wid * per_w

        @pl.loop(0, n_chunks)
        def _(j):
            off = pl.multiple_of(base + j * SC_CHUNK, 8)
            pltpu.sync_copy(x_hbm.at[pl.ds(off, SC_CHUNK)], rows_v)
            for k in range(TOP_K):
                pltpu.sync_copy(pos_hbm.at[pl.ds(pl.multiple_of(k * t + off, 8), SC_CHUNK)], idx_v)
                pltpu.sync_copy(rows_v, out_hbm.at[idx_v])

    return dispatch_kernel(x, pos_kmajor)


def _expert_kernel(bexp_ref, nvalid_ref, nused_ref, x_ref, wgu_ref, bgu_ref, wd_ref, bd_ref, y_ref,
                   wgu_bf, wd_bf):
    i = pl.program_id(0)
    de = wd_ref.shape[2]

    @pl.when(i < nused_ref[0])
    def _():
        prev = bexp_ref[jnp.maximum(i - 1, 0)]

        @pl.when((i == 0) | (prev != bexp_ref[i]))
        def _():
            wgu_bf[...] = wgu_ref[0, 0].astype(BF16)
            wd_bf[...] = wd_ref[0, 0].astype(BF16)

        rows = lax.broadcasted_iota(jnp.int32, x_ref.shape, 0)
        xb = _unpack_rows(jnp.where(rows < nvalid_ref[i], x_ref[...], jnp.uint32(0))).astype(BF16)
        gu = _dot(xb, wgu_bf[...]) + bgu_ref[0, 0]
        x_glu = jnp.minimum(gu[:, :de], SWIGLU_LIMIT)
        x_lin = jnp.clip(gu[:, de:], -SWIGLU_LIMIT, SWIGLU_LIMIT)
        act = x_glu * jax.nn.sigmoid(SWIGLU_ALPHA * x_glu) * (x_lin + 1.0)
        y_ref[...] = _pack_rows(_dot(act.astype(BF16), wd_bf[...]) + bd_ref[0, 0])

    @pl.when(i >= nused_ref[0])
    def _():
        y_ref[...] = jnp.zeros_like(y_ref)


def _experts(xr, block_exp, n_valid, n_used, layer, w_gu, b_gu, w_down, b_down, bm=MOE_BM):
    n_rows, dp = xr.shape
    d = 2 * dp
    n_blocks = n_rows // bm
    nl, ne, _, de2 = w_gu.shape
    de = de2 // 2
    row_blk = lambda i, nu: jnp.minimum(i, nu[0] - 1)
    grid_spec = pltpu.PrefetchScalarGridSpec(
        num_scalar_prefetch=3,
        grid=(n_blocks,),
        in_specs=[pl.BlockSpec((bm, dp), lambda i, be, nv, nu: (row_blk(i, nu), 0)),
                  pl.BlockSpec((1, 1, d, de2), lambda i, be, nv, nu: (layer, be[i], 0, 0)),
                  pl.BlockSpec((1, 1, 1, de2), lambda i, be, nv, nu: (layer, be[i], 0, 0)),
                  pl.BlockSpec((1, 1, de, d), lambda i, be, nv, nu: (layer, be[i], 0, 0)),
                  pl.BlockSpec((1, 1, 1, d), lambda i, be, nv, nu: (layer, be[i], 0, 0))],
        out_specs=pl.BlockSpec((bm, dp), lambda i, be, nv, nu: (i, 0)),
        scratch_shapes=[pltpu.VMEM((d, de2), BF16),
                        pltpu.VMEM((de, d), BF16)],
    )
    return pl.pallas_call(
        _expert_kernel,
        out_shape=jax.ShapeDtypeStruct((n_rows, dp), jnp.uint32),
        grid_spec=grid_spec,
        compiler_params=_cparams(("arbitrary",)),
        name="experts",
    )(block_exp, n_valid, n_used, xr, w_gu, b_gu.reshape(nl, ne, 1, de2), w_down,
      b_down.reshape(nl, ne, 1, d))


def _combine_kernel(y0_ref, y1_ref, y2_ref, y3_ref, h1_ref, gate_ref, g_ref, b_ref, o_ref):
    gate = gate_ref[...]
    acc = DEEPNORM_ALPHA * h1_ref[...]
    for k, y_ref in enumerate((y0_ref, y1_ref, y2_ref, y3_ref)):
        acc = acc + gate[:, k:k + 1] * _unpack_rows(y_ref[0])
    o_ref[...] = _layer_norm_rows(acc, g_ref[...], b_ref[...])


def _combine(yg, h1, gates, ln_g, ln_b, tl=ROW_TILE):
    t, d = h1.shape
    ysel = lambda k: pl.BlockSpec((1, tl, d // 2), lambda i: (k, i, 0))
    return pl.pallas_call(
        _combine_kernel,
        out_shape=jax.ShapeDtypeStruct((t, d), F32),
        grid=(t // tl,),
        in_specs=[ysel(0), ysel(1), ysel(2), ysel(3),
                  pl.BlockSpec((tl, d), lambda i: (i, 0)),
                  pl.BlockSpec((tl, LANES), lambda i: (i, 0)),
                  pl.BlockSpec((1, d), lambda i: (0, 0)),
                  pl.BlockSpec((1, d), lambda i: (0, 0))],
        out_specs=pl.BlockSpec((tl, d), lambda i: (i, 0)),
        compiler_params=_cparams(("parallel",)),
        name="combine",
    )(yg, yg, yg, yg, h1, gates, ln_g, ln_b)


def _route(idx, bm=MOE_BM):
    t = idx.shape[0]
    n_assign = t * TOP_K
    n_rows = (-(-n_assign // bm) + N_EXPERTS) * bm
    n_blocks = n_rows // bm
    onehot = (idx[:, :, None] == jnp.arange(N_EXPERTS, dtype=jnp.int32)[None, None, :])
    sel = jnp.sum(onehot.astype(jnp.int32), axis=1)
    csum = jnp.cumsum(sel, axis=0)
    counts = csum[-1]
    rank = csum - sel
    padded = ((counts + bm - 1) // bm) * bm
    pad_end = jnp.cumsum(padded)
    pad_start = pad_end - padded
    base = pad_start[None, :] + rank
    pos = jnp.sum(jnp.where(onehot, base[:, None, :], 0), axis=2)
    block_start = jnp.arange(n_blocks, dtype=jnp.int32) * bm
    block_exp = jnp.minimum(
        jnp.sum((block_start[:, None] >= pad_end[None, :]).astype(jnp.int32), axis=1),
        N_EXPERTS - 1).astype(jnp.int32)
    own = block_exp[:, None] == jnp.arange(N_EXPERTS, dtype=jnp.int32)[None, :]
    filled_end = jnp.sum(jnp.where(own, (pad_start + counts)[None, :], 0), axis=1)
    n_valid = jnp.clip(filled_end - block_start, 0, bm).astype(jnp.int32)
    n_used = (pad_end[-1] // bm).astype(jnp.int32).reshape(1)
    return pos.T.reshape(-1).astype(jnp.int32), block_exp, n_valid, n_used, n_rows


def _take_cols(w, cols):
    cols = list(cols)
    parts, i = [], 0
    while i < len(cols):
        j = i
        if cols[i] < 0:
            while j < len(cols) and cols[j] < 0:
                j += 1
            parts.append(jnp.zeros(w.shape[:-1] + (j - i,), w.dtype))
        else:
            while j + 1 < len(cols) and cols[j + 1] == cols[j] + 1:
                j += 1
            j += 1
            parts.append(w[..., cols[i]:cols[i] + (j - i)])
        i = j
    return jnp.concatenate(parts, axis=-1)


def _head_cols(offset, width, pad):
    cols = []
    for h in range(N_HEADS):
        cols += list(range(offset + h * width, offset + (h + 1) * width)) + [-1] * (pad - width)
    return cols


def _rope_head_cols(offset):
    half = DQK // 2
    slot = DQK_PAD // 2
    cols = []
    for h in range(N_HEADS):
        b = offset + h * DQK
        cols += list(range(b, b + half)) + [-1] * (slot - half)
        cols += list(range(b + half, b + DQK)) + [-1] * (slot - half)
    return cols


_ML_GATE_OFF = 2 * ML_QK + 2 * MIX_WIDTH
_ML_COLS = (_head_cols(0, DQK, DQK_PAD) + _head_cols(ML_QK, DQK, DQK_PAD)
            + _head_cols(2 * ML_QK, DV, DV_PAD) + _head_cols(2 * ML_QK + MIX_WIDTH, DV, DV_PAD)
            + list(range(_ML_GATE_OFF + 2 * N_HEADS, _ML_GATE_OFF + 2 * N_HEADS + XATTN_WIDTH))
            + list(range(_ML_GATE_OFF, _ML_GATE_OFF + 2 * N_HEADS)) + [-1] * (LANES - 2 * N_HEADS))
_RET_COLS = (_rope_head_cols(0) + _rope_head_cols(ML_QK)
             + _head_cols(2 * ML_QK, DV, DV_PAD) + _head_cols(2 * ML_QK + MIX_WIDTH, DV, DV_PAD)
             + list(range(2 * ML_QK + 2 * MIX_WIDTH, 2 * ML_QK + 2 * MIX_WIDTH + XATTN_WIDTH)))
_MIX_PAD_COLS = _head_cols(0, DV, DV_PAD)
_XQ_BLK_PADDED = (2 * HEADS_QK + 2 * HEADS_V) // XATTN_WIDTH
_XQ_BLK_S5 = MIX_WIDTH // XATTN_WIDTH


def kernel(x, mem, positions, mem_w_k, mem_w_v, l0_w_in, l0_s5_a_re, l0_s5_a_im, l0_s5_log_dt, l0_s5_b_re, l0_s5_b_im, l0_s5_c_re, l0_s5_c_im, l0_s5_d, l0_s5_w_glu, l0_s5_b_glu, l1_w_in, l1_ml_conv_q, l1_ml_conv_k, l1_ml_b_i, l1_ml_b_f, l1_ml_norm_g, l2_w_in, l2_ret_norm_g, l3_w_in, l3_s5_a_re, l3_s5_a_im, l3_s5_log_dt, l3_s5_b_re, l3_s5_b_im, l3_s5_c_re, l3_s5_c_im, l3_s5_d, l3_s5_w_glu, l3_s5_b_glu, w_out, ln1_g, ln1_b, ln2_g, ln2_b, router_w, router_b, exp_w_gu, exp_b_gu, exp_w_down, exp_b_down):
    bsz, seqlen, d = x.shape
    t = bsz * seqlen
    h = x.reshape(t, d)

    w_kv = jnp.concatenate([mem_w_k, mem_w_v], axis=1).astype(BF16)
    kv = _inproj(mem.reshape(bsz * N_MEM, d), w_kv).astype(BF16)
    mem_k = kv[:, :XATTN_WIDTH].reshape(bsz, N_MEM, XATTN_WIDTH)
    mem_v = kv[:, XATTN_WIDTH:].reshape(bsz, N_MEM, XATTN_WIDTH)

    half = DQK // 2
    inv = ROPE_BASE ** (-jnp.arange(0, DQK, 2, dtype=F32) / DQK)
    zpad = jnp.zeros((DQK_PAD // 2 - half,), F32)
    inv_pad = jnp.concatenate([inv, zpad, inv, zpad]).reshape(1, DQK_PAD)
    sgn_pad = jnp.concatenate([-jnp.ones((half,), F32), zpad, jnp.ones((half,), F32), zpad]
                              ).reshape(1, DQK_PAD)
    pos_f = positions.astype(F32).reshape(t, 1)

    s5_params = {
        0: (l0_s5_a_re, l0_s5_a_im, l0_s5_log_dt, l0_s5_b_re, l0_s5_b_im, l0_s5_c_re, l0_s5_c_im,
            l0_s5_d, l0_s5_w_glu, l0_s5_b_glu),
        3: (l3_s5_a_re, l3_s5_a_im, l3_s5_log_dt, l3_s5_b_re, l3_s5_b_im, l3_s5_c_re, l3_s5_c_im,
            l3_s5_d, l3_s5_w_glu, l3_s5_b_glu),
    }
    w_ins = (l0_w_in, l1_w_in, l2_w_in, l3_w_in)

    for i in range(DEPTH):
        kind = i % 3
        wo = w_out[i]
        if kind == 0:
            proj = _inproj(h, w_ins[i].astype(BF16))
            y_mix = _s5_mixer(proj, bsz, seqlen, *s5_params[i])
            wo_mix = wo[:MIX_WIDTH].astype(BF16)
            xq_blk = _XQ_BLK_S5
        elif kind == 1:
            proj = _inproj(h, _take_cols(w_ins[i], _ML_COLS).astype(BF16))
            cw = jnp.concatenate([_take_cols(l1_ml_conv_q, _head_cols(0, DQK, DQK_PAD)),
                                  _take_cols(l1_ml_conv_k, _head_cols(0, DQK, DQK_PAD))], axis=1)
            gbias = jnp.concatenate([l1_ml_b_i, l1_ml_b_f,
                                     jnp.zeros((LANES - 2 * N_HEADS,), F32)]).reshape(1, LANES)
            norm_g = _take_cols(l1_ml_norm_g, _MIX_PAD_COLS).reshape(1, HEADS_V)
            y_mix = _mlstm_mixer(proj, bsz, seqlen, cw, gbias, norm_g)
            wo_mix = _take_cols(wo[:MIX_WIDTH].T, _MIX_PAD_COLS).T.astype(BF16)
            xq_blk = _XQ_BLK_PADDED
        else:
            proj = _inproj(h, _take_cols(w_ins[i], _RET_COLS).astype(BF16))
            norm_g = _take_cols(l2_ret_norm_g, _MIX_PAD_COLS).reshape(1, HEADS_V)
            y_mix = _ret_mixer(proj, bsz, seqlen, pos_f, inv_pad, sgn_pad, norm_g)
            wo_mix = _take_cols(wo[:MIX_WIDTH].T, _MIX_PAD_COLS).T.astype(BF16)
            xq_blk = _XQ_BLK_PADDED
        wo_mem = wo[MIX_WIDTH:].astype(BF16)
        rw = jnp.pad(router_w[i], ((0, 0), (0, LANES - N_EXPERTS)))
        rb = jnp.concatenate([router_b[i], jnp.full((LANES - N_EXPERTS,), -1e30, F32)]
                             ).reshape(1, LANES)
        h1, h1p, idx, gates = _post_mixer(y_mix, proj, xq_blk, h, mem_k, mem_v, wo_mix, wo_mem,
                                     ln1_g[i].reshape(1, d), ln1_b[i].reshape(1, d), rw, rb, seqlen)
        pos_kmajor, block_exp, n_valid, n_used, n_rows = _route(idx[:, :TOP_K])
        xr = _sc_dispatch(h1p, pos_kmajor, n_rows)
        yr = _experts(xr, block_exp, n_valid, n_used, i, exp_w_gu, exp_b_gu, exp_w_down, exp_b_down)
        yg = _sc_gather(yr, pos_kmajor).reshape(TOP_K, t, d // 2)
        h = _combine(yg, h1, gates, ln2_g[i].reshape(1, d), ln2_b[i].reshape(1, d))
    return h.reshape(bsz, seqlen, d)
```
